```python
import math
import numpy as np
import jax, jax.numpy as jnp
from jax import lax

D_MODEL = 4096
BATCH = 4
SEQ = 2048
DEPTH = 2
DEC_BATCH = 8
DEC_SEQ = 4
PAST_LEN = 16384
PAGE_SIZE = 128

HEAD_DIM = 128
N_HEADS = (D_MODEL // 2) // HEAD_DIM
N_KV_HEADS = 4
Q_W = N_HEADS * HEAD_DIM
KV_W = N_KV_HEADS * HEAD_DIM
IDX_HEADS = 32
IDX_DIM = 128
IDX_Q_RANK = 768
IDX_W_SCALE = (IDX_HEADS * IDX_DIM) ** -0.5
TOPK_MAX = 256
Q_BLOCK = 128
LRU_WIDTH = D_MODEL // 2
LRU_HEADS = 16
LRU_BLOCK = LRU_WIDTH // LRU_HEADS
CONV_W = 4
LRU_C = 8.0
S5_GROUP = 16
S5_GROUPS = D_MODEL // S5_GROUP
S5_STATE = 64
S5_CHUNK = 128
N_EXPERTS = 32
TOP_K = 4
D_FF = D_MODEL // 4
SWIGLU_LIMIT = 7.0
SWIGLU_ALPHA = 1.702
LN_EPS = 1e-5
DN_ALPHA = (2 * DEPTH) ** 0.25
DN_BETA = (8 * DEPTH) ** -0.25
N_EVEN = (DEPTH + 1) // 2
N_ODD = DEPTH // 2
IN_W = Q_W + 2 * KV_W + IDX_Q_RANK + IDX_HEADS + IDX_DIM + 2 * LRU_WIDTH

kernel_name = 'hybrid_dsa_rglru_s5_moe_step'


def _layer_norm(x, g, b):
    xf = x.astype(jnp.float32)
    mu = jnp.mean(xf, axis=-1, keepdims=True)
    var = jnp.mean(jnp.square(xf - mu), axis=-1, keepdims=True)
    y = (xf - mu) * lax.rsqrt(var + LN_EPS) * g.astype(jnp.float32) + b.astype(jnp.float32)
    return y.astype(x.dtype)


def _rows(a, i):
    return jax.vmap(lambda aa, ii: aa[ii])(a, i)


def _even_project(x, w_in, w_idx_qb, kidx_g, kidx_b):
    b, t, _ = x.shape
    sizes = [Q_W, KV_W, KV_W, IDX_Q_RANK, IDX_HEADS, IDX_DIM, LRU_WIDTH, LRU_WIDTH]
    cuts = np.cumsum(sizes)[:-1].tolist()
    q, k, v, cq, wi, ki, u, g = jnp.split(x @ w_in, cuts, axis=-1)
    q = q.reshape(b, t, N_HEADS, HEAD_DIM)
    k = k.reshape(b, t, N_KV_HEADS, HEAD_DIM)
    v = v.reshape(b, t, N_KV_HEADS, HEAD_DIM)
    q_idx = (cq @ w_idx_qb).reshape(b, t, IDX_HEADS, IDX_DIM)
    w_idx = wi * IDX_W_SCALE
    k_idx = _layer_norm(ki, kidx_g, kidx_b)
    return q, k, v, q_idx, w_idx, k_idx, u, g


def _index_scores(q_idx, w_idx, k_idx, kpos, qpos):
    dots = jnp.einsum('bthd,bsd->btsh', q_idx, k_idx)
    sc = jnp.einsum('btsh,bth->bts', jax.nn.relu(dots), w_idx).astype(jnp.float32)
    return jnp.where(kpos[None, None, :] <= qpos[None, :, None], sc, -jnp.inf)


def _sparse_attend(q, kg, vg, valid):
    b, t, h, d = q.shape
    qg = q.reshape(b, t, N_KV_HEADS, h // N_KV_HEADS, d)
    s = jnp.einsum('btkgd,btnkd->btkgn', qg, kg).astype(jnp.float32) * (d ** -0.5)
    s = jnp.where(valid[:, :, None, None, :], s, -jnp.inf)
    p = jax.nn.softmax(s, axis=-1).astype(vg.dtype)
    o = jnp.einsum('btkgn,btnkd->btkgd', p, vg)
    return o.reshape(b, t, h * d)


def _dsa_prompt(q, k, v, q_idx, w_idx, k_idx):
    b, s = q.shape[0], q.shape[1]
    topk = min(TOPK_MAX, s // 4)
    qb = min(Q_BLOCK, s)
    nb = s // qb
    kpos = jnp.arange(s)

    def to_blocks(a):
        return jnp.moveaxis(a.reshape((b, nb, qb) + a.shape[2:]), 1, 0)

    def block(args):
        q_b, qi_b, wi_b, qpos = args
        sc = _index_scores(qi_b, wi_b, k_idx, kpos, qpos)
        idx = lax.top_k(sc, topk)[1]
        valid = idx <= qpos[None, :, None]
        return _sparse_attend(q_b, _rows(k, idx), _rows(v, idx), valid)

    out = lax.map(block, (to_blocks(q), to_blocks(q_idx), to_blocks(w_idx), kpos.reshape(nb, qb)))
    return jnp.moveaxis(out, 0, 1).reshape(b, s, Q_W)


def _dsa_sample(q, k, v, q_idx, w_idx, k_idx, cache_k, cache_v, cache_kidx, page_table):
    b, t = q.shape[0], q.shape[1]
    n_pages = page_table.shape[1]
    past = n_pages * PAGE_SIZE
    topk = min(TOPK_MAX, (past + t) // 4)
    kidx_past = cache_kidx[page_table].reshape(b, past, IDX_DIM)
    kidx_all = jnp.concatenate([kidx_past.astype(k_idx.dtype), k_idx], axis=1)
    qpos = past + jnp.arange(t)
    sc = _index_scores(q_idx, w_idx, kidx_all, jnp.arange(past + t), qpos)
    idx = lax.top_k(sc, topk)[1]
    in_past = (idx < past)[..., None, None]
    page = jnp.minimum(idx // PAGE_SIZE, n_pages - 1)
    phys = jnp.take_along_axis(page_table, page.reshape(b, -1), axis=1).reshape(idx.shape)
    off = idx % PAGE_SIZE
    loc = jnp.clip(idx - past, 0, t - 1)
    kg = jnp.where(in_past, cache_k[phys, off].astype(k.dtype), _rows(k, loc))
    vg = jnp.where(in_past, cache_v[phys, off].astype(v.dtype), _rows(v, loc))
    valid = idx <= qpos[None, :, None]
    return _sparse_attend(q, kg, vg, valid)


def _rglru_branch(u, g, conv_prev, h0, conv_w, conv_b, wa, ba, wx, bx, lam):
    b, t, w = u.shape
    up = jnp.concatenate([conv_prev.astype(u.dtype), u], axis=1)
    xc = conv_b + sum(conv_w[j] * up[:, j:j + t] for j in range(CONV_W))
    xh = xc.reshape(b, t, LRU_HEADS, LRU_BLOCK)
    r = jax.nn.sigmoid(jnp.einsum('bthi,hij->bthj', xh, wa).reshape(b, t, w) + ba)
    i = jax.nn.sigmoid(jnp.einsum('bthi,hij->bthj', xh, wx).reshape(b, t, w) + bx)
    log_a = (-LRU_C * r * jax.nn.softplus(-lam)).astype(jnp.float32)
    a = jnp.exp(log_a)
    bt = jnp.sqrt(-jnp.expm1(2.0 * log_a)) * (i * xc).astype(jnp.float32)
    bt = bt.at[:, 0].add(a[:, 0] * h0.astype(jnp.float32))

    def comb(e1, e2):
        return (e1[0] * e2[0], e2[0] * e1[1] + e2[1])

    h = lax.associative_scan(comb, (a, bt), axis=1)[1]
    y = h.astype(u.dtype) * jax.nn.gelu(g)
    return y, up[:, t:], h[:, -1]


def _s5_scan(u, h0_re, h0_im, a_re, a_im, log_step, b_re, b_im, c_re, c_im, d):
    f32 = jnp.float32
    bsz, t, dm = u.shape
    a_re, a_im = a_re.astype(f32), a_im.astype(f32)
    b_re, b_im, c_re, c_im = b_re.astype(f32), b_im.astype(f32), c_re.astype(f32), c_im.astype(f32)
    dt = jnp.exp(log_step.astype(f32))[:, None]
    mag = jnp.exp(a_re * dt)
    ab_re, ab_im = mag * jnp.cos(a_im * dt), mag * jnp.sin(a_im * dt)
    den = a_re * a_re + a_im * a_im
    nr, ni = ab_re - 1.0, ab_im
    cf_re = (nr * a_re + ni * a_im) / den
    cf_im = (ni * a_re - nr * a_im) / den
    bb_re = cf_re[..., None] * b_re - cf_im[..., None] * b_im
    bb_im = cf_re[..., None] * b_im + cf_im[..., None] * b_re
    uf = u.astype(f32).reshape(bsz, t, S5_GROUPS, S5_GROUP)
    tc = S5_CHUNK if t % S5_CHUNK == 0 else t
    nc = t // tc
    u_chunks = uf.reshape(bsz, nc, tc, S5_GROUPS, S5_GROUP).transpose(1, 2, 0, 3, 4)
    a_seq_re = jnp.broadcast_to(ab_re, (tc, bsz) + ab_re.shape)
    a_seq_im = jnp.broadcast_to(ab_im, (tc, bsz) + ab_im.shape)

    def comb(e1, e2):
        a1r, a1i, b1r, b1i = e1
        a2r, a2i, b2r, b2i = e2
        return (a2r * a1r - a2i * a1i, a2r * a1i + a2i * a1r,
                a2r * b1r - a2i * b1i + b2r, a2r * b1i + a2i * b1r + b2i)

    def step(carry, uc):
        hr, hi = carry
        br = jnp.einsum('gnc,tbgc->tbgn', bb_re, uc)
        bi = jnp.einsum('gnc,tbgc->tbgn', bb_im, uc)
        pr, pi, xr, xi = lax.associative_scan(comb, (a_seq_re, a_seq_im, br, bi), axis=0)
        xr, xi = xr + pr * hr - pi * hi, xi + pr * hi + pi * hr
        y = jnp.einsum('gcn,tbgn->tbgc', c_re, xr) - jnp.einsum('gcn,tbgn->tbgc', c_im, xi)
        return (xr[-1], xi[-1]), y

    (hr, hi), ys = lax.scan(step, (h0_re.astype(f32), h0_im.astype(f32)), u_chunks)
    y = jnp.moveaxis(ys, 2, 0).reshape(bsz, t, dm)
    return (y + d.astype(f32) * u.astype(f32)).astype(u.dtype), hr, hi


def _s5_block(x, h0_re, h0_im, a_re, a_im, log_step, b_re, b_im, c_re, c_im, d, w1, b1, w2, b2):
    y, hr, hi = _s5_scan(x, h0_re, h0_im, a_re, a_im, log_step, b_re, b_im, c_re, c_im, d)
    z = jax.nn.gelu(y)
    return (z @ w1 + b1) * jax.nn.sigmoid(z @ w2 + b2), hr, hi


def _moe(x, rw, rb, w_gu, b_gu, w_dn, b_dn):
    logits = (x @ rw + rb).astype(jnp.float32)
    top_v, top_i = lax.top_k(logits, TOP_K)
    gates = jax.nn.softmax(top_v, axis=-1)
    dense = jnp.sum(jax.nn.one_hot(top_i, N_EXPERTS, dtype=jnp.float32) * gates[..., None], axis=1)
    out = jnp.zeros(x.shape, jnp.float32)
    for e in range(N_EXPERTS):
        h = x @ w_gu[e] + b_gu[e]
        gate = jnp.minimum(h[:, :D_FF], SWIGLU_LIMIT)
        up = jnp.clip(h[:, D_FF:], -SWIGLU_LIMIT, SWIGLU_LIMIT)
        act = (up + 1.0) * gate * jax.nn.sigmoid(SWIGLU_ALPHA * gate)
        out = out + dense[:, e:e + 1] * (act @ w_dn[e] + b_dn[e]).astype(jnp.float32)
    return out.astype(x.dtype)


def setup_inputs(seed: int = 0) -> dict:
    key = jax.random.key(seed)
    ks = iter(jax.random.split(key, 64))

    def nrm(shape, scale):
        return jax.random.normal(next(ks), shape, jnp.float32) * scale

    n_pages = PAST_LEN // PAGE_SIZE
    n_used = DEC_BATCH * n_pages
    n_pool = n_used + max(1, n_used // 4)
    x_prompt = nrm((BATCH, SEQ, D_MODEL), 1.0)
    x_sample = nrm((DEC_BATCH, DEC_SEQ, D_MODEL), 1.0)
    cache_k = nrm((N_EVEN, n_pool, PAGE_SIZE, N_KV_HEADS, HEAD_DIM), 1.0)
    cache_v = nrm((N_EVEN, n_pool, PAGE_SIZE, N_KV_HEADS, HEAD_DIM), 1.0)
    cache_kidx = nrm((N_EVEN, n_pool, PAGE_SIZE, IDX_DIM), 1.0)
    state_conv = nrm((N_EVEN, DEC_BATCH, CONV_W - 1, LRU_WIDTH), 1.0)
    state_lru = nrm((N_EVEN, DEC_BATCH, LRU_WIDTH), 0.5)
    state_s5_re = nrm((N_ODD, DEC_BATCH, S5_GROUPS, S5_STATE), 0.3)
    state_s5_im = nrm((N_ODD, DEC_BATCH, S5_GROUPS, S5_STATE), 0.3)
    page_table = jax.random.permutation(next(ks), n_pool)[:n_used].reshape(DEC_BATCH, n_pages).astype(jnp.int32)

    w_in = nrm((N_EVEN, D_MODEL, IN_W), D_MODEL ** -0.5)
    w_idx_qb = nrm((N_EVEN, IDX_Q_RANK, IDX_HEADS * IDX_DIM), IDX_Q_RANK ** -0.5)
    kidx_ln_g = 1.0 + nrm((N_EVEN, IDX_DIM), 0.02)
    kidx_ln_b = nrm((N_EVEN, IDX_DIM), 0.02)
    conv_w = nrm((N_EVEN, CONV_W, LRU_WIDTH), CONV_W ** -0.5)
    conv_b = nrm((N_EVEN, LRU_WIDTH), 0.02)
    lru_wa = nrm((N_EVEN, LRU_HEADS, LRU_BLOCK, LRU_BLOCK), LRU_BLOCK ** -0.5)
    lru_ba = nrm((N_EVEN, LRU_WIDTH), 0.02)
    lru_wx = nrm((N_EVEN, LRU_HEADS, LRU_BLOCK, LRU_BLOCK), LRU_BLOCK ** -0.5)
    lru_bx = nrm((N_EVEN, LRU_WIDTH), 0.02)
    a_pow = jax.random.uniform(next(ks), (N_EVEN, LRU_WIDTH), jnp.float32, 0.9, 0.999)
    p = a_pow ** (1.0 / LRU_C)
    lru_lambda = jnp.log(p) - jnp.log1p(-p)
    w_out = nrm((N_EVEN, Q_W + LRU_WIDTH, D_MODEL), (Q_W + LRU_WIDTH) ** -0.5 * DN_BETA)

    n_idx = jnp.arange(S5_STATE, dtype=jnp.float32)
    s5_a_re = -0.5 + nrm((N_ODD, S5_GROUPS, S5_STATE), 1e-3)
    s5_a_im = math.pi * n_idx + nrm((N_ODD, S5_GROUPS, S5_STATE), 1e-3)
    s5_log_step = jax.random.uniform(next(ks), (N_ODD, S5_GROUPS), jnp.float32, math.log(1e-3), math.log(1e-1))
    s5_b_re = nrm((N_ODD, S5_GROUPS, S5_STATE, S5_GROUP), S5_GROUP ** -0.5)
    s5_b_im = nrm((N_ODD, S5_GROUPS, S5_STATE, S5_GROUP), S5_GROUP ** -0.5)
    s5_c_re = nrm((N_ODD, S5_GROUPS, S5_GROUP, S5_STATE), S5_STATE ** -0.5)
    s5_c_im = nrm((N_ODD, S5_GROUPS, S5_GROUP, S5_STATE), S5_STATE ** -0.5)
    s5_d = nrm((N_ODD, D_MODEL), 1.0)
    glu_w1 = nrm((N_ODD, D_MODEL, D_MODEL), D_MODEL ** -0.5 * DN_BETA)
    glu_b1 = nrm((N_ODD, D_MODEL), 0.02)
    glu_w2 = nrm((N_ODD, D_MODEL, D_MODEL), D_MODEL ** -0.5)
    glu_b2 = nrm((N_ODD, D_MODEL), 0.02)

    ln1_g = 1.0 + nrm((DEPTH, D_MODEL), 0.02)
    ln1_b = nrm((DEPTH, D_MODEL), 0.02)
    ln2_g = 1.0 + nrm((DEPTH, D_MODEL), 0.02)
    ln2_b = nrm((DEPTH, D_MODEL), 0.02)

    router_w = nrm((DEPTH, D_MODEL, N_EXPERTS), D_MODEL ** -0.5)
    router_b = nrm((DEPTH, N_EXPERTS), 0.01)
    moe_w_gu = nrm((DEPTH, N_EXPERTS, D_MODEL, 2 * D_FF), D_MODEL ** -0.5)
    moe_b_gu = nrm((DEPTH, N_EXPERTS, 2 * D_FF), 0.01)
    moe_w_down = nrm((DEPTH, N_EXPERTS, D_FF, D_MODEL), D_FF ** -0.5 * DN_BETA)
    moe_b_down = nrm((DEPTH, N_EXPERTS, D_MODEL), 0.01)

    return {'x_prompt': x_prompt, 'x_sample': x_sample,
            'cache_k': cache_k, 'cache_v': cache_v, 'cache_kidx': cache_kidx,
            'state_conv': state_conv, 'state_lru': state_lru,
            'state_s5_re': state_s5_re, 'state_s5_im': state_s5_im,
            'page_table': page_table,
            'w_in': w_in, 'w_idx_qb': w_idx_qb, 'kidx_ln_g': kidx_ln_g, 'kidx_ln_b': kidx_ln_b,
            'conv_w': conv_w, 'conv_b': conv_b, 'lru_wa': lru_wa, 'lru_ba': lru_ba,
            'lru_wx': lru_wx, 'lru_bx': lru_bx, 'lru_lambda': lru_lambda, 'w_out': w_out,
            's5_a_re': s5_a_re, 's5_a_im': s5_a_im, 's5_log_step': s5_log_step,
            's5_b_re': s5_b_re, 's5_b_im': s5_b_im, 's5_c_re': s5_c_re, 's5_c_im': s5_c_im, 's5_d': s5_d,
            'glu_w1': glu_w1, 'glu_b1': glu_b1, 'glu_w2': glu_w2, 'glu_b2': glu_b2,
            'ln1_g': ln1_g, 'ln1_b': ln1_b, 'ln2_g': ln2_g, 'ln2_b': ln2_b,
            'router_w': router_w, 'router_b': router_b, 'moe_w_gu': moe_w_gu, 'moe_b_gu': moe_b_gu,
            'moe_w_down': moe_w_down, 'moe_b_down': moe_b_down}


def reference(x_prompt, x_sample, cache_k, cache_v, cache_kidx, state_conv, state_lru,
              state_s5_re, state_s5_im, page_table,
              w_in, w_idx_qb, kidx_ln_g, kidx_ln_b, conv_w, conv_b, lru_wa, lru_ba,
              lru_wx, lru_bx, lru_lambda, w_out,
              s5_a_re, s5_a_im, s5_log_step, s5_b_re, s5_b_im, s5_c_re, s5_c_im, s5_d,
              glu_w1, glu_b1, glu_w2, glu_b2,
              ln1_g, ln1_b, ln2_g, ln2_b,
              router_w, router_b, moe_w_gu, moe_b_gu, moe_w_down, moe_b_down):
    bp, sp, dm = x_prompt.shape
    bs, ss, _ = x_sample.shape
    xp, xs = x_prompt, x_sample
    k_p, k_s, v_p, v_s, i_p, i_s, c_p, c_s, h_p, h_s = [], [], [], [], [], [], [], [], [], []
    r_p, r_s, m_p, m_s = [], [], [], []
    for layer in range(DEPTH):
        j = layer // 2
        if layer % 2 == 0:
            lru_params = (conv_w[j], conv_b[j], lru_wa[j], lru_ba[j], lru_wx[j], lru_bx[j], lru_lambda[j])
            q, k, v, qi, wi, ki, u, g = _even_project(xp, w_in[j], w_idx_qb[j], kidx_ln_g[j], kidx_ln_b[j])
            att = _dsa_prompt(q, k, v, qi, wi, ki)
            rec, conv_new, h_new = _rglru_branch(
                u, g, jnp.zeros((bp, CONV_W - 1, LRU_WIDTH), u.dtype), jnp.zeros((bp, LRU_WIDTH), jnp.float32),
                *lru_params)
            mix_p = jnp.concatenate([att, rec], axis=-1) @ w_out[j]
            k_p.append(k); v_p.append(v); i_p.append(ki); c_p.append(conv_new); h_p.append(h_new)
            q, k, v, qi, wi, ki, u, g = _even_project(xs, w_in[j], w_idx_qb[j], kidx_ln_g[j], kidx_ln_b[j])
            att = _dsa_sample(q, k, v, qi, wi, ki, cache_k[j], cache_v[j], cache_kidx[j], page_table)
            rec, conv_new, h_new = _rglru_branch(u, g, state_conv[j], state_lru[j], *lru_params)
            mix_s = jnp.concatenate([att, rec], axis=-1) @ w_out[j]
            k_s.append(k); v_s.append(v); i_s.append(ki); c_s.append(conv_new); h_s.append(h_new)
        else:
            s5_params = (s5_a_re[j], s5_a_im[j], s5_log_step[j], s5_b_re[j], s5_b_im[j],
                         s5_c_re[j], s5_c_im[j], s5_d[j], glu_w1[j], glu_b1[j], glu_w2[j], glu_b2[j])
            zeros = jnp.zeros((bp, S5_GROUPS, S5_STATE), jnp.float32)
            mix_p, hr, hi = _s5_block(xp, zeros, zeros, *s5_params)
            r_p.append(hr); m_p.append(hi)
            mix_s, hr, hi = _s5_block(xs, state_s5_re[j], state_s5_im[j], *s5_params)
            r_s.append(hr); m_s.append(hi)
        xp = _layer_norm(DN_ALPHA * xp + mix_p, ln1_g[layer], ln1_b[layer])
        xs = _layer_norm(DN_ALPHA * xs + mix_s, ln1_g[layer], ln1_b[layer])
        tok = jnp.concatenate([xp.reshape(-1, dm), xs.reshape(-1, dm)], axis=0)
        f = _moe(tok, router_w[layer], router_b[layer], moe_w_gu[layer], moe_b_gu[layer],
                 moe_w_down[layer], moe_b_down[layer])
        fp = f[:bp * sp].reshape(bp, sp, dm)
        fs = f[bp * sp:].reshape(bs, ss, dm)
        xp = _layer_norm(DN_ALPHA * xp + fp, ln2_g[layer], ln2_b[layer])
        xs = _layer_norm(DN_ALPHA * xs + fs, ln2_g[layer], ln2_b[layer])
    return (xp, xs,
            jnp.stack(k_p), jnp.stack(k_s), jnp.stack(v_p), jnp.stack(v_s),
            jnp.stack(i_p), jnp.stack(i_s), jnp.stack(c_p), jnp.stack(c_s),
            jnp.stack(h_p), jnp.stack(h_s),
            jnp.stack(r_p), jnp.stack(r_s), jnp.stack(m_p), jnp.stack(m_s))
```

```python
import functools
import math

import jax
import jax.numpy as jnp
from jax import lax
from jax.experimental import pallas as pl
from jax.experimental.pallas import tpu as pltpu

F32 = jnp.float32
BF16 = jnp.bfloat16
I32 = jnp.int32

HEAD_DIM = 128
N_KV_HEADS = 4
IDX_HEADS = 32
IDX_DIM = 128
IDX_Q_RANK = 768
TOPK_MAX = 256
LRU_HEADS = 16
CONV_W = 4
LRU_C = 8.0
S5_GROUP = 16
TOP_K = 4
SWIGLU_LIMIT = 7.0
SWIGLU_ALPHA = 1.702
LN_EPS = 1e-5

LANES = 128
SUBLANES = 8
VMEM_LIMIT_BYTES = 56 * 1024 * 1024

NEG_INF = float("-inf")
INT_MIN = -(2 ** 31)


def _params(*sem):
    return pltpu.CompilerParams(dimension_semantics=sem, vmem_limit_bytes=VMEM_LIMIT_BYTES)


def _nt_dot(a, b):
    return lax.dot_general(a, b, (((1,), (1,)), ((), ())), preferred_element_type=F32)


def _dot(a, b):
    return jnp.dot(a, b, preferred_element_type=F32)


def _mm_kernel(x_ref, w_ref, o_ref):
    o_ref[...] = _dot(x_ref[...].astype(BF16), w_ref[...]).astype(o_ref.dtype)


def _matmul(x, w, *, tm, tn, out_dtype=F32, name="matmul"):
    m, k = x.shape
    n = w.shape[1]
    return pl.pallas_call(
        _mm_kernel,
        out_shape=jax.ShapeDtypeStruct((m, n), out_dtype),
        grid=(m // tm, n // tn),
        in_specs=[pl.BlockSpec((tm, k), lambda i, j: (i, 0)),
                  pl.BlockSpec((k, tn), lambda i, j: (0, j))],
        out_specs=pl.BlockSpec((tm, tn), lambda i, j: (i, j)),
        compiler_params=_params("parallel", "arbitrary"),
        name=name,
    )(x, w)


def _mm2_kernel(xa_ref, xb_ref, wa_ref, wb_ref, o_ref):
    o_ref[...] = _dot(xa_ref[...], wa_ref[...]) + _dot(xb_ref[...], wb_ref[...])


def _matmul_2in(xa, xb, w, *, tm, tn, name):
    m, ka = xa.shape
    n = w.shape[1]
    return pl.pallas_call(
        _mm2_kernel,
        out_shape=jax.ShapeDtypeStruct((m, n), F32),
        grid=(m // tm, n // tn),
        in_specs=[pl.BlockSpec((tm, ka), lambda i, j: (i, 0)),
                  pl.BlockSpec((tm, ka), lambda i, j: (i, 0)),
                  pl.BlockSpec((ka, tn), lambda i, j: (0, j)),
                  pl.BlockSpec((ka, tn), lambda i, j: (1, j))],
        out_specs=pl.BlockSpec((tm, tn), lambda i, j: (i, j)),
        compiler_params=_params("parallel", "arbitrary"),
        name=name,
    )(xa, xb, w, w)


def _glu_kernel(z_ref, w1_ref, b1_ref, w2_ref, b2_ref, o_ref):
    z = z_ref[...]
    a = _dot(z, w1_ref[...]) + b1_ref[...]
    b = _dot(z, w2_ref[...]) + b2_ref[...]
    o_ref[...] = a * jax.nn.sigmoid(b)


def _glu(z, w1, b1, w2, b2, *, tm, tn):
    m, k = z.shape
    n = w1.shape[1]
    x_spec = pl.BlockSpec((tm, k), lambda i, j: (i, 0))
    w_spec = pl.BlockSpec((k, tn), lambda i, j: (0, j))
    b_spec = pl.BlockSpec((1, tn), lambda i, j: (0, j))
    return pl.pallas_call(
        _glu_kernel,
        out_shape=jax.ShapeDtypeStruct((m, n), F32),
        grid=(m // tm, n // tn),
        in_specs=[x_spec, w_spec, b_spec, w_spec, b_spec],
        out_specs=pl.BlockSpec((tm, tn), lambda i, j: (i, j)),
        compiler_params=_params("parallel", "arbitrary"),
        name="glu",
    )(z, w1, b1.reshape(1, n), w2, b2.reshape(1, n))


def _gelu_mix_kernel(y_ref, x_ref, d_ref, o_ref):
    o_ref[...] = jax.nn.gelu(y_ref[...] + d_ref[...] * x_ref[...]).astype(o_ref.dtype)


def _gelu_mix(y, x, d, *, tr):
    m, n = y.shape
    row = lambda i: (i, 0)
    return pl.pallas_call(
        _gelu_mix_kernel,
        out_shape=jax.ShapeDtypeStruct((m, n), BF16),
        grid=(m // tr,),
        in_specs=[pl.BlockSpec((tr, n), row), pl.BlockSpec((tr, n), row), pl.BlockSpec((1, n), lambda i: (0, 0))],
        out_specs=pl.BlockSpec((tr, n), row),
        compiler_params=_params("parallel"),
        name="gelu_mix",
    )(y, x, d.reshape(1, n))


def _ln_rows(x, g, b):
    mu = jnp.mean(x, axis=-1, keepdims=True)
    xc = x - mu
    var = jnp.mean(xc * xc, axis=-1, keepdims=True)
    return xc * lax.rsqrt(var + LN_EPS) * g + b


def _ln_kernel(x_ref, g_ref, b_ref, o_ref):
    o_ref[...] = _ln_rows(x_ref[...], g_ref[...], b_ref[...])


def _layer_norm_cols(x, g, b, *, tr, col_block, width):
    m = x.shape[0]
    return pl.pallas_call(
        _ln_kernel,
        out_shape=jax.ShapeDtypeStruct((m, width), F32),
        grid=(m // tr,),
        in_specs=[pl.BlockSpec((tr, width), lambda i: (i, col_block)),
                  pl.BlockSpec((1, width), lambda i: (0, 0)),
                  pl.BlockSpec((1, width), lambda i: (0, 0))],
        out_specs=pl.BlockSpec((tr, width), lambda i: (i, 0)),
        compiler_params=_params("parallel"),
        name="layer_norm",
    )(x, g.reshape(1, width), b.reshape(1, width))


def _top_gates(logits, n_experts):
    lane = lax.broadcasted_iota(I32, logits.shape, 1)
    lane_f = lane.astype(F32)
    l = jnp.where(lane < n_experts, logits, NEG_INF)
    vals, ids = [], []
    for _ in range(TOP_K):
        m = jnp.max(l, axis=-1, keepdims=True)
        idx = jnp.min(jnp.where(l == m, lane_f, float(LANES)), axis=-1, keepdims=True)
        vals.append(m)
        ids.append(idx.astype(I32))
        l = jnp.where(lane_f == idx, NEG_INF, l)
    es = [jnp.exp(v - vals[0]) for v in vals]
    den = es[0]
    for e in es[1:]:
        den = den + e
    gates = jnp.zeros(logits.shape, F32)
    idv = jnp.zeros(logits.shape, I32)
    for r in range(TOP_K):
        gates = jnp.where(lane == r, es[r] / den, gates)
        idv = jnp.where(lane == r, ids[r], idv)
    return idv, gates


def _ln_router_kernel(x_ref, mix_ref, g_ref, b_ref, rw_ref, rb_ref, o_ref, id_ref, gate_ref, *, alpha, n_experts):
    y = _ln_rows(alpha * x_ref[...] + mix_ref[...], g_ref[...], b_ref[...])
    o_ref[...] = y
    logits = _dot(y.astype(BF16), rw_ref[...]) + rb_ref[...]
    idv, gates = _top_gates(logits, n_experts)
    id_ref[...] = idv
    gate_ref[...] = gates


def _ln_router(x, mix, g, b, rw, rb, *, alpha, tr):
    m, d = x.shape
    n_experts = rw.shape[1]
    rw_p = jnp.zeros((d, LANES), BF16).at[:, :n_experts].set(rw.astype(BF16))
    rb_p = jnp.zeros((1, LANES), F32).at[0, :n_experts].set(rb)
    row = lambda i: (i, 0)
    fixed = lambda i: (0, 0)
    x1, ids, gates = pl.pallas_call(
        functools.partial(_ln_router_kernel, alpha=alpha, n_experts=n_experts),
        out_shape=(jax.ShapeDtypeStruct((m, d), F32),
                   jax.ShapeDtypeStruct((m, LANES), I32),
                   jax.ShapeDtypeStruct((m, LANES), F32)),
        grid=(m // tr,),
        in_specs=[pl.BlockSpec((tr, d), row), pl.BlockSpec((tr, d), row),
                  pl.BlockSpec((1, d), fixed), pl.BlockSpec((1, d), fixed),
                  pl.BlockSpec((d, LANES), fixed), pl.BlockSpec((1, LANES), fixed)],
        out_specs=(pl.BlockSpec((tr, d), row), pl.BlockSpec((tr, LANES), row), pl.BlockSpec((tr, LANES), row)),
        compiler_params=_params("parallel"),
        name="ln_router",
    )(x, mix, g.reshape(1, d), b.reshape(1, d), rw_p, rb_p)
    return x1, ids[:, :TOP_K], gates[:, :TOP_K]


def _sortable_key(x):
    bits = pltpu.bitcast(x, I32)
    return jnp.where(bits < 0, bits ^ jnp.int32(0x7FFFFFFF), bits)


def _count(mask):
    return jnp.sum(jnp.where(mask, 1.0, 0.0), axis=-1, keepdims=True)


def _kth_largest_key(keys, k):
    def body(i, t):
        cand = t + lax.shift_left(jnp.int32(1), jnp.int32(31) - i)
        cnt = _count(keys[0] >= cand)
        for kk in keys[1:]:
            cnt = cnt + _count(kk >= cand)
        return jnp.where(cnt >= float(k), cand, t)

    t0 = jnp.full((keys[0].shape[0], 1), INT_MIN, I32)
    return lax.fori_loop(0, 32, body, t0)


def _strict_upper(n):
    return (jnp.arange(n)[:, None] < jnp.arange(n)[None, :]).astype(BF16)


def _dsa_prompt_kernel(q_ref, k_ref, v_ref, qi_ref, wi_ref, ki_ref, tri_ref, o_ref, *, topk, n_heads, w_scale):
    i = pl.program_id(1)
    tq = q_ref.shape[0]
    s_len = k_ref.shape[0]
    kidx_bf = ki_ref[...].astype(BF16)
    w = wi_ref[...][:, :IDX_HEADS] * w_scale
    sc = jnp.zeros((tq, s_len), F32)
    for h in range(IDX_HEADS):
        d = _nt_dot(qi_ref[:, h * IDX_DIM:(h + 1) * IDX_DIM], kidx_bf)
        sc = sc + jnp.maximum(d, 0.0) * w[:, h:h + 1]
    row = i * tq + lax.broadcasted_iota(I32, (tq, s_len), 0)
    col = lax.broadcasted_iota(I32, (tq, s_len), 1)
    causal = col <= row
    key = _sortable_key(jnp.where(causal, sc, NEG_INF))
    thr = _kth_largest_key([key], topk)
    gt = key > thr
    eq = key == thr
    need = float(topk) - _count(gt)
    rank = _dot(jnp.where(eq, 1.0, 0.0).astype(BF16), tri_ref[...])
    sel = (gt | (eq & (rank < need))) & causal
    bias = jnp.where(sel, 0.0, NEG_INF)
    group = n_heads // N_KV_HEADS
    for kv in range(N_KV_HEADS):
        kh = k_ref[:, kv * HEAD_DIM:(kv + 1) * HEAD_DIM].astype(BF16)
        vh = v_ref[:, kv * HEAD_DIM:(kv + 1) * HEAD_DIM].astype(BF16)
        for g in range(group):
            hh = kv * group + g
            qh = q_ref[:, hh * HEAD_DIM:(hh + 1) * HEAD_DIM].astype(BF16)
            s = _nt_dot(qh, kh) * (HEAD_DIM ** -0.5) + bias
            m = jnp.max(s, axis=-1, keepdims=True)
            p = jnp.exp(s - m)
            l = jnp.sum(p, axis=-1, keepdims=True)
            o = _dot(p.astype(BF16), vh) / l
            o_ref[:, hh * HEAD_DIM:(hh + 1) * HEAD_DIM] = o.astype(o_ref.dtype)


def _dsa_prompt(proj, q_idx, k_idx, lay, *, bsz, seq, tq):
    topk = min(TOPK_MAX, seq // 4)
    q_w = lay["q_w"]
    kv_w = N_KV_HEADS * HEAD_DIM
    nq = seq // tq
    return pl.pallas_call(
        functools.partial(_dsa_prompt_kernel, topk=topk, n_heads=q_w // HEAD_DIM,
                          w_scale=(IDX_HEADS * IDX_DIM) ** -0.5),
        out_shape=jax.ShapeDtypeStruct((bsz * seq, q_w), BF16),
        grid=(bsz, nq),
        in_specs=[pl.BlockSpec((tq, q_w), lambda b, i: (b * nq + i, lay["q"] // q_w)),
                  pl.BlockSpec((seq, kv_w), lambda b, i: (b, lay["k"] // kv_w)),
                  pl.BlockSpec((seq, kv_w), lambda b, i: (b, lay["v"] // kv_w)),
                  pl.BlockSpec((tq, IDX_HEADS * IDX_DIM), lambda b, i: (b * nq + i, 0)),
                  pl.BlockSpec((tq, LANES), lambda b, i: (b * nq + i, lay["wi"] // LANES)),
                  pl.BlockSpec((seq, IDX_DIM), lambda b, i: (b, 0)),
                  pl.BlockSpec((seq, seq), lambda b, i: (0, 0))],
        out_specs=pl.BlockSpec((tq, q_w), lambda b, i: (b * nq + i, 0)),
        compiler_params=_params("parallel", "arbitrary"),
        name="dsa_prompt",
    )(proj, proj, proj, q_idx, proj, k_idx, _strict_upper(seq))


PAGES_PER_STEP = 8


def _page_scores(qi_rows, w_rows, page_bf, t_new):
    d = jnp.maximum(_nt_dot(qi_rows, page_bf), 0.0) * w_rows
    return jnp.concatenate(
        [jnp.sum(d[t * IDX_HEADS:(t + 1) * IDX_HEADS], axis=0, keepdims=True) for t in range(t_new)], axis=0)


def _dsa_sample_scores_kernel(pt_ref, qi_ref, w_ref, *rest, t_new, page):
    pages = rest[:PAGES_PER_STEP]
    knew_ref, sc_ref, scn_ref = rest[PAGES_PER_STEP:]
    p = pl.program_id(1)
    qi = qi_ref[0]
    w = w_ref[0]
    for g in range(PAGES_PER_STEP):
        sc_ref[0, :, g * page:(g + 1) * page] = _page_scores(qi, w, pages[g][0].astype(BF16), t_new)

    @pl.when(p == pl.num_programs(1) - 1)
    def _():
        sc = _page_scores(qi, w, knew_ref[0].astype(BF16), t_new)
        row = lax.broadcasted_iota(I32, sc.shape, 0)
        col = lax.broadcasted_iota(I32, sc.shape, 1)
        scn_ref[0] = jnp.where(col <= row, sc, NEG_INF)


def _dsa_sample_attn_kernel(pt_ref, sc_ref, scn_ref, q_ref, *rest, t_new, page, topk, n_heads):
    kp = rest[:PAGES_PER_STEP]
    vp = rest[PAGES_PER_STEP:2 * PAGES_PER_STEP]
    knew_ref, vnew_ref, tri_ref, o_ref, thr_s, need_s, cnt_s, m_s, l_s, acc_s, bias_s, flag_s = rest[2 * PAGES_PER_STEP:]
    p = pl.program_id(1)
    chunk = PAGES_PER_STEP * page
    rows = t_new * n_heads

    @pl.when(p == 0)
    def _():
        key_p = _sortable_key(sc_ref[0])
        key_n = _sortable_key(scn_ref[0])
        thr = _kth_largest_key([key_p, key_n], topk)
        n_gt = _count(key_p > thr) + _count(key_n > thr)
        n_eq = _count(key_p == thr) + _count(key_n == thr)
        need = float(topk) - n_gt
        thr_s[...] = jnp.broadcast_to(thr, thr_s.shape)
        need_s[...] = jnp.broadcast_to(need, need_s.shape)
        cnt_s[...] = jnp.zeros(cnt_s.shape, F32)
        flag_s[0] = (jnp.max(n_eq - need) > 0.0).astype(I32)
        m_s[...] = jnp.full(m_s.shape, NEG_INF, F32)
        l_s[...] = jnp.zeros(l_s.shape, F32)
        acc_s[...] = jnp.zeros(acc_s.shape, F32)

    thr = thr_s[:, 0:1]
    q = q_ref[0]

    def select(sc, width):
        key = _sortable_key(sc)
        bias_s[:, :width] = jnp.where(key >= thr, 0.0, NEG_INF)

        @pl.when(flag_s[0] != 0)
        def _():
            eq = key == thr
            eqf = jnp.where(eq, 1.0, 0.0)
            rank = _dot(eqf.astype(BF16), tri_ref[:width, :width]) + cnt_s[:, 0:1]
            sel = (key > thr) | (eq & (rank < need_s[:, 0:1]))
            bias_s[:, :width] = jnp.where(sel, 0.0, NEG_INF)
            cnt_s[...] = cnt_s[...] + jnp.sum(eqf, axis=-1, keepdims=True)

    def attend(k_bf, v_bf, bias_t):
        bias = jnp.concatenate([jnp.broadcast_to(bias_t[t:t + 1], (n_heads, bias_t.shape[1])) for t in range(t_new)],
                               axis=0)
        s = _nt_dot(q, k_bf) * (HEAD_DIM ** -0.5) + bias
        m_old = m_s[:, 0:1]
        m_new = jnp.maximum(m_old, jnp.max(s, axis=-1, keepdims=True))
        m_safe = jnp.where(m_new == NEG_INF, 0.0, m_new)
        alpha = jnp.exp(m_old - m_safe)
        pr = jnp.exp(s - m_safe)
        l_s[...] = alpha * l_s[...] + jnp.sum(pr, axis=-1, keepdims=True)
        acc_s[...] = alpha * acc_s[...] + _dot(pr.astype(BF16), v_bf)
        m_s[...] = jnp.broadcast_to(m_new, m_s.shape)

    start = pl.multiple_of(p * chunk, chunk)
    select(sc_ref[0, :, pl.ds(start, chunk)], chunk)
    for g in range(PAGES_PER_STEP):
        attend(kp[g][0].astype(BF16), vp[g][0].astype(BF16), bias_s[:, g * page:(g + 1) * page])

    @pl.when(p == pl.num_programs(1) - 1)
    def _():
        select(scn_ref[0], page)
        attend(knew_ref[0].astype(BF16), vnew_ref[0].astype(BF16), bias_s[:, :page])
        o_ref[0] = acc_s[...] / l_s[:, 0:1]


def _dsa_sample(q, k, v, q_idx, w_idx, k_idx, cache_k, cache_v, cache_kidx, page_table):
    bsz, t_new, q_w = q.shape
    n_heads = q_w // HEAD_DIM
    kv_w = N_KV_HEADS * HEAD_DIM
    page = cache_k.shape[1]
    n_pages = page_table.shape[1]
    past = n_pages * page
    topk = min(TOPK_MAX, (past + t_new) // 4)
    steps = n_pages // PAGES_PER_STEP
    chunk = PAGES_PER_STEP * page
    rows = t_new * n_heads

    qi_rows = q_idx.reshape(bsz, t_new * IDX_HEADS, IDX_DIM).astype(BF16)
    w_rows = (w_idx * (IDX_HEADS * IDX_DIM) ** -0.5).reshape(bsz, t_new * IDX_HEADS, 1)
    pad_rows = lambda a: jnp.pad(a, ((0, 0), (0, page - t_new), (0, 0)))
    kidx_new, k_new, v_new = pad_rows(k_idx), pad_rows(k), pad_rows(v)

    def page_spec(width, g):
        return pl.BlockSpec((1, page, width), lambda b, p, pt: (pt[b, p * PAGES_PER_STEP + g], 0, 0))

    per_b = lambda b, p, pt: (b, 0, 0)
    sc_past, sc_new = pl.pallas_call(
        functools.partial(_dsa_sample_scores_kernel, t_new=t_new, page=page),
        out_shape=(jax.ShapeDtypeStruct((bsz, t_new, past), F32),
                   jax.ShapeDtypeStruct((bsz, t_new, page), F32)),
        grid_spec=pltpu.PrefetchScalarGridSpec(
            num_scalar_prefetch=1,
            grid=(bsz, steps),
            in_specs=[pl.BlockSpec((1, t_new * IDX_HEADS, IDX_DIM), per_b),
                      pl.BlockSpec((1, t_new * IDX_HEADS, 1), per_b)]
                     + [page_spec(IDX_DIM, g) for g in range(PAGES_PER_STEP)]
                     + [pl.BlockSpec((1, page, IDX_DIM), per_b)],
            out_specs=(pl.BlockSpec((1, t_new, chunk), lambda b, p, pt: (b, 0, p)),
                       pl.BlockSpec((1, t_new, page), per_b))),
        compiler_params=_params("parallel", "arbitrary"),
        name="dsa_sample_scores",
    )(page_table, qi_rows, w_rows, *([cache_kidx] * PAGES_PER_STEP), kidx_new)

    group = n_heads // N_KV_HEADS
    head_kv = jnp.arange(n_heads) // group
    blockdiag = (head_kv[:, None] == jnp.arange(N_KV_HEADS)[None, :]).astype(F32)
    q4 = q.reshape(bsz, t_new, n_heads, 1, HEAD_DIM) * blockdiag[None, None, :, :, None]
    q_bd = q4.reshape(bsz, rows, kv_w).astype(BF16)

    out = pl.pallas_call(
        functools.partial(_dsa_sample_attn_kernel, t_new=t_new, page=page, topk=topk, n_heads=n_heads),
        out_shape=jax.ShapeDtypeStruct((bsz, rows, kv_w), F32),
        grid_spec=pltpu.PrefetchScalarGridSpec(
            num_scalar_prefetch=1,
            grid=(bsz, steps),
            in_specs=[pl.BlockSpec((1, t_new, past), per_b),
                      pl.BlockSpec((1, t_new, page), per_b),
                      pl.BlockSpec((1, rows, kv_w), per_b)]
                     + [page_spec(kv_w, g) for g in range(PAGES_PER_STEP)]
                     + [page_spec(kv_w, g) for g in range(PAGES_PER_STEP)]
                     + [pl.BlockSpec((1, page, kv_w), per_b), pl.BlockSpec((1, page, kv_w), per_b),
                        pl.BlockSpec((chunk, chunk), lambda b, p, pt: (0, 0))],
            out_specs=pl.BlockSpec((1, rows, kv_w), per_b),
            scratch_shapes=[pltpu.VMEM((t_new, LANES), I32), pltpu.VMEM((t_new, LANES), F32),
                            pltpu.VMEM((t_new, LANES), F32), pltpu.VMEM((rows, LANES), F32),
                            pltpu.VMEM((rows, LANES), F32), pltpu.VMEM((rows, kv_w), F32),
                            pltpu.VMEM((t_new, chunk), F32), pltpu.SMEM((1,), I32)]),
        compiler_params=_params("parallel", "arbitrary"),
        name="dsa_sample_attn",
    )(page_table, sc_past, sc_new, q_bd, *([cache_k] * PAGES_PER_STEP), *([cache_v] * PAGES_PER_STEP),
      k_new, v_new, _strict_upper(chunk))
    o5 = out.reshape(bsz, t_new, N_KV_HEADS, group, N_KV_HEADS, HEAD_DIM)
    o = jnp.stack([o5[:, :, kv, :, kv, :] for kv in range(N_KV_HEADS)], axis=2)
    return o.reshape(bsz, t_new, q_w)


def _softplus(x):
    return jnp.maximum(x, 0.0) + jnp.log1p(jnp.exp(-jnp.abs(x)))


def _rglru_kernel(u_ref, g_ref, cp_ref, h0_ref, cw_ref, cb_ref, wa_ref, ba_ref, wx_ref, bx_ref, lam_ref,
                  y_ref, hl_ref, up_s, a_s, b_s, *, t_valid, t_chunk):
    t_len, wb = u_ref.shape
    lead = SUBLANES - (CONV_W - 1)
    up_s[0:SUBLANES, :] = jnp.zeros((SUBLANES, wb), F32)
    up_s[lead:SUBLANES, :] = cp_ref[0]
    up_s[SUBLANES:SUBLANES + t_len, :] = u_ref[...]
    neg_c_sp = -LRU_C * _softplus(-lam_ref[...])
    for c0 in range(0, t_len, t_chunk):
        xc = cw_ref[0:1, :] * up_s[lead + c0:lead + c0 + t_chunk, :]
        for j in range(1, CONV_W):
            xc = xc + cw_ref[j:j + 1, :] * up_s[lead + j + c0:lead + j + c0 + t_chunk, :]
        xc = cb_ref[...] + xc
        ra, ix = [], []
        for hd in range(wb // LANES):
            xh = xc[:, hd * LANES:(hd + 1) * LANES].astype(BF16)
            ra.append(_dot(xh, wa_ref[hd]))
            ix.append(_dot(xh, wx_ref[hd]))
        r = jax.nn.sigmoid(jnp.concatenate(ra, axis=-1) + ba_ref[...])
        gi = jax.nn.sigmoid(jnp.concatenate(ix, axis=-1) + bx_ref[...])
        log_a = r * neg_c_sp
        a = jnp.exp(log_a)
        a_s[c0:c0 + t_chunk, :] = a
        b_s[c0:c0 + t_chunk, :] = jnp.sqrt(-jnp.tanh(log_a) * (a * a + 1.0)) * (gi * xc)

    def step(t, h):
        h = a_s[pl.ds(t, 1), :] * h + b_s[pl.ds(t, 1), :]
        b_s[pl.ds(t, 1), :] = h
        return h

    lax.fori_loop(0, t_len, step, h0_ref[0], unroll=8)
    hl_ref[0] = b_s[t_valid - 1:t_valid, :]
    for c0 in range(0, t_len, t_chunk):
        rows = slice(c0, c0 + t_chunk)
        y_ref[rows, :] = (b_s[rows, :] * jax.nn.gelu(g_ref[rows, :])).astype(y_ref.dtype)


def _rglru(u_src, g_src, conv_prev, h0, conv_w, conv_b, wa, ba, wx, bx, lam, *, bsz, t_len, t_valid,
           u_col, g_col, width, wb):
    nb = width // wb
    heads_per = wb // LANES
    vec = lambda a: a.reshape(1, width)
    vspec = pl.BlockSpec((1, wb), lambda b, c: (0, c))
    t_chunk = min(t_len, 256)
    y, h_last = pl.pallas_call(
        functools.partial(_rglru_kernel, t_valid=t_valid, t_chunk=t_chunk),
        out_shape=(jax.ShapeDtypeStruct((bsz * t_len, width), BF16),
                   jax.ShapeDtypeStruct((bsz, 1, width), F32)),
        grid=(bsz, nb),
        in_specs=[pl.BlockSpec((t_len, wb), lambda b, c: (b, u_col // wb + c)),
                  pl.BlockSpec((t_len, wb), lambda b, c: (b, g_col // wb + c)),
                  pl.BlockSpec((1, CONV_W - 1, wb), lambda b, c: (b, 0, c)),
                  pl.BlockSpec((1, 1, wb), lambda b, c: (b, 0, c)),
                  pl.BlockSpec((CONV_W, wb), lambda b, c: (0, c)),
                  vspec,
                  pl.BlockSpec((heads_per, LANES, LANES), lambda b, c: (c, 0, 0)),
                  vspec,
                  pl.BlockSpec((heads_per, LANES, LANES), lambda b, c: (c, 0, 0)),
                  vspec, vspec],
        out_specs=(pl.BlockSpec((t_len, wb), lambda b, c: (b, c)),
                   pl.BlockSpec((1, 1, wb), lambda b, c: (b, 0, c))),
        scratch_shapes=[pltpu.VMEM((t_len + SUBLANES, wb), F32), pltpu.VMEM((t_len, wb), F32),
                        pltpu.VMEM((t_len, wb), F32)],
        compiler_params=_params("parallel", "arbitrary"),
        name="rglru",
    )(u_src, g_src, conv_prev, h0.reshape(bsz, 1, width), conv_w, vec(conv_b), wa.astype(BF16), vec(ba),
      wx.astype(BF16), vec(bx), vec(lam))
    return y, h_last.reshape(bsz, width)


def _s5_disc_kernel(are_ref, aim_ref, ls_ref, abr_ref, abi_ref, cfr_ref, cfi_ref):
    a_re, a_im = are_ref[...], aim_ref[...]
    dt = jnp.exp(ls_ref[...])
    mag = jnp.exp(a_re * dt)
    ab_re = mag * jnp.cos(a_im * dt)
    ab_im = mag * jnp.sin(a_im * dt)
    den = a_re * a_re + a_im * a_im
    nr, ni = ab_re - 1.0, ab_im
    abr_ref[...] = ab_re
    abi_ref[...] = ab_im
    cfr_ref[...] = (nr * a_re + ni * a_im) / den
    cfi_ref[...] = (ni * a_re - nr * a_im) / den


def _s5_discretize(a_re, a_im, log_step):
    g, n = a_re.shape
    shp = jax.ShapeDtypeStruct((g, n), F32)
    return pl.pallas_call(_s5_disc_kernel, out_shape=(shp, shp, shp, shp), name="s5_discretize")(
        a_re, a_im, jnp.broadcast_to(log_step[:, None], (g, n)))


def _s5_kernel(lhs_ref, wb_ref, wc_ref, are_ref, aim_ref, h0r_ref, h0i_ref, y_ref, hr_ref, hi_ref, x_s, st_s,
               *, nj, t_chunk):
    c = pl.program_id(1)
    ns = are_ref.shape[2]

    @pl.when(c == 0)
    def _():
        st_s[0] = h0r_ref[0]
        st_s[1] = h0i_ref[0]

    x_s[...] = _dot(lhs_ref[0], wb_ref[0])
    ar, ai = are_ref[0], aim_ref[0]

    def step(t, carry):
        xr, xi = carry
        r0 = pl.multiple_of(t * SUBLANES, SUBLANES)
        nr = ar * xr - ai * xi + x_s[pl.ds(r0, SUBLANES), 0:ns]
        ni = ar * xi + ai * xr + x_s[pl.ds(r0, SUBLANES), ns:2 * ns]
        x_s[pl.ds(r0, SUBLANES), 0:ns] = nr
        x_s[pl.ds(r0, SUBLANES), ns:2 * ns] = ni
        return nr, ni

    xr, xi = lax.fori_loop(0, t_chunk, step, (st_s[0], st_s[1]), unroll=min(t_chunk, 4))
    st_s[0] = xr
    st_s[1] = xi
    yb = _dot(x_s[...].astype(BF16), wc_ref[0])
    y = yb[:, 0:LANES]
    if nj > 1:
        rowj = (lax.broadcasted_iota(I32, y.shape, 0) % SUBLANES) // (SUBLANES // nj)
        for j in range(1, nj):
            y = jnp.where(rowj == j, yb[:, j * LANES:(j + 1) * LANES], y)
    y_ref[0] = y

    @pl.when(c == pl.num_programs(1) - 1)
    def _():
        hr_ref[0] = xr
        hi_ref[0] = xi


def _s5_scan(u, h0_re, h0_im, ab_re, ab_im, bb_re, bb_im, c_re, c_im):
    bsz, t_len, d = u.shape
    groups, n_state = ab_re.shape
    gl = LANES // S5_GROUP
    ns = gl * n_state
    nj = SUBLANES // bsz
    nq = d // (LANES * nj)
    t_chunk = min(t_len, 256)
    eye_g = jnp.eye(gl, dtype=F32)
    eye_j = jnp.eye(nj, dtype=F32)

    u6 = u.transpose(1, 0, 2).reshape(t_len, bsz, nq, 1, nj, LANES) * eye_j[None, None, None, :, :, None]
    lhs = u6.transpose(2, 0, 3, 1, 4, 5).reshape(nq, t_len * SUBLANES, nj * LANES).astype(BF16)

    def b_weight(bb):
        w = bb.reshape(nq, nj, gl, n_state, S5_GROUP).transpose(0, 1, 2, 4, 3)
        w = w[:, :, :, :, None, :] * eye_g[None, None, :, None, :, None]
        return w.reshape(nq, nj * LANES, ns)

    w_b = jnp.concatenate([b_weight(bb_re), b_weight(bb_im)], axis=-1).astype(BF16)

    def c_weight(cc):
        w = cc.reshape(nq, nj, gl, S5_GROUP, n_state).transpose(0, 2, 4, 1, 3)
        w = w[:, :, :, :, None, :] * eye_g[None, :, None, None, :, None]
        return w.reshape(nq, ns, nj * LANES)

    w_c = jnp.concatenate([c_weight(c_re), -c_weight(c_im)], axis=1).astype(BF16)

    def coef(a):
        a = a.reshape(nq, nj, 1, ns)
        return jnp.broadcast_to(a, (nq, nj, bsz, ns)).reshape(nq, SUBLANES, ns)

    def state_in(h):
        return h.reshape(bsz, nq, nj, ns).transpose(1, 2, 0, 3).reshape(nq, SUBLANES, ns)

    def state_out(h):
        return h.reshape(nq, nj, bsz, ns).transpose(2, 0, 1, 3).reshape(bsz, groups, n_state)

    rows = t_chunk * SUBLANES
    per_q = lambda q, c: (q, 0, 0)
    st_spec = pl.BlockSpec((1, SUBLANES, ns), per_q)
    st_shape = jax.ShapeDtypeStruct((nq, SUBLANES, ns), F32)
    y, hr, hi = pl.pallas_call(
        functools.partial(_s5_kernel, nj=nj, t_chunk=t_chunk),
        out_shape=(jax.ShapeDtypeStruct((nq, t_len * SUBLANES, LANES), F32), st_shape, st_shape),
        grid=(nq, t_len // t_chunk),
        in_specs=[pl.BlockSpec((1, rows, nj * LANES), lambda q, c: (q, c, 0)),
                  pl.BlockSpec((1, nj * LANES, 2 * ns), per_q),
                  pl.BlockSpec((1, 2 * ns, nj * LANES), per_q),
                  st_spec, st_spec, st_spec, st_spec],
        out_specs=(pl.BlockSpec((1, rows, LANES), lambda q, c: (q, c, 0)), st_spec, st_spec),
        scratch_shapes=[pltpu.VMEM((rows, 2 * ns), F32), pltpu.VMEM((2, SUBLANES, ns), F32)],
        compiler_params=_params("parallel", "arbitrary"),
        name="s5_scan",
    )(lhs, w_b, w_c, coef(ab_re), coef(ab_im), state_in(h0_re), state_in(h0_im))
    y = y.reshape(nq, t_len, nj, bsz, LANES).transpose(3, 1, 0, 2, 4).reshape(bsz, t_len, d)
    return y, state_out(hr), state_out(hi)


MOE_TM = 256
COMBINE_ROWS = 32


def _moe_plan(ids, gates, n_experts, n_rows_total):
    n = ids.shape[0]
    pairs = n * TOP_K
    n_tiles = pairs // MOE_TM + n_experts
    flat_e = ids.reshape(-1)
    order = jnp.argsort(flat_e, stable=True)
    counts = jnp.zeros((n_experts,), I32).at[flat_e].add(1)
    tiles_per = (counts + MOE_TM - 1) // MOE_TM
    tile_end = jnp.cumsum(tiles_per)
    pad_start = (tile_end - tiles_per) * MOE_TM
    start = jnp.cumsum(counts) - counts
    sorted_e = flat_e[order]
    dest_sorted = pad_start[sorted_e] + (jnp.arange(pairs, dtype=I32) - start[sorted_e])
    dest = jnp.zeros((pairs,), I32).at[order].set(dest_sorted)
    n_slots = n_tiles * MOE_TM
    row_src = jnp.zeros((n_slots,), I32).at[dest].set(jnp.arange(pairs, dtype=I32) // TOP_K)
    row_gate = jnp.zeros((n_slots, 1), F32).at[dest, 0].set(gates.reshape(-1))
    n_valid = tile_end[-1].astype(I32)
    tile_ids = jnp.arange(n_tiles, dtype=I32)
    tile_expert = jnp.searchsorted(tile_end, jnp.minimum(tile_ids, n_valid - 1), side="right").astype(I32)
    dest_pad = jnp.zeros((n_rows_total * TOP_K,), I32).at[:pairs].set(dest)
    return row_src, row_gate, tile_expert, n_valid.reshape(1), dest_pad, n_tiles


def _gather_rows_kernel(src_ref, x_hbm, o_ref, buf, sem):
    t = pl.program_id(0)
    nt = pl.num_programs(0)
    tm = o_ref.shape[0]

    def issue(tile, slot):
        def body(r, carry):
            row = src_ref[tile * tm + r]
            pltpu.make_async_copy(x_hbm.at[pl.ds(row, 1), :], buf.at[slot, pl.ds(r, 1), :], sem.at[slot]).start()
            return carry
        lax.fori_loop(0, tm, body, 0)

    @pl.when(t == 0)
    def _():
        issue(0, 0)

    @pl.when(t + 1 < nt)
    def _():
        issue(t + 1, (t + 1) % 2)

    slot = t % 2
    pltpu.make_async_copy(x_hbm.at[pl.ds(0, tm), :], buf.at[slot], sem.at[slot]).wait()
    o_ref[...] = buf[slot].astype(o_ref.dtype)


def _gather_rows(x, row_src, n_tiles):
    d = x.shape[1]
    return pl.pallas_call(
        _gather_rows_kernel,
        out_shape=jax.ShapeDtypeStruct((n_tiles * MOE_TM, d), BF16),
        grid_spec=pltpu.PrefetchScalarGridSpec(
            num_scalar_prefetch=1,
            grid=(n_tiles,),
            in_specs=[pl.BlockSpec(memory_space=pl.ANY)],
            out_specs=pl.BlockSpec((MOE_TM, d), lambda t, src: (t, 0)),
            scratch_shapes=[pltpu.VMEM((2, MOE_TM, d), x.dtype), pltpu.SemaphoreType.DMA((2,))]),
        compiler_params=_params("arbitrary"),
        name="moe_gather",
    )(row_src, x)


def _moe_gate_up_kernel(te_ref, nv_ref, x_ref, wg_ref, wu_ref, bg_ref, bu_ref, a_ref, *, k_chunk):
    t = pl.program_id(1)

    @pl.when(t < nv_ref[0])
    def _():
        d = x_ref.shape[1]
        hg = jnp.zeros(a_ref.shape, F32)
        hu = jnp.zeros(a_ref.shape, F32)
        for k0 in range(0, d, k_chunk):
            xk = x_ref[:, k0:k0 + k_chunk]
            hg = hg + _dot(xk, wg_ref[0, k0:k0 + k_chunk, :].astype(BF16))
            hu = hu + _dot(xk, wu_ref[0, k0:k0 + k_chunk, :].astype(BF16))
        gate = jnp.minimum(hg + bg_ref[0], SWIGLU_LIMIT)
        up = jnp.clip(hu + bu_ref[0], -SWIGLU_LIMIT, SWIGLU_LIMIT)
        a_ref[...] = ((up + 1.0) * gate * jax.nn.sigmoid(SWIGLU_ALPHA * gate)).astype(a_ref.dtype)

    @pl.when(t >= nv_ref[0])
    def _():
        a_ref[...] = jnp.zeros(a_ref.shape, a_ref.dtype)


def _moe_down_kernel(te_ref, nv_ref, a_ref, w_ref, b_ref, g_ref, y_ref, *, k_chunk):
    t = pl.program_id(1)

    @pl.when(t < nv_ref[0])
    def _():
        f = a_ref.shape[1]
        acc = jnp.zeros(y_ref.shape, F32)
        for k0 in range(0, f, k_chunk):
            acc = acc + _dot(a_ref[:, k0:k0 + k_chunk], w_ref[0, k0:k0 + k_chunk, :].astype(BF16))
        y_ref[...] = (acc + b_ref[0]) * g_ref[...]

    @pl.when(t >= nv_ref[0])
    def _():
        y_ref[...] = jnp.zeros(y_ref.shape, y_ref.dtype)


def _moe_experts(xs, tile_expert, n_valid, row_gate, w_gu, b_gu, w_dn, b_dn, n_tiles):
    n_experts, d, two_f = w_gu.shape
    f = two_f // 2
    tn1 = min(512, f)
    nb1 = f // tn1
    n_slots = n_tiles * MOE_TM
    b_gu3 = b_gu.reshape(n_experts, 1, two_f)
    act = pl.pallas_call(
        functools.partial(_moe_gate_up_kernel, k_chunk=min(1024, d)),
        out_shape=jax.ShapeDtypeStruct((n_slots, f), BF16),
        grid_spec=pltpu.PrefetchScalarGridSpec(
            num_scalar_prefetch=2,
            grid=(nb1, n_tiles),
            in_specs=[pl.BlockSpec((MOE_TM, d), lambda j, t, te, nv: (t, 0)),
                      pl.BlockSpec((1, d, tn1), lambda j, t, te, nv: (te[t], 0, j)),
                      pl.BlockSpec((1, d, tn1), lambda j, t, te, nv: (te[t], 0, nb1 + j)),
                      pl.BlockSpec((1, 1, tn1), lambda j, t, te, nv: (te[t], 0, j)),
                      pl.BlockSpec((1, 1, tn1), lambda j, t, te, nv: (te[t], 0, nb1 + j))],
            out_specs=pl.BlockSpec((MOE_TM, tn1), lambda j, t, te, nv: (t, j))),
        compiler_params=_params("arbitrary", "arbitrary"),
        name="moe_gate_up",
    )(tile_expert, n_valid, xs, w_gu, w_gu, b_gu3, b_gu3)

    tn2 = min(2048, d)
    nb2 = d // tn2
    ys = pl.pallas_call(
        functools.partial(_moe_down_kernel, k_chunk=min(512, f)),
        out_shape=jax.ShapeDtypeStruct((n_slots, d), F32),
        grid_spec=pltpu.PrefetchScalarGridSpec(
            num_scalar_prefetch=2,
            grid=(nb2, n_tiles),
            in_specs=[pl.BlockSpec((MOE_TM, f), lambda j, t, te, nv: (t, 0)),
                      pl.BlockSpec((1, f, tn2), lambda j, t, te, nv: (te[t], 0, j)),
                      pl.BlockSpec((1, 1, tn2), lambda j, t, te, nv: (te[t], 0, j)),
                      pl.BlockSpec((MOE_TM, 1), lambda j, t, te, nv: (t, 0))],
            out_specs=pl.BlockSpec((MOE_TM, tn2), lambda j, t, te, nv: (t, j))),
        compiler_params=_params("arbitrary", "arbitrary"),
        name="moe_down",
    )(tile_expert, n_valid, act, w_dn, b_dn.reshape(n_experts, 1, d), row_gate)
    return ys


def _combine_ln_kernel(dest_ref, ys_hbm, x_ref, g_ref, b_ref, o_ref, ob_ref, buf, sem, *, alpha):
    t = pl.program_id(0)
    nt = pl.num_programs(0)
    tr = o_ref.shape[0]

    def issue(tile, slot):
        def body(r, carry):
            for k in range(TOP_K):
                row = dest_ref[(tile * tr + r) * TOP_K + k]
                pltpu.make_async_copy(ys_hbm.at[pl.ds(row, 1), :], buf.at[slot, k, pl.ds(r, 1), :],
                                      sem.at[slot]).start()
            return carry
        lax.fori_loop(0, tr, body, 0)

    @pl.when(t == 0)
    def _():
        issue(0, 0)

    @pl.when(t + 1 < nt)
    def _():
        issue(t + 1, (t + 1) % 2)

    slot = t % 2
    for k in range(TOP_K):
        pltpu.make_async_copy(ys_hbm.at[pl.ds(0, tr), :], buf.at[slot, k], sem.at[slot]).wait()
    f = buf[slot, 0]
    for k in range(1, TOP_K):
        f = f + buf[slot, k]
    y = _ln_rows(alpha * x_ref[...] + f, g_ref[...], b_ref[...])
    o_ref[...] = y
    ob_ref[...] = y.astype(ob_ref.dtype)


def _combine_ln(ys, dest, x1, g, b, *, alpha):
    m, d = x1.shape
    row = lambda t, dst: (t, 0)
    fixed = lambda t, dst: (0, 0)
    return pl.pallas_call(
        functools.partial(_combine_ln_kernel, alpha=alpha),
        out_shape=(jax.ShapeDtypeStruct((m, d), F32), jax.ShapeDtypeStruct((m, d), BF16)),
        grid_spec=pltpu.PrefetchScalarGridSpec(
            num_scalar_prefetch=1,
            grid=(m // COMBINE_ROWS,),
            in_specs=[pl.BlockSpec(memory_space=pl.ANY),
                      pl.BlockSpec((COMBINE_ROWS, d), row),
                      pl.BlockSpec((1, d), fixed), pl.BlockSpec((1, d), fixed)],
            out_specs=(pl.BlockSpec((COMBINE_ROWS, d), row), pl.BlockSpec((COMBINE_ROWS, d), row)),
            scratch_shapes=[pltpu.VMEM((2, TOP_K, COMBINE_ROWS, d), F32), pltpu.SemaphoreType.DMA((2,))]),
        compiler_params=_params("arbitrary"),
        name="moe_combine_ln",
    )(dest, ys, x1, g.reshape(1, d), b.reshape(1, d))


def _moe_block(x, mix, n_tokens, ln1_g, ln1_b, ln2_g, ln2_b, rw, rb, w_gu, b_gu, w_dn, b_dn, *, alpha):
    x1, ids, gates = _ln_router(x, mix, ln1_g, ln1_b, rw, rb, alpha=alpha, tr=256)
    row_src, row_gate, tile_expert, n_valid, dest, n_tiles = _moe_plan(
        ids[:n_tokens], gates[:n_tokens], rw.shape[1], x.shape[0])
    xs = _gather_rows(x1, row_src, n_tiles)
    ys = _moe_experts(xs, tile_expert, n_valid, row_gate, w_gu, b_gu, w_dn, b_dn, n_tiles)
    return _combine_ln(ys, dest, x1, ln2_g, ln2_b, alpha=alpha)


ROW_TILE = 768


def kernel(x_prompt, x_sample, cache_k, cache_v, cache_kidx, state_conv, state_lru, state_s5_re, state_s5_im,
           page_table, w_in, w_idx_qb, kidx_ln_g, kidx_ln_b, conv_w, conv_b, lru_wa, lru_ba, lru_wx, lru_bx,
           lru_lambda, w_out, s5_a_re, s5_a_im, s5_log_step, s5_b_re, s5_b_im, s5_c_re, s5_c_im, s5_d,
           glu_w1, glu_b1, glu_w2, glu_b2, ln1_g, ln1_b, ln2_g, ln2_b, router_w, router_b, moe_w_gu, moe_b_gu,
           moe_w_down, moe_b_down):
    bp, sp, dm = x_prompt.shape
    bs, ss, _ = x_sample.shape
    depth = ln1_g.shape[0]
    alpha = (2 * depth) ** 0.25
    n_p, n_s = bp * sp, bs * ss
    n_tok = n_p + n_s
    n_rows = -(-n_tok // ROW_TILE) * ROW_TILE
    lru_w = conv_w.shape[2]
    q_w = w_out.shape[1] - lru_w
    kv_w = N_KV_HEADS * HEAD_DIM
    s5_groups = dm // S5_GROUP

    def unify(a_p, a_s):
        pad = jnp.zeros((n_rows - n_tok, a_p.shape[-1]), a_p.dtype)
        return jnp.concatenate([a_p.reshape(n_p, -1), a_s.reshape(n_s, -1), pad], axis=0)

    x = unify(x_prompt, x_sample)
    xb = x.astype(BF16)
    outs = {name: [] for name in ("k_p", "k_s", "v_p", "v_s", "i_p", "i_s", "c_p", "c_s", "h_p", "h_s",
                                   "r_p", "r_s", "m_p", "m_s")}
    for layer in range(depth):
        j = layer // 2
        if layer % 2 == 0:
            sizes = [q_w, kv_w, kv_w, IDX_Q_RANK, IDX_HEADS, IDX_DIM, lru_w, lru_w]
            cuts = [0]
            for s_ in sizes:
                cuts.append(cuts[-1] + s_)
            seg = lambda i: w_in[j][:, cuts[i]:cuts[i + 1]]
            lay = {"q": 0, "q_w": q_w, "u": q_w, "g": q_w + lru_w, "k": q_w + 2 * lru_w,
                   "v": q_w + 2 * lru_w + kv_w, "ki": q_w + 2 * lru_w + 2 * kv_w}
            lay["wi"] = lay["ki"] + IDX_DIM
            n_used = lay["wi"] + IDX_HEADS
            n_proj = -(-n_used // 512) * 512
            w_main = jnp.concatenate([seg(0), seg(6), seg(7), seg(1), seg(2), seg(5), seg(4),
                                      jnp.zeros((dm, n_proj - n_used), F32)], axis=1).astype(BF16)
            proj = _matmul(xb, w_main, tm=ROW_TILE, tn=512, name="in_proj")
            cq = _matmul(xb, seg(3).astype(BF16), tm=ROW_TILE, tn=IDX_Q_RANK // 2, out_dtype=BF16, name="cq_proj")
            q_idx = _matmul(cq, w_idx_qb[j].astype(BF16), tm=ROW_TILE, tn=1024, out_dtype=BF16, name="q_idx_proj")
            k_idx = _layer_norm_cols(proj, kidx_ln_g[j], kidx_ln_b[j], tr=ROW_TILE, col_block=lay["ki"] // IDX_DIM,
                                     width=IDX_DIM)
            lru_args = (conv_w[j], conv_b[j], lru_wa[j], lru_ba[j], lru_wx[j], lru_bx[j], lru_lambda[j])

            att_p = _dsa_prompt(proj, q_idx, k_idx, lay, bsz=bp, seq=sp, tq=128)
            rec_p, h_p = _rglru(proj, proj, jnp.zeros((bp, CONV_W - 1, lru_w), F32), jnp.zeros((bp, lru_w), F32),
                                *lru_args, bsz=bp, t_len=sp, t_valid=sp, u_col=lay["u"], g_col=lay["g"],
                                width=lru_w, wb=512)
            col = lambda a, name, width: a[:, lay[name]:lay[name] + width]
            proj_p = proj[:n_p]
            outs["k_p"].append(col(proj_p, "k", kv_w).reshape(bp, sp, N_KV_HEADS, HEAD_DIM))
            outs["v_p"].append(col(proj_p, "v", kv_w).reshape(bp, sp, N_KV_HEADS, HEAD_DIM))
            outs["i_p"].append(k_idx[:n_p].reshape(bp, sp, IDX_DIM))
            u_p = col(proj_p, "u", lru_w).reshape(bp, sp, lru_w)
            outs["c_p"].append(u_p[:, sp - (CONV_W - 1):])
            outs["h_p"].append(h_p)

            proj_s = proj[n_p:n_tok]
            s3 = lambda a: a.reshape(bs, ss, -1)
            q_s, k_s, v_s = s3(col(proj_s, "q", q_w)), s3(col(proj_s, "k", kv_w)), s3(col(proj_s, "v", kv_w))
            u_s, g_s = s3(col(proj_s, "u", lru_w)), s3(col(proj_s, "g", lru_w))
            wi_s = s3(col(proj_s, "wi", IDX_HEADS))
            kidx_s = s3(k_idx[n_p:n_tok])
            n_pool = cache_k.shape[1]
            att_s = _dsa_sample(q_s, k_s, v_s, s3(q_idx[n_p:n_tok]), wi_s, kidx_s,
                                cache_k[j].reshape(n_pool, -1, kv_w), cache_v[j].reshape(n_pool, -1, kv_w),
                                cache_kidx[j], page_table)
            t_pad = -(-ss // SUBLANES) * SUBLANES
            pad_t = lambda a: jnp.pad(a, ((0, 0), (0, t_pad - ss), (0, 0))).reshape(bs * t_pad, -1)
            rec_s, h_s = _rglru(pad_t(u_s), pad_t(g_s), state_conv[j], state_lru[j], *lru_args, bsz=bs, t_len=t_pad,
                                t_valid=ss, u_col=0, g_col=0, width=lru_w, wb=512)
            rec_s = rec_s.reshape(bs, t_pad, lru_w)[:, :ss]
            outs["k_s"].append(k_s.reshape(bs, ss, N_KV_HEADS, HEAD_DIM))
            outs["v_s"].append(v_s.reshape(bs, ss, N_KV_HEADS, HEAD_DIM))
            outs["i_s"].append(kidx_s)
            up_s = jnp.concatenate([state_conv[j], u_s], axis=1)
            outs["c_s"].append(up_s[:, ss:])
            outs["h_s"].append(h_s)

            att = unify(att_p, att_s.astype(BF16))
            rec = unify(rec_p, rec_s)
            mix = _matmul_2in(att, rec, w_out[j].astype(BF16), tm=ROW_TILE, tn=512, name="out_proj")
        else:
            ab_re, ab_im, cf_re, cf_im = _s5_discretize(s5_a_re[j], s5_a_im[j], s5_log_step[j])
            b_re, b_im = s5_b_re[j], s5_b_im[j]
            bb_re = cf_re[..., None] * b_re - cf_im[..., None] * b_im
            bb_im = cf_re[..., None] * b_im + cf_im[..., None] * b_re
            s5_args = (ab_re, ab_im, bb_re, bb_im, s5_c_re[j], s5_c_im[j])
            n_state = ab_re.shape[1]
            zeros = jnp.zeros((bp, s5_groups, n_state), F32)
            y_p, hr, hi = _s5_scan(x[:n_p].reshape(bp, sp, dm), zeros, zeros, *s5_args)
            outs["r_p"].append(hr)
            outs["m_p"].append(hi)
            y_s, hr, hi = _s5_scan(x[n_p:n_tok].reshape(bs, ss, dm), state_s5_re[j], state_s5_im[j], *s5_args)
            outs["r_s"].append(hr)
            outs["m_s"].append(hi)
            z = _gelu_mix(unify(y_p, y_s), x, s5_d[j], tr=256)
            mix = _glu(z, glu_w1[j].astype(BF16), glu_b1[j], glu_w2[j].astype(BF16), glu_b2[j], tm=ROW_TILE, tn=512)
        x, xb = _moe_block(x, mix, n_tok, ln1_g[layer], ln1_b[layer], ln2_g[layer], ln2_b[layer],
                           router_w[layer], router_b[layer], moe_w_gu[layer], moe_b_gu[layer],
                           moe_w_down[layer], moe_b_down[layer], alpha=alpha)
    st = lambda name: jnp.stack(outs[name])
    return (x[:n_p].reshape(bp, sp, dm), x[n_p:n_tok].reshape(bs, ss, dm),
            st("k_p"), st("k_s"), st("v_p"), st("v_s"), st("i_p"), st("i_s"), st("c_p"), st("c_s"),
            st("h_p"), st("h_s"), st("r_p"), st("r_s"), st("m_p"), st("m_s"))
```

```python
import functools
import math

import jax
import jax.numpy as jnp
from jax import lax
from jax.experimental import pallas as pl
from jax.experimental.pallas import tpu as pltpu

F32 = jnp.float32
BF16 = jnp.bfloat16
I32 = jnp.int32

HEAD_DIM = 128
N_KV_HEADS = 4
IDX_HEADS = 32
IDX_DIM = 128
IDX_Q_RANK = 768
TOPK_MAX = 256
LRU_HEADS = 16
CONV_W = 4
LRU_C = 8.0
S5_GROUP = 16
TOP_K = 4
SWIGLU_LIMIT = 7.0
SWIGLU_ALPHA = 1.702
LN_EPS = 1e-5

LANES = 128
SUBLANES = 8
VMEM_LIMIT_BYTES = 56 * 1024 * 1024

NEG_INF = float("-inf")
INT_MIN = -(2 ** 31)


def _params(*sem):
    return pltpu.CompilerParams(dimension_semantics=sem, vmem_limit_bytes=VMEM_LIMIT_BYTES)


def _nt_dot(a, b):
    return lax.dot_general(a, b, (((1,), (1,)), ((), ())), preferred_element_type=F32)


def _dot(a, b):
    return jnp.dot(a, b, preferred_element_type=F32)


def _mm_kernel(x_ref, w_ref, o_ref):
    o_ref[...] = _dot(x_ref[...].astype(BF16), w_ref[...]).astype(o_ref.dtype)


def _matmul(x, w, *, tm, tn, out_dtype=F32, name="matmul"):
    m, k = x.shape
    n = w.shape[1]
    return pl.pallas_call(
        _mm_kernel,
        out_shape=jax.ShapeDtypeStruct((m, n), out_dtype),
        grid=(m // tm, n // tn),
        in_specs=[pl.BlockSpec((tm, k), lambda i, j: (i, 0)),
                  pl.BlockSpec((k, tn), lambda i, j: (0, j))],
        out_specs=pl.BlockSpec((tm, tn), lambda i, j: (i, j)),
        compiler_params=_params("parallel", "arbitrary"),
        name=name,
    )(x, w)


def _mm2_kernel(xa_ref, xb_ref, wa_ref, wb_ref, o_ref):
    o_ref[...] = _dot(xa_ref[...], wa_ref[...]) + _dot(xb_ref[...], wb_ref[...])


def _matmul_2in(xa, xb, w, *, tm, tn, name):
    m, ka = xa.shape
    n = w.shape[1]
    return pl.pallas_call(
        _mm2_kernel,
        out_shape=jax.ShapeDtypeStruct((m, n), F32),
        grid=(m // tm, n // tn),
        in_specs=[pl.BlockSpec((tm, ka), lambda i, j: (i, 0)),
                  pl.BlockSpec((tm, ka), lambda i, j: (i, 0)),
                  pl.BlockSpec((ka, tn), lambda i, j: (0, j)),
                  pl.BlockSpec((ka, tn), lambda i, j: (1, j))],
        out_specs=pl.BlockSpec((tm, tn), lambda i, j: (i, j)),
        compiler_params=_params("parallel", "arbitrary"),
        name=name,
    )(xa, xb, w, w)


def _glu_kernel(z_ref, w1_ref, b1_ref, w2_ref, b2_ref, o_ref):
    z = z_ref[...]
    a = _dot(z, w1_ref[...]) + b1_ref[...]
    b = _dot(z, w2_ref[...]) + b2_ref[...]
    o_ref[...] = a * jax.nn.sigmoid(b)


def _glu(z, w1, b1, w2, b2, *, tm, tn):
    m, k = z.shape
    n = w1.shape[1]
    x_spec = pl.BlockSpec((tm, k), lambda i, j: (i, 0))
    w_spec = pl.BlockSpec((k, tn), lambda i, j: (0, j))
    b_spec = pl.BlockSpec((1, tn), lambda i, j: (0, j))
    return pl.pallas_call(
        _glu_kernel,
        out_shape=jax.ShapeDtypeStruct((m, n), F32),
        grid=(m // tm, n // tn),
        in_specs=[x_spec, w_spec, b_spec, w_spec, b_spec],
        out_specs=pl.BlockSpec((tm, tn), lambda i, j: (i, j)),
        compiler_params=_params("parallel", "arbitrary"),
        name="glu",
    )(z, w1, b1.reshape(1, n), w2, b2.reshape(1, n))


def _ln_rows(x, g, b):
    mu = jnp.mean(x, axis=-1, keepdims=True)
    xc = x - mu
    var = jnp.mean(xc * xc, axis=-1, keepdims=True)
    return xc * lax.rsqrt(var + LN_EPS) * g + b


def _ln_kernel(x_ref, g_ref, b_ref, o_ref):
    o_ref[...] = _ln_rows(x_ref[...], g_ref[...], b_ref[...])


def _layer_norm_cols(x, g, b, *, tr, col_block, width):
    m = x.shape[0]
    return pl.pallas_call(
        _ln_kernel,
        out_shape=jax.ShapeDtypeStruct((m, width), F32),
        grid=(m // tr,),
        in_specs=[pl.BlockSpec((tr, width), lambda i: (i, col_block)),
                  pl.BlockSpec((1, width), lambda i: (0, 0)),
                  pl.BlockSpec((1, width), lambda i: (0, 0))],
        out_specs=pl.BlockSpec((tr, width), lambda i: (i, 0)),
        compiler_params=_params("parallel"),
        name="layer_norm",
    )(x, g.reshape(1, width), b.reshape(1, width))


def _top_gates(logits, n_experts):
    lane_f = lax.broadcasted_iota(I32, logits.shape, 1).astype(F32)
    l = jnp.where(lane_f < float(n_experts), logits, NEG_INF)
    vals, ids = [], []
    for _ in range(TOP_K):
        m = jnp.max(l, axis=-1, keepdims=True)
        idx = jnp.min(jnp.where(l == m, lane_f, float(LANES)), axis=-1, keepdims=True)
        vals.append(m)
        ids.append(idx)
        l = jnp.where(lane_f == idx, NEG_INF, l)
    es = [jnp.exp(v - vals[0]) for v in vals]
    den = es[0]
    for e in es[1:]:
        den = den + e
    return ids, [e / den for e in es]


def _ln_router_kernel(x_ref, mix_ref, g_ref, b_ref, rw_ref, rb_ref, tri_ref, o_ref, id_ref, gate_ref, rank_ref,
                      cnt_ref, *, alpha, n_experts, n_tokens):
    i = pl.program_id(0)
    tr = x_ref.shape[0]
    y = _ln_rows(alpha * x_ref[...] + mix_ref[...], g_ref[...], b_ref[...])
    o_ref[...] = y
    logits = _dot(y.astype(BF16), rw_ref[...]) + rb_ref[...]
    ids, gates = _top_gates(logits, n_experts)

    lane = lax.broadcasted_iota(I32, logits.shape, 1)
    lane_f = lane.astype(F32)
    valid = (i * tr + lax.broadcasted_iota(I32, logits.shape, 0)) < n_tokens
    chosen = jnp.zeros(logits.shape, F32)
    for r in range(TOP_K):
        chosen = chosen + jnp.where(lane_f == ids[r], 1.0, 0.0)
    chosen = jnp.where(valid, chosen, 0.0)

    @pl.when(i == 0)
    def _():
        cnt_ref[...] = jnp.zeros(cnt_ref.shape, F32)

    before = _dot(tri_ref[...], chosen.astype(BF16)) + cnt_ref[...]
    cnt_ref[...] = cnt_ref[...] + jnp.sum(chosen, axis=0, keepdims=True)

    idv = jnp.zeros(logits.shape, I32)
    gv = jnp.zeros(logits.shape, F32)
    rv = jnp.zeros(logits.shape, I32)
    for r in range(TOP_K):
        rank = jnp.sum(jnp.where(lane_f == ids[r], before, 0.0), axis=-1, keepdims=True)
        idv = jnp.where(lane == r, ids[r].astype(I32), idv)
        gv = jnp.where(lane == r, gates[r], gv)
        rv = jnp.where(lane == r, rank.astype(I32), rv)
    id_ref[...] = idv
    gate_ref[...] = gv
    rank_ref[...] = rv


def _ln_router(x, mix, g, b, rw, rb, *, alpha, tr, n_tokens):
    m, d = x.shape
    n_experts = rw.shape[1]
    rw_p = jnp.zeros((d, LANES), BF16).at[:, :n_experts].set(rw.astype(BF16))
    rb_p = jnp.zeros((1, LANES), F32).at[0, :n_experts].set(rb)
    tri = (jnp.arange(tr)[None, :] < jnp.arange(tr)[:, None]).astype(BF16)
    row = lambda i: (i, 0)
    fixed = lambda i: (0, 0)
    lanes_i = jax.ShapeDtypeStruct((m, LANES), I32)
    return pl.pallas_call(
        functools.partial(_ln_router_kernel, alpha=alpha, n_experts=n_experts, n_tokens=n_tokens),
        out_shape=(jax.ShapeDtypeStruct((m, d), F32), lanes_i, jax.ShapeDtypeStruct((m, LANES), F32), lanes_i,
                   jax.ShapeDtypeStruct((1, LANES), F32)),
        grid=(m // tr,),
        in_specs=[pl.BlockSpec((tr, d), row), pl.BlockSpec((tr, d), row),
                  pl.BlockSpec((1, d), fixed), pl.BlockSpec((1, d), fixed),
                  pl.BlockSpec((d, LANES), fixed), pl.BlockSpec((1, LANES), fixed),
                  pl.BlockSpec((tr, tr), fixed)],
        out_specs=(pl.BlockSpec((tr, d), row), pl.BlockSpec((tr, LANES), row), pl.BlockSpec((tr, LANES), row),
                   pl.BlockSpec((tr, LANES), row), pl.BlockSpec((1, LANES), fixed)),
        compiler_params=_params("arbitrary"),
        name="ln_router",
    )(x, mix, g.reshape(1, d), b.reshape(1, d), rw_p, rb_p, tri)


def _sortable_key(x):
    bits = pltpu.bitcast(x, I32)
    return jnp.where(bits < 0, bits ^ jnp.int32(0x7FFFFFFF), bits)


def _count(mask):
    return jnp.sum(jnp.where(mask, 1.0, 0.0), axis=-1, keepdims=True)


def _kth_largest_key(keys, k):
    def body(i, t):
        cand = t + lax.shift_left(jnp.int32(1), jnp.int32(31) - i)
        cnt = _count(keys[0] >= cand)
        for kk in keys[1:]:
            cnt = cnt + _count(kk >= cand)
        return jnp.where(cnt >= float(k), cand, t)

    t0 = jnp.full((keys[0].shape[0], 1), INT_MIN, I32)
    return lax.fori_loop(0, 32, body, t0)


def _strict_upper(n):
    return (jnp.arange(n)[:, None] < jnp.arange(n)[None, :]).astype(BF16)


CAUSAL_CLASSES = 4


def _dsa_prompt_kernel(q_ref, k_ref, v_ref, qi_ref, wi_ref, ki_ref, tri_ref, o_ref, *, topk, n_heads, w_scale):
    i = pl.program_id(1)
    tq = q_ref.shape[0]
    seq = k_ref.shape[0]
    n_classes = CAUSAL_CLASSES if (seq // tq) % CAUSAL_CLASSES == 0 else 1
    per_class = (seq // tq) // n_classes
    for c in range(n_classes):
        @pl.when(i // per_class == c)
        def _(c=c):
            _dsa_prompt_body(q_ref, k_ref, v_ref, qi_ref, wi_ref, ki_ref, tri_ref, o_ref, i,
                             (c + 1) * per_class * tq, topk=topk, n_heads=n_heads, w_scale=w_scale)


def _dsa_prompt_body(q_ref, k_ref, v_ref, qi_ref, wi_ref, ki_ref, tri_ref, o_ref, i, s_len, *, topk, n_heads,
                     w_scale):
    tq = q_ref.shape[0]
    kidx_bf = ki_ref[0:s_len, :].astype(BF16)
    w = wi_ref[...][:, :IDX_HEADS] * w_scale
    sc = jnp.zeros((tq, s_len), F32)
    for h in range(IDX_HEADS):
        d = _nt_dot(qi_ref[:, h * IDX_DIM:(h + 1) * IDX_DIM], kidx_bf)
        sc = sc + jnp.maximum(d, 0.0) * w[:, h:h + 1]
    row = i * tq + lax.broadcasted_iota(I32, (tq, s_len), 0)
    col = lax.broadcasted_iota(I32, (tq, s_len), 1)
    causal = col <= row
    key = _sortable_key(jnp.where(causal, sc, NEG_INF))
    thr = _kth_largest_key([key], topk)
    gt = key > thr
    eq = key == thr
    need = float(topk) - _count(gt)
    rank = _dot(jnp.where(eq, 1.0, 0.0).astype(BF16), tri_ref[0:s_len, 0:s_len])
    sel = (gt | (eq & (rank < need))) & causal
    bias = jnp.where(sel, 0.0, NEG_INF)
    group = n_heads // N_KV_HEADS
    for kv in range(N_KV_HEADS):
        kh = k_ref[0:s_len, kv * HEAD_DIM:(kv + 1) * HEAD_DIM].astype(BF16)
        vh = v_ref[0:s_len, kv * HEAD_DIM:(kv + 1) * HEAD_DIM].astype(BF16)
        for g in range(group):
            hh = kv * group + g
            qh = q_ref[:, hh * HEAD_DIM:(hh + 1) * HEAD_DIM].astype(BF16)
            s = _nt_dot(qh, kh) * (HEAD_DIM ** -0.5) + bias
            m = jnp.max(s, axis=-1, keepdims=True)
            p = jnp.exp(s - m)
            l = jnp.sum(p, axis=-1, keepdims=True)
            o = _dot(p.astype(BF16), vh) / l
            o_ref[:, hh * HEAD_DIM:(hh + 1) * HEAD_DIM] = o.astype(o_ref.dtype)


def _dsa_prompt(proj, q_idx, k_idx, lay, *, bsz, seq, tq):
    topk = min(TOPK_MAX, seq // 4)
    q_w = lay["q_w"]
    kv_w = N_KV_HEADS * HEAD_DIM
    nq = seq // tq
    return pl.pallas_call(
        functools.partial(_dsa_prompt_kernel, topk=topk, n_heads=q_w // HEAD_DIM,
                          w_scale=(IDX_HEADS * IDX_DIM) ** -0.5),
        out_shape=jax.ShapeDtypeStruct((bsz * seq, q_w), BF16),
        grid=(bsz, nq),
        in_specs=[pl.BlockSpec((tq, q_w), lambda b, i: (b * nq + i, lay["q"] // q_w)),
                  pl.BlockSpec((seq, kv_w), lambda b, i: (b, lay["k"] // kv_w)),
                  pl.BlockSpec((seq, kv_w), lambda b, i: (b, lay["v"] // kv_w)),
                  pl.BlockSpec((tq, IDX_HEADS * IDX_DIM), lambda b, i: (b * nq + i, 0)),
                  pl.BlockSpec((tq, LANES), lambda b, i: (b * nq + i, lay["wi"] // LANES)),
                  pl.BlockSpec((seq, IDX_DIM), lambda b, i: (b, 0)),
                  pl.BlockSpec((seq, seq), lambda b, i: (0, 0))],
        out_specs=pl.BlockSpec((tq, q_w), lambda b, i: (b * nq + i, 0)),
        compiler_params=_params("parallel", "arbitrary"),
        name="dsa_prompt",
    )(proj, proj, proj, q_idx, proj, k_idx, _strict_upper(seq))


PAGES_PER_STEP = 8


def _page_scores(qi_rows, w_rows, page_bf, t_new):
    d = jnp.maximum(_nt_dot(qi_rows, page_bf), 0.0) * w_rows
    return jnp.concatenate(
        [jnp.sum(d[t * IDX_HEADS:(t + 1) * IDX_HEADS], axis=0, keepdims=True) for t in range(t_new)], axis=0)


def _dsa_sample_scores_kernel(pt_ref, qi_ref, w_ref, *rest, t_new, page):
    pages = rest[:PAGES_PER_STEP]
    knew_ref, sc_ref, scn_ref = rest[PAGES_PER_STEP:]
    p = pl.program_id(1)
    qi = qi_ref[0]
    w = w_ref[0]
    for g in range(PAGES_PER_STEP):
        sc_ref[0, :, g * page:(g + 1) * page] = _page_scores(qi, w, pages[g][0, 0].astype(BF16), t_new)

    @pl.when(p == pl.num_programs(1) - 1)
    def _():
        sc = _page_scores(qi, w, knew_ref[0].astype(BF16), t_new)
        row = lax.broadcasted_iota(I32, sc.shape, 0)
        col = lax.broadcasted_iota(I32, sc.shape, 1)
        scn_ref[0] = jnp.where(col <= row, sc, NEG_INF)


def _dsa_sample_attn_kernel(pt_ref, sc_ref, scn_ref, q_ref, *rest, t_new, page, topk, n_heads):
    kp = rest[:PAGES_PER_STEP]
    vp = rest[PAGES_PER_STEP:2 * PAGES_PER_STEP]
    knew_ref, vnew_ref, tri_ref, o_ref, thr_s, need_s, cnt_s, m_s, l_s, acc_s, bias_s, flag_s = rest[2 * PAGES_PER_STEP:]
    p = pl.program_id(1)
    chunk = PAGES_PER_STEP * page
    rows = t_new * n_heads

    @pl.when(p == 0)
    def _():
        key_p = _sortable_key(sc_ref[0])
        key_n = _sortable_key(scn_ref[0])
        thr = _kth_largest_key([key_p, key_n], topk)
        n_gt = _count(key_p > thr) + _count(key_n > thr)
        n_eq = _count(key_p == thr) + _count(key_n == thr)
        need = float(topk) - n_gt
        thr_s[...] = jnp.broadcast_to(thr, thr_s.shape)
        need_s[...] = jnp.broadcast_to(need, need_s.shape)
        cnt_s[...] = jnp.zeros(cnt_s.shape, F32)
        flag_s[0] = (jnp.max(n_eq - need) > 0.0).astype(I32)
        m_s[...] = jnp.full(m_s.shape, NEG_INF, F32)
        l_s[...] = jnp.zeros(l_s.shape, F32)
        acc_s[...] = jnp.zeros(acc_s.shape, F32)

    thr = thr_s[:, 0:1]
    q = [q_ref[0, kv] for kv in range(N_KV_HEADS)]

    def select(sc, width):
        key = _sortable_key(sc)
        bias_s[:, :width] = jnp.where(key >= thr, 0.0, NEG_INF)

        @pl.when(flag_s[0] != 0)
        def _():
            eq = key == thr
            eqf = jnp.where(eq, 1.0, 0.0)
            rank = _dot(eqf.astype(BF16), tri_ref[:width, :width]) + cnt_s[:, 0:1]
            sel = (key > thr) | (eq & (rank < need_s[:, 0:1]))
            bias_s[:, :width] = jnp.where(sel, 0.0, NEG_INF)
            cnt_s[...] = cnt_s[...] + jnp.sum(eqf, axis=-1, keepdims=True)

    def attend(k_kv, v_kv, bias_t):
        bias = jnp.concatenate([jnp.broadcast_to(bias_t[t:t + 1], (n_heads, bias_t.shape[1])) for t in range(t_new)],
                               axis=0)
        s = _nt_dot(q[0], k_kv[0])
        for kv in range(1, N_KV_HEADS):
            s = s + _nt_dot(q[kv], k_kv[kv])
        s = s * (HEAD_DIM ** -0.5) + bias
        m_old = m_s[:, 0:1]
        m_new = jnp.maximum(m_old, jnp.max(s, axis=-1, keepdims=True))
        m_safe = jnp.where(m_new == NEG_INF, 0.0, m_new)
        alpha = jnp.exp(m_old - m_safe)
        pr = jnp.exp(s - m_safe)
        l_s[...] = alpha * l_s[...] + jnp.sum(pr, axis=-1, keepdims=True)
        pb = pr.astype(BF16)
        for kv in range(N_KV_HEADS):
            cols = slice(kv * HEAD_DIM, (kv + 1) * HEAD_DIM)
            acc_s[:, cols] = alpha * acc_s[:, cols] + _dot(pb, v_kv[kv])
        m_s[...] = jnp.broadcast_to(m_new, m_s.shape)

    def page_heads(ref):
        return [ref[0, 0, :, kv, :].astype(BF16) for kv in range(N_KV_HEADS)]

    def row_heads(ref):
        return [ref[0, :, kv * HEAD_DIM:(kv + 1) * HEAD_DIM].astype(BF16) for kv in range(N_KV_HEADS)]

    start = pl.multiple_of(p * chunk, chunk)
    select(sc_ref[0, :, pl.ds(start, chunk)], chunk)
    for g in range(PAGES_PER_STEP):
        attend(page_heads(kp[g]), page_heads(vp[g]), bias_s[:, g * page:(g + 1) * page])

    @pl.when(p == pl.num_programs(1) - 1)
    def _():
        select(scn_ref[0], page)
        attend(row_heads(knew_ref), row_heads(vnew_ref), bias_s[:, :page])
        o_ref[0] = acc_s[...] / l_s[:, 0:1]


def _dsa_sample(q, k, v, q_idx, w_idx, k_idx, cache_k, cache_v, cache_kidx, page_table, layer):
    bsz, t_new, q_w = q.shape
    n_heads = q_w // HEAD_DIM
    kv_w = N_KV_HEADS * HEAD_DIM
    page = cache_k.shape[2]
    n_pages = page_table.shape[1]
    past = n_pages * page
    topk = min(TOPK_MAX, (past + t_new) // 4)
    steps = n_pages // PAGES_PER_STEP
    chunk = PAGES_PER_STEP * page
    rows = t_new * n_heads

    qi_rows = q_idx.reshape(bsz, t_new * IDX_HEADS, IDX_DIM).astype(BF16)
    w_rows = (w_idx * (IDX_HEADS * IDX_DIM) ** -0.5).reshape(bsz, t_new * IDX_HEADS, 1)
    pad_rows = lambda a: jnp.pad(a, ((0, 0), (0, page - t_new), (0, 0)))
    kidx_new, k_new, v_new = pad_rows(k_idx), pad_rows(k), pad_rows(v)

    def page_spec(trailing, g):
        zeros = (0,) * (1 + len(trailing))
        return pl.BlockSpec((1, 1, page) + trailing,
                            lambda b, p, pt: (layer, pt[b, p * PAGES_PER_STEP + g]) + zeros)

    per_b = lambda b, p, pt: (b, 0, 0)
    sc_past, sc_new = pl.pallas_call(
        functools.partial(_dsa_sample_scores_kernel, t_new=t_new, page=page),
        out_shape=(jax.ShapeDtypeStruct((bsz, t_new, past), F32),
                   jax.ShapeDtypeStruct((bsz, t_new, page), F32)),
        grid_spec=pltpu.PrefetchScalarGridSpec(
            num_scalar_prefetch=1,
            grid=(bsz, steps),
            in_specs=[pl.BlockSpec((1, t_new * IDX_HEADS, IDX_DIM), per_b),
                      pl.BlockSpec((1, t_new * IDX_HEADS, 1), per_b)]
                     + [page_spec((IDX_DIM,), g) for g in range(PAGES_PER_STEP)]
                     + [pl.BlockSpec((1, page, IDX_DIM), per_b)],
            out_specs=(pl.BlockSpec((1, t_new, chunk), lambda b, p, pt: (b, 0, p)),
                       pl.BlockSpec((1, t_new, page), per_b))),
        compiler_params=_params("parallel", "arbitrary"),
        name="dsa_sample_scores",
    )(page_table, qi_rows, w_rows, *([cache_kidx] * PAGES_PER_STEP), kidx_new)

    group = n_heads // N_KV_HEADS
    head_kv = jnp.arange(n_heads) // group
    blockdiag = (head_kv[:, None] == jnp.arange(N_KV_HEADS)[None, :]).astype(F32)
    q4 = q.reshape(bsz, t_new, n_heads, 1, HEAD_DIM) * blockdiag[None, None, :, :, None]
    q_bd = q4.transpose(0, 3, 1, 2, 4).reshape(bsz, N_KV_HEADS, rows, HEAD_DIM).astype(BF16)

    out = pl.pallas_call(
        functools.partial(_dsa_sample_attn_kernel, t_new=t_new, page=page, topk=topk, n_heads=n_heads),
        out_shape=jax.ShapeDtypeStruct((bsz, rows, kv_w), F32),
        grid_spec=pltpu.PrefetchScalarGridSpec(
            num_scalar_prefetch=1,
            grid=(bsz, steps),
            in_specs=[pl.BlockSpec((1, t_new, past), per_b),
                      pl.BlockSpec((1, t_new, page), per_b),
                      pl.BlockSpec((1, N_KV_HEADS, rows, HEAD_DIM), lambda b, p, pt: (b, 0, 0, 0))]
                     + [page_spec((N_KV_HEADS, HEAD_DIM), g) for g in range(PAGES_PER_STEP)]
                     + [page_spec((N_KV_HEADS, HEAD_DIM), g) for g in range(PAGES_PER_STEP)]
                     + [pl.BlockSpec((1, page, kv_w), per_b), pl.BlockSpec((1, page, kv_w), per_b),
                        pl.BlockSpec((chunk, chunk), lambda b, p, pt: (0, 0))],
            out_specs=pl.BlockSpec((1, rows, kv_w), per_b),
            scratch_shapes=[pltpu.VMEM((t_new, LANES), I32), pltpu.VMEM((t_new, LANES), F32),
                            pltpu.VMEM((t_new, LANES), F32), pltpu.VMEM((rows, LANES), F32),
                            pltpu.VMEM((rows, LANES), F32), pltpu.VMEM((rows, kv_w), F32),
                            pltpu.VMEM((t_new, chunk), F32), pltpu.SMEM((1,), I32)]),
        compiler_params=_params("parallel", "arbitrary"),
        name="dsa_sample_attn",
    )(page_table, sc_past, sc_new, q_bd, *([cache_k] * PAGES_PER_STEP), *([cache_v] * PAGES_PER_STEP),
      k_new, v_new, _strict_upper(chunk))
    o5 = out.reshape(bsz, t_new, N_KV_HEADS, group, N_KV_HEADS, HEAD_DIM)
    o = jnp.stack([o5[:, :, kv, :, kv, :] for kv in range(N_KV_HEADS)], axis=2)
    return o.reshape(bsz, t_new, q_w)


def _softplus(x):
    return jnp.maximum(x, 0.0) + jnp.log1p(jnp.exp(-jnp.abs(x)))


def _rglru_kernel(u_ref, g_ref, cp_ref, h0_ref, cw_ref, cb_ref, wa_ref, ba_ref, wx_ref, bx_ref, lam_ref,
                  y_ref, hl_ref, up_s, a_s, b_s, *, t_valid, t_chunk):
    t_len, wb = u_ref.shape
    lead = SUBLANES - (CONV_W - 1)
    up_s[0:SUBLANES, :] = jnp.zeros((SUBLANES, wb), F32)
    up_s[lead:SUBLANES, :] = cp_ref[0]
    up_s[SUBLANES:SUBLANES + t_len, :] = u_ref[...]
    neg_c_sp = -LRU_C * _softplus(-lam_ref[...])
    for c0 in range(0, t_len, t_chunk):
        xc = cw_ref[0:1, :] * up_s[lead + c0:lead + c0 + t_chunk, :]
        for j in range(1, CONV_W):
            xc = xc + cw_ref[j:j + 1, :] * up_s[lead + j + c0:lead + j + c0 + t_chunk, :]
        xc = cb_ref[...] + xc
        ra, ix = [], []
        for hd in range(wb // LANES):
            xh = xc[:, hd * LANES:(hd + 1) * LANES].astype(BF16)
            ra.append(_dot(xh, wa_ref[hd]))
            ix.append(_dot(xh, wx_ref[hd]))
        r = jax.nn.sigmoid(jnp.concatenate(ra, axis=-1) + ba_ref[...])
        gi = jax.nn.sigmoid(jnp.concatenate(ix, axis=-1) + bx_ref[...])
        log_a = r * neg_c_sp
        a = jnp.exp(log_a)
        a_s[c0:c0 + t_chunk, :] = a
        b_s[c0:c0 + t_chunk, :] = jnp.sqrt(-jnp.tanh(log_a) * (a * a + 1.0)) * (gi * xc)

    def step(t, h):
        h = a_s[pl.ds(t, 1), :] * h + b_s[pl.ds(t, 1), :]
        b_s[pl.ds(t, 1), :] = h
        return h

    lax.fori_loop(0, t_len, step, h0_ref[0], unroll=8)
    hl_ref[0] = b_s[t_valid - 1:t_valid, :]
    for c0 in range(0, t_len, t_chunk):
        rows = slice(c0, c0 + t_chunk)
        y_ref[rows, :] = (b_s[rows, :] * jax.nn.gelu(g_ref[rows, :])).astype(y_ref.dtype)


def _rglru(u_src, g_src, conv_prev, h0, conv_w, conv_b, wa, ba, wx, bx, lam, *, bsz, t_len, t_valid,
           u_col, g_col, width, wb):
    nb = width // wb
    heads_per = wb // LANES
    vec = lambda a: a.reshape(1, width)
    vspec = pl.BlockSpec((1, wb), lambda b, c: (0, c))
    t_chunk = min(t_len, 256)
    y, h_last = pl.pallas_call(
        functools.partial(_rglru_kernel, t_valid=t_valid, t_chunk=t_chunk),
        out_shape=(jax.ShapeDtypeStruct((bsz * t_len, width), BF16),
                   jax.ShapeDtypeStruct((bsz, 1, width), F32)),
        grid=(bsz, nb),
        in_specs=[pl.BlockSpec((t_len, wb), lambda b, c: (b, u_col // wb + c)),
                  pl.BlockSpec((t_len, wb), lambda b, c: (b, g_col // wb + c)),
                  pl.BlockSpec((1, CONV_W - 1, wb), lambda b, c: (b, 0, c)),
                  pl.BlockSpec((1, 1, wb), lambda b, c: (b, 0, c)),
                  pl.BlockSpec((CONV_W, wb), lambda b, c: (0, c)),
                  vspec,
                  pl.BlockSpec((heads_per, LANES, LANES), lambda b, c: (c, 0, 0)),
                  vspec,
                  pl.BlockSpec((heads_per, LANES, LANES), lambda b, c: (c, 0, 0)),
                  vspec, vspec],
        out_specs=(pl.BlockSpec((t_len, wb), lambda b, c: (b, c)),
                   pl.BlockSpec((1, 1, wb), lambda b, c: (b, 0, c))),
        scratch_shapes=[pltpu.VMEM((t_len + SUBLANES, wb), F32), pltpu.VMEM((t_len, wb), F32),
                        pltpu.VMEM((t_len, wb), F32)],
        compiler_params=_params("parallel", "arbitrary"),
        name="rglru",
    )(u_src, g_src, conv_prev, h0.reshape(bsz, 1, width), conv_w, vec(conv_b), wa.astype(BF16), vec(ba),
      wx.astype(BF16), vec(bx), vec(lam))
    return y, h_last.reshape(bsz, width)


def _s5_disc_kernel(are_ref, aim_ref, ls_ref, abr_ref, abi_ref, cfr_ref, cfi_ref):
    a_re, a_im = are_ref[...], aim_ref[...]
    dt = jnp.exp(ls_ref[...])
    mag = jnp.exp(a_re * dt)
    ab_re = mag * jnp.cos(a_im * dt)
    ab_im = mag * jnp.sin(a_im * dt)
    den = a_re * a_re + a_im * a_im
    nr, ni = ab_re - 1.0, ab_im
    abr_ref[...] = ab_re
    abi_ref[...] = ab_im
    cfr_ref[...] = (nr * a_re + ni * a_im) / den
    cfi_ref[...] = (ni * a_re - nr * a_im) / den


def _s5_discretize(a_re, a_im, log_step):
    g, n = a_re.shape
    shp = jax.ShapeDtypeStruct((g, n), F32)
    return pl.pallas_call(_s5_disc_kernel, out_shape=(shp, shp, shp, shp), name="s5_discretize")(
        a_re, a_im, jnp.broadcast_to(log_step[:, None], (g, n)))


def _s5_kernel(u_ref, wb_ref, wc_ref, are_ref, aim_ref, d_ref, h0r_ref, h0i_ref, z_ref, hr_ref, hi_ref, x_s, st_s,
               *, nj, t_chunk):
    c = pl.program_id(1)
    ns = are_ref.shape[2]

    @pl.when(c == 0)
    def _():
        st_s[0] = h0r_ref[0]
        st_s[1] = h0i_ref[0]

    u = u_ref[0]
    rowj = (lax.broadcasted_iota(I32, u.shape, 0) % SUBLANES) // (SUBLANES // nj)
    u_bf = u.astype(BF16)
    if nj > 1:
        zero = jnp.zeros(u_bf.shape, BF16)
        u_bf = jnp.concatenate([jnp.where(rowj == j, u_bf, zero) for j in range(nj)], axis=1)
    x_s[...] = _dot(u_bf, wb_ref[0])
    ar, ai = are_ref[0], aim_ref[0]

    def step(t, carry):
        xr, xi = carry
        r0 = pl.multiple_of(t * SUBLANES, SUBLANES)
        nr = ar * xr - ai * xi + x_s[pl.ds(r0, SUBLANES), 0:ns]
        ni = ar * xi + ai * xr + x_s[pl.ds(r0, SUBLANES), ns:2 * ns]
        x_s[pl.ds(r0, SUBLANES), 0:ns] = nr
        x_s[pl.ds(r0, SUBLANES), ns:2 * ns] = ni
        return nr, ni

    xr, xi = lax.fori_loop(0, t_chunk, step, (st_s[0], st_s[1]), unroll=min(t_chunk, 4))
    st_s[0] = xr
    st_s[1] = xi
    yb = _dot(x_s[...].astype(BF16), wc_ref[0])
    y = yb[:, 0:LANES]
    dvec = jnp.broadcast_to(d_ref[0, 0:1, :], y.shape)
    for j in range(1, nj):
        y = jnp.where(rowj == j, yb[:, j * LANES:(j + 1) * LANES], y)
        dvec = jnp.where(rowj == j, d_ref[0, j:j + 1, :], dvec)
    z_ref[0] = jax.nn.gelu(y + dvec * u).astype(z_ref.dtype)

    @pl.when(c == pl.num_programs(1) - 1)
    def _():
        hr_ref[0] = xr
        hi_ref[0] = xi


def _s5_scan(u, h0_re, h0_im, ab_re, ab_im, bb_re, bb_im, c_re, c_im, d_skip):
    bsz, t_len, d = u.shape
    groups, n_state = ab_re.shape
    gl = LANES // S5_GROUP
    ns = gl * n_state
    nj = SUBLANES // bsz
    nq = d // (LANES * nj)
    t_chunk = min(t_len, 256)
    eye_g = jnp.eye(gl, dtype=F32)

    u_rows = u.reshape(bsz, t_len, nq, nj, LANES).transpose(2, 1, 3, 0, 4).reshape(nq, t_len * SUBLANES, LANES)

    def b_weight(bb):
        w = bb.reshape(nq, nj, gl, n_state, S5_GROUP).transpose(0, 1, 2, 4, 3)
        w = w[:, :, :, :, None, :] * eye_g[None, None, :, None, :, None]
        return w.reshape(nq, nj * LANES, ns)

    w_b = jnp.concatenate([b_weight(bb_re), b_weight(bb_im)], axis=-1).astype(BF16)

    def c_weight(cc):
        w = cc.reshape(nq, nj, gl, S5_GROUP, n_state).transpose(0, 2, 4, 1, 3)
        w = w[:, :, :, :, None, :] * eye_g[None, :, None, None, :, None]
        return w.reshape(nq, ns, nj * LANES)

    w_c = jnp.concatenate([c_weight(c_re), -c_weight(c_im)], axis=1).astype(BF16)

    def coef(a):
        a = a.reshape(nq, nj, 1, ns)
        return jnp.broadcast_to(a, (nq, nj, bsz, ns)).reshape(nq, SUBLANES, ns)

    def state_in(h):
        return h.reshape(bsz, nq, nj, ns).transpose(1, 2, 0, 3).reshape(nq, SUBLANES, ns)

    def state_out(h):
        return h.reshape(nq, nj, bsz, ns).transpose(2, 0, 1, 3).reshape(bsz, groups, n_state)

    rows = t_chunk * SUBLANES
    per_q = lambda q, c: (q, 0, 0)
    st_spec = pl.BlockSpec((1, SUBLANES, ns), per_q)
    st_shape = jax.ShapeDtypeStruct((nq, SUBLANES, ns), F32)
    z, hr, hi = pl.pallas_call(
        functools.partial(_s5_kernel, nj=nj, t_chunk=t_chunk),
        out_shape=(jax.ShapeDtypeStruct((nq, t_len * SUBLANES, LANES), BF16), st_shape, st_shape),
        grid=(nq, t_len // t_chunk),
        in_specs=[pl.BlockSpec((1, rows, LANES), lambda q, c: (q, c, 0)),
                  pl.BlockSpec((1, nj * LANES, 2 * ns), per_q),
                  pl.BlockSpec((1, 2 * ns, nj * LANES), per_q),
                  st_spec, st_spec,
                  pl.BlockSpec((1, nj, LANES), per_q),
                  st_spec, st_spec],
        out_specs=(pl.BlockSpec((1, rows, LANES), lambda q, c: (q, c, 0)), st_spec, st_spec),
        scratch_shapes=[pltpu.VMEM((rows, 2 * ns), F32), pltpu.VMEM((2, SUBLANES, ns), F32)],
        compiler_params=_params("parallel", "arbitrary"),
        name="s5_scan",
    )(u_rows, w_b, w_c, coef(ab_re), coef(ab_im), d_skip.reshape(nq, nj, LANES), state_in(h0_re), state_in(h0_im))
    z = z.reshape(nq, t_len, nj, bsz, LANES).transpose(3, 1, 0, 2, 4).reshape(bsz, t_len, d)
    return z, state_out(hr), state_out(hi)


MOE_TM = 256
COMBINE_ROWS = 32
DISPATCH_ROWS = 128


def _moe_plan(ids, ranks, counts, n_tokens, n_experts):
    rows = ids.shape[0]
    n_tiles = (n_tokens * TOP_K) // MOE_TM + n_experts
    spare_tiles = -(-(rows - n_tokens) * TOP_K // MOE_TM)
    tiles_per = (counts + MOE_TM - 1) // MOE_TM
    tile_end = jnp.cumsum(tiles_per)
    run_start = (tile_end - tiles_per) * MOE_TM
    chose = ids[..., None] == jnp.arange(n_experts, dtype=I32)
    dest = jnp.sum(jnp.where(chose, run_start, 0), axis=-1) + ranks
    r = jnp.arange(rows, dtype=I32)[:, None]
    spare = n_tiles * MOE_TM + (r - n_tokens) * TOP_K + jnp.arange(TOP_K, dtype=I32)[None, :]
    is_token = r < n_tokens
    dest_dispatch = jnp.where(is_token, dest, spare).reshape(-1)
    dest_combine = jnp.where(is_token, dest, 0).reshape(-1)
    n_valid = tile_end[-1]
    tile_ids = jnp.arange(n_tiles + spare_tiles, dtype=I32)
    tile_expert = jnp.sum(tile_end[None, :] <= jnp.minimum(tile_ids[:n_tiles], n_valid - 1)[:, None], axis=1)
    is_last = jnp.any((tiles_per > 0)[None, :] & (tile_end[None, :] - 1 == tile_ids[:, None]), axis=1)
    zero_fill = (is_last | (tile_ids >= n_valid)).astype(I32)
    return (dest_dispatch, dest_combine, tile_expert.astype(I32), n_valid.astype(I32).reshape(1), zero_fill,
            n_tiles, n_tiles + spare_tiles)


def _dispatch_kernel(dest_ref, zf_ref, x_ref, o_hbm, zbuf, sem, zsem, *, total_tiles):
    t = pl.program_id(0)
    tr = x_ref.shape[0]

    def zero_copy(tile):
        start = pl.multiple_of(tile * MOE_TM, MOE_TM)
        return pltpu.make_async_copy(zbuf, o_hbm.at[pl.ds(start, MOE_TM), :], zsem)

    @pl.when(t == 0)
    def _():
        zbuf[...] = jnp.zeros(zbuf.shape, zbuf.dtype)

        def start(tile, c):
            @pl.when(zf_ref[tile] != 0)
            def _():
                zero_copy(tile).start()
            return c

        def wait(tile, c):
            @pl.when(zf_ref[tile] != 0)
            def _():
                zero_copy(tile).wait()
            return c

        lax.fori_loop(0, total_tiles, start, 0)
        lax.fori_loop(0, total_tiles, wait, 0)

    def body(r, c):
        for k in range(TOP_K):
            row = dest_ref[(t * tr + r) * TOP_K + k]
            pltpu.make_async_copy(x_ref.at[pl.ds(r, 1), :], o_hbm.at[pl.ds(row, 1), :], sem).start()
        return c

    lax.fori_loop(0, tr, body, 0)
    for k in range(TOP_K):
        pltpu.make_async_copy(x_ref, o_hbm.at[pl.ds(0, tr), :], sem).wait()


def _dispatch(x, dest, zero_fill, total_tiles):
    m, d = x.shape
    return pl.pallas_call(
        functools.partial(_dispatch_kernel, total_tiles=total_tiles),
        out_shape=jax.ShapeDtypeStruct((total_tiles * MOE_TM, d), x.dtype),
        grid_spec=pltpu.PrefetchScalarGridSpec(
            num_scalar_prefetch=2,
            grid=(m // DISPATCH_ROWS,),
            in_specs=[pl.BlockSpec((DISPATCH_ROWS, d), lambda t, dst, zf: (t, 0))],
            out_specs=pl.BlockSpec(memory_space=pl.ANY),
            scratch_shapes=[pltpu.VMEM((MOE_TM, d), x.dtype), pltpu.SemaphoreType.DMA(()),
                            pltpu.SemaphoreType.DMA(())]),
        compiler_params=_params("arbitrary"),
        name="moe_dispatch",
    )(dest, zero_fill, x)


def _new_expert(te_ref, t):
    return (t == 0) | (te_ref[t] != te_ref[jnp.maximum(t - 1, 0)])


def _moe_gate_up_kernel(te_ref, nv_ref, x_ref, wg_ref, wu_ref, bg_ref, bu_ref, a_ref, wg_bf, wu_bf, *, k_chunk):
    t = pl.program_id(1)

    @pl.when(t < nv_ref[0])
    def _():
        d = x_ref.shape[1]

        @pl.when(_new_expert(te_ref, t))
        def _():
            for k0 in range(0, d, k_chunk):
                wg_bf[k0:k0 + k_chunk, :] = wg_ref[0, 0, k0:k0 + k_chunk, :].astype(BF16)
                wu_bf[k0:k0 + k_chunk, :] = wu_ref[0, 0, k0:k0 + k_chunk, :].astype(BF16)

        hg = jnp.zeros(a_ref.shape, F32)
        hu = jnp.zeros(a_ref.shape, F32)
        for k0 in range(0, d, k_chunk):
            xk = x_ref[:, k0:k0 + k_chunk].astype(BF16)
            hg = hg + _dot(xk, wg_bf[k0:k0 + k_chunk, :])
            hu = hu + _dot(xk, wu_bf[k0:k0 + k_chunk, :])
        gate = jnp.minimum(hg + bg_ref[0, 0], SWIGLU_LIMIT)
        up = jnp.clip(hu + bu_ref[0, 0], -SWIGLU_LIMIT, SWIGLU_LIMIT)
        a_ref[...] = ((up + 1.0) * gate * jax.nn.sigmoid(SWIGLU_ALPHA * gate)).astype(a_ref.dtype)

    @pl.when(t >= nv_ref[0])
    def _():
        a_ref[...] = jnp.zeros(a_ref.shape, a_ref.dtype)


def _moe_down_kernel(te_ref, nv_ref, a_ref, w_ref, b_ref, y_ref, w_bf, *, k_chunk):
    t = pl.program_id(1)

    @pl.when(t < nv_ref[0])
    def _():
        f = a_ref.shape[1]

        @pl.when(_new_expert(te_ref, t))
        def _():
            for k0 in range(0, f, k_chunk):
                w_bf[k0:k0 + k_chunk, :] = w_ref[0, 0, k0:k0 + k_chunk, :].astype(BF16)

        y_ref[...] = _dot(a_ref[...], w_bf[...]) + b_ref[0, 0]

    @pl.when(t >= nv_ref[0])
    def _():
        y_ref[...] = jnp.zeros(y_ref.shape, y_ref.dtype)


def _moe_experts(xs, tile_expert, n_valid, w_gu, b_gu, w_dn, b_dn, layer, n_tiles):
    _, n_experts, d, two_f = w_gu.shape
    f = two_f // 2
    tn1 = min(512, f)
    nb1 = f // tn1
    n_slots = n_tiles * MOE_TM
    b_gu4 = b_gu.reshape(b_gu.shape[0], n_experts, 1, two_f)
    last = lambda t, nv: jnp.minimum(t, nv[0] - 1)
    act = pl.pallas_call(
        functools.partial(_moe_gate_up_kernel, k_chunk=min(1024, d)),
        out_shape=jax.ShapeDtypeStruct((n_slots, f), BF16),
        grid_spec=pltpu.PrefetchScalarGridSpec(
            num_scalar_prefetch=2,
            grid=(nb1, n_tiles),
            in_specs=[pl.BlockSpec((MOE_TM, d), lambda j, t, te, nv: (last(t, nv), 0)),
                      pl.BlockSpec((1, 1, d, tn1), lambda j, t, te, nv: (layer, te[t], 0, j)),
                      pl.BlockSpec((1, 1, d, tn1), lambda j, t, te, nv: (layer, te[t], 0, nb1 + j)),
                      pl.BlockSpec((1, 1, 1, tn1), lambda j, t, te, nv: (layer, te[t], 0, j)),
                      pl.BlockSpec((1, 1, 1, tn1), lambda j, t, te, nv: (layer, te[t], 0, nb1 + j))],
            out_specs=pl.BlockSpec((MOE_TM, tn1), lambda j, t, te, nv: (t, j)),
            scratch_shapes=[pltpu.VMEM((d, tn1), BF16), pltpu.VMEM((d, tn1), BF16)]),
        compiler_params=_params("arbitrary", "arbitrary"),
        name="moe_gate_up",
    )(tile_expert, n_valid, xs, w_gu, w_gu, b_gu4, b_gu4)

    tn2 = min(2048, d)
    nb2 = d // tn2
    return pl.pallas_call(
        functools.partial(_moe_down_kernel, k_chunk=min(512, f)),
        out_shape=jax.ShapeDtypeStruct((n_slots, d), F32),
        grid_spec=pltpu.PrefetchScalarGridSpec(
            num_scalar_prefetch=2,
            grid=(nb2, n_tiles),
            in_specs=[pl.BlockSpec((MOE_TM, f), lambda j, t, te, nv: (last(t, nv), 0)),
                      pl.BlockSpec((1, 1, f, tn2), lambda j, t, te, nv: (layer, te[t], 0, j)),
                      pl.BlockSpec((1, 1, 1, tn2), lambda j, t, te, nv: (layer, te[t], 0, j))],
            out_specs=pl.BlockSpec((MOE_TM, tn2), lambda j, t, te, nv: (t, j)),
            scratch_shapes=[pltpu.VMEM((f, tn2), BF16)]),
        compiler_params=_params("arbitrary", "arbitrary"),
        name="moe_down",
    )(tile_expert, n_valid, act, w_dn, b_dn.reshape(b_dn.shape[0], n_experts, 1, d))


def _combine_ln_kernel(dest_ref, ys_hbm, gate_ref, x_ref, g_ref, b_ref, o_ref, ob_ref, buf, sem, *, alpha):
    t = pl.program_id(0)
    nt = pl.num_programs(0)
    tr = o_ref.shape[0]

    def issue(tile, slot):
        def body(r, carry):
            for k in range(TOP_K):
                row = dest_ref[(tile * tr + r) * TOP_K + k]
                pltpu.make_async_copy(ys_hbm.at[pl.ds(row, 1), :], buf.at[slot, k, pl.ds(r, 1), :],
                                      sem.at[slot]).start()
            return carry
        lax.fori_loop(0, tr, body, 0)

    @pl.when(t == 0)
    def _():
        issue(0, 0)

    @pl.when(t + 1 < nt)
    def _():
        issue(t + 1, (t + 1) % 2)

    slot = t % 2
    for k in range(TOP_K):
        pltpu.make_async_copy(ys_hbm.at[pl.ds(0, tr), :], buf.at[slot, k], sem.at[slot]).wait()
    gates = gate_ref[...]
    f = gates[:, 0:1] * buf[slot, 0]
    for k in range(1, TOP_K):
        f = f + gates[:, k:k + 1] * buf[slot, k]
    y = _ln_rows(alpha * x_ref[...] + f, g_ref[...], b_ref[...])
    o_ref[...] = y
    ob_ref[...] = y.astype(ob_ref.dtype)


def _combine_ln(ys, dest, gates, x1, g, b, *, alpha):
    m, d = x1.shape
    row = lambda t, dst: (t, 0)
    fixed = lambda t, dst: (0, 0)
    return pl.pallas_call(
        functools.partial(_combine_ln_kernel, alpha=alpha),
        out_shape=(jax.ShapeDtypeStruct((m, d), F32), jax.ShapeDtypeStruct((m, d), BF16)),
        grid_spec=pltpu.PrefetchScalarGridSpec(
            num_scalar_prefetch=1,
            grid=(m // COMBINE_ROWS,),
            in_specs=[pl.BlockSpec(memory_space=pl.ANY),
                      pl.BlockSpec((COMBINE_ROWS, LANES), row),
                      pl.BlockSpec((COMBINE_ROWS, d), row),
                      pl.BlockSpec((1, d), fixed), pl.BlockSpec((1, d), fixed)],
            out_specs=(pl.BlockSpec((COMBINE_ROWS, d), row), pl.BlockSpec((COMBINE_ROWS, d), row)),
            scratch_shapes=[pltpu.VMEM((2, TOP_K, COMBINE_ROWS, d), F32), pltpu.SemaphoreType.DMA((2,))]),
        compiler_params=_params("arbitrary"),
        name="moe_combine_ln",
    )(dest, ys, gates, x1, g.reshape(1, d), b.reshape(1, d))


def _moe_block(x, mix, n_tokens, layer, ln1_g, ln1_b, ln2_g, ln2_b, rw, rb, w_gu, b_gu, w_dn, b_dn, *, alpha):
    n_experts = rw.shape[1]
    x1, ids, gates, ranks, counts = _ln_router(x, mix, ln1_g, ln1_b, rw, rb, alpha=alpha, tr=256, n_tokens=n_tokens)
    dest_d, dest_c, tile_expert, n_valid, zero_fill, n_tiles, total_tiles = _moe_plan(
        ids[:, :TOP_K], ranks[:, :TOP_K], counts[0, :n_experts].astype(I32), n_tokens, n_experts)
    xs = _dispatch(x1, dest_d, zero_fill, total_tiles)
    ys = _moe_experts(xs, tile_expert, n_valid, w_gu, b_gu, w_dn, b_dn, layer, n_tiles)
    return _combine_ln(ys, dest_c, gates, x1, ln2_g, ln2_b, alpha=alpha)


ROW_TILE = 768


def kernel(x_prompt, x_sample, cache_k, cache_v, cache_kidx, state_conv, state_lru, state_s5_re, state_s5_im,
           page_table, w_in, w_idx_qb, kidx_ln_g, kidx_ln_b, conv_w, conv_b, lru_wa, lru_ba, lru_wx, lru_bx,
           lru_lambda, w_out, s5_a_re, s5_a_im, s5_log_step, s5_b_re, s5_b_im, s5_c_re, s5_c_im, s5_d,
           glu_w1, glu_b1, glu_w2, glu_b2, ln1_g, ln1_b, ln2_g, ln2_b, router_w, router_b, moe_w_gu, moe_b_gu,
           moe_w_down, moe_b_down):
    bp, sp, dm = x_prompt.shape
    bs, ss, _ = x_sample.shape
    depth = ln1_g.shape[0]
    alpha = (2 * depth) ** 0.25
    n_p, n_s = bp * sp, bs * ss
    n_tok = n_p + n_s
    n_rows = -(-n_tok // ROW_TILE) * ROW_TILE
    lru_w = conv_w.shape[2]
    q_w = w_out.shape[1] - lru_w
    kv_w = N_KV_HEADS * HEAD_DIM
    s5_groups = dm // S5_GROUP

    def unify(a_p, a_s):
        pad = jnp.zeros((n_rows - n_tok, a_p.shape[-1]), a_p.dtype)
        return jnp.concatenate([a_p.reshape(n_p, -1), a_s.reshape(n_s, -1), pad], axis=0)

    x = unify(x_prompt, x_sample)
    xb = x.astype(BF16)
    outs = {name: [] for name in ("k_p", "k_s", "v_p", "v_s", "i_p", "i_s", "c_p", "c_s", "h_p", "h_s",
                                   "r_p", "r_s", "m_p", "m_s")}
    for layer in range(depth):
        j = layer // 2
        if layer % 2 == 0:
            sizes = [q_w, kv_w, kv_w, IDX_Q_RANK, IDX_HEADS, IDX_DIM, lru_w, lru_w]
            cuts = [0]
            for s_ in sizes:
                cuts.append(cuts[-1] + s_)
            seg = lambda i: w_in[j][:, cuts[i]:cuts[i + 1]]
            lay = {"q": 0, "q_w": q_w, "u": q_w, "g": q_w + lru_w, "k": q_w + 2 * lru_w,
                   "v": q_w + 2 * lru_w + kv_w, "ki": q_w + 2 * lru_w + 2 * kv_w}
            lay["wi"] = lay["ki"] + IDX_DIM
            n_used = lay["wi"] + IDX_HEADS
            n_proj = -(-n_used // 512) * 512
            w_main = jnp.concatenate([seg(0), seg(6), seg(7), seg(1), seg(2), seg(5), seg(4),
                                      jnp.zeros((dm, n_proj - n_used), F32)], axis=1).astype(BF16)
            proj = _matmul(xb, w_main, tm=ROW_TILE, tn=512, name="in_proj")
            cq = _matmul(xb, seg(3).astype(BF16), tm=ROW_TILE, tn=IDX_Q_RANK // 2, out_dtype=BF16, name="cq_proj")
            q_idx = _matmul(cq, w_idx_qb[j].astype(BF16), tm=ROW_TILE, tn=1024, out_dtype=BF16, name="q_idx_proj")
            k_idx = _layer_norm_cols(proj, kidx_ln_g[j], kidx_ln_b[j], tr=ROW_TILE, col_block=lay["ki"] // IDX_DIM,
                                     width=IDX_DIM)
            lru_args = (conv_w[j], conv_b[j], lru_wa[j], lru_ba[j], lru_wx[j], lru_bx[j], lru_lambda[j])

            att_p = _dsa_prompt(proj, q_idx, k_idx, lay, bsz=bp, seq=sp, tq=128)
            rec_p, h_p = _rglru(proj, proj, jnp.zeros((bp, CONV_W - 1, lru_w), F32), jnp.zeros((bp, lru_w), F32),
                                *lru_args, bsz=bp, t_len=sp, t_valid=sp, u_col=lay["u"], g_col=lay["g"],
                                width=lru_w, wb=512)
            col = lambda a, name, width: a[:, lay[name]:lay[name] + width]
            proj_p = proj[:n_p]
            outs["k_p"].append(col(proj_p, "k", kv_w).reshape(bp, sp, N_KV_HEADS, HEAD_DIM))
            outs["v_p"].append(col(proj_p, "v", kv_w).reshape(bp, sp, N_KV_HEADS, HEAD_DIM))
            outs["i_p"].append(k_idx[:n_p].reshape(bp, sp, IDX_DIM))
            u_p = col(proj_p, "u", lru_w).reshape(bp, sp, lru_w)
            outs["c_p"].append(u_p[:, sp - (CONV_W - 1):])
            outs["h_p"].append(h_p)

            proj_s = proj[n_p:n_tok]
            s3 = lambda a: a.reshape(bs, ss, -1)
            q_s, k_s, v_s = s3(col(proj_s, "q", q_w)), s3(col(proj_s, "k", kv_w)), s3(col(proj_s, "v", kv_w))
            u_s, g_s = s3(col(proj_s, "u", lru_w)), s3(col(proj_s, "g", lru_w))
            wi_s = s3(col(proj_s, "wi", IDX_HEADS))
            kidx_s = s3(k_idx[n_p:n_tok])
            att_s = _dsa_sample(q_s, k_s, v_s, s3(q_idx[n_p:n_tok]), wi_s, kidx_s, cache_k, cache_v, cache_kidx,
                                page_table, j)
            t_pad = -(-ss // SUBLANES) * SUBLANES
            pad_t = lambda a: jnp.pad(a, ((0, 0), (0, t_pad - ss), (0, 0))).reshape(bs * t_pad, -1)
            rec_s, h_s = _rglru(pad_t(u_s), pad_t(g_s), state_conv[j], state_lru[j], *lru_args, bsz=bs, t_len=t_pad,
                                t_valid=ss, u_col=0, g_col=0, width=lru_w, wb=512)
            rec_s = rec_s.reshape(bs, t_pad, lru_w)[:, :ss]
            outs["k_s"].append(k_s.reshape(bs, ss, N_KV_HEADS, HEAD_DIM))
            outs["v_s"].append(v_s.reshape(bs, ss, N_KV_HEADS, HEAD_DIM))
            outs["i_s"].append(kidx_s)
            up_s = jnp.concatenate([state_conv[j], u_s], axis=1)
            outs["c_s"].append(up_s[:, ss:])
            outs["h_s"].append(h_s)

            att = unify(att_p, att_s.astype(BF16))
            rec = unify(rec_p, rec_s)
            mix = _matmul_2in(att, rec, w_out[j].astype(BF16), tm=ROW_TILE, tn=512, name="out_proj")
        else:
            ab_re, ab_im, cf_re, cf_im = _s5_discretize(s5_a_re[j], s5_a_im[j], s5_log_step[j])
            b_re, b_im = s5_b_re[j], s5_b_im[j]
            bb_re = cf_re[..., None] * b_re - cf_im[..., None] * b_im
            bb_im = cf_re[..., None] * b_im + cf_im[..., None] * b_re
            s5_args = (ab_re, ab_im, bb_re, bb_im, s5_c_re[j], s5_c_im[j], s5_d[j])
            n_state = ab_re.shape[1]
            zeros = jnp.zeros((bp, s5_groups, n_state), F32)
            z_p, hr, hi = _s5_scan(x[:n_p].reshape(bp, sp, dm), zeros, zeros, *s5_args)
            outs["r_p"].append(hr)
            outs["m_p"].append(hi)
            z_s, hr, hi = _s5_scan(x[n_p:n_tok].reshape(bs, ss, dm), state_s5_re[j], state_s5_im[j], *s5_args)
            outs["r_s"].append(hr)
            outs["m_s"].append(hi)
            mix = _glu(unify(z_p, z_s), glu_w1[j].astype(BF16), glu_b1[j], glu_w2[j].astype(BF16), glu_b2[j],
                       tm=ROW_TILE, tn=512)
        x, xb = _moe_block(x, mix, n_tok, layer, ln1_g[layer], ln1_b[layer], ln2_g[layer], ln2_b[layer],
                           router_w[layer], router_b[layer], moe_w_gu, moe_b_gu, moe_w_down, moe_b_down,
                           alpha=alpha)
    st = lambda name: jnp.stack(outs[name])
    return (x[:n_p].reshape(bp, sp, dm), x[n_p:n_tok].reshape(bs, ss, dm),
            st("k_p"), st("k_s"), st("v_p"), st("v_s"), st("i_p"), st("i_s"), st("c_p"), st("c_s"),
            st("h_p"), st("h_s"), st("r_p"), st("r_s"), st("m_p"), st("m_s"))
```

```python
import functools
import math

import jax
import jax.numpy as jnp
from jax import lax
from jax.experimental import pallas as pl
from jax.experimental.pallas import tpu as pltpu

F32 = jnp.float32
BF16 = jnp.bfloat16
I32 = jnp.int32

HEAD_DIM = 128
N_KV_HEADS = 4
IDX_HEADS = 32
IDX_DIM = 128
IDX_Q_RANK = 768
TOPK_MAX = 256
LRU_HEADS = 16
CONV_W = 4
LRU_C = 8.0
S5_GROUP = 16
TOP_K = 4
SWIGLU_LIMIT = 7.0
SWIGLU_ALPHA = 1.702
LN_EPS = 1e-5

LANES = 128
SUBLANES = 8
VMEM_LIMIT_BYTES = 56 * 1024 * 1024

NEG_INF = float("-inf")
INT_MIN = -(2 ** 31)


def _params(*sem):
    return pltpu.CompilerParams(dimension_semantics=sem, vmem_limit_bytes=VMEM_LIMIT_BYTES)


def _nt_dot(a, b):
    return lax.dot_general(a, b, (((1,), (1,)), ((), ())), preferred_element_type=F32)


def _dot(a, b):
    return jnp.dot(a, b, preferred_element_type=F32)


def _mm_kernel(x_ref, w_ref, o_ref):
    o_ref[...] = _dot(x_ref[...].astype(BF16), w_ref[...]).astype(o_ref.dtype)


def _matmul(x, w, *, tm, tn, out_dtype=F32, name="matmul"):
    m, k = x.shape
    n = w.shape[1]
    return pl.pallas_call(
        _mm_kernel,
        out_shape=jax.ShapeDtypeStruct((m, n), out_dtype),
        grid=(m // tm, n // tn),
        in_specs=[pl.BlockSpec((tm, k), lambda i, j: (i, 0)),
                  pl.BlockSpec((k, tn), lambda i, j: (0, j))],
        out_specs=pl.BlockSpec((tm, tn), lambda i, j: (i, j)),
        compiler_params=_params("parallel", "arbitrary"),
        name=name,
    )(x, w)


def _mm2_kernel(xa_ref, xb_ref, wa_ref, wb_ref, o_ref):
    o_ref[...] = _dot(xa_ref[...], wa_ref[...]) + _dot(xb_ref[...], wb_ref[...])


def _matmul_2in(xa, xb, w, *, tm, tn, name):
    m, ka = xa.shape
    n = w.shape[1]
    return pl.pallas_call(
        _mm2_kernel,
        out_shape=jax.ShapeDtypeStruct((m, n), F32),
        grid=(m // tm, n // tn),
        in_specs=[pl.BlockSpec((tm, ka), lambda i, j: (i, 0)),
                  pl.BlockSpec((tm, ka), lambda i, j: (i, 0)),
                  pl.BlockSpec((ka, tn), lambda i, j: (0, j)),
                  pl.BlockSpec((ka, tn), lambda i, j: (1, j))],
        out_specs=pl.BlockSpec((tm, tn), lambda i, j: (i, j)),
        compiler_params=_params("parallel", "arbitrary"),
        name=name,
    )(xa, xb, w, w)


def _glu_kernel(z_ref, w1_ref, b1_ref, w2_ref, b2_ref, o_ref):
    z = z_ref[...]
    a = _dot(z, w1_ref[...]) + b1_ref[...]
    b = _dot(z, w2_ref[...]) + b2_ref[...]
    o_ref[...] = a * jax.nn.sigmoid(b)


def _glu(z, w1, b1, w2, b2, *, tm, tn):
    m, k = z.shape
    n = w1.shape[1]
    x_spec = pl.BlockSpec((tm, k), lambda i, j: (i, 0))
    w_spec = pl.BlockSpec((k, tn), lambda i, j: (0, j))
    b_spec = pl.BlockSpec((1, tn), lambda i, j: (0, j))
    return pl.pallas_call(
        _glu_kernel,
        out_shape=jax.ShapeDtypeStruct((m, n), F32),
        grid=(m // tm, n // tn),
        in_specs=[x_spec, w_spec, b_spec, w_spec, b_spec],
        out_specs=pl.BlockSpec((tm, tn), lambda i, j: (i, j)),
        compiler_params=_params("parallel", "arbitrary"),
        name="glu",
    )(z, w1, b1.reshape(1, n), w2, b2.reshape(1, n))


def _ln_rows(x, g, b):
    mu = jnp.mean(x, axis=-1, keepdims=True)
    xc = x - mu
    var = jnp.mean(xc * xc, axis=-1, keepdims=True)
    return xc * lax.rsqrt(var + LN_EPS) * g + b


def _ln_kernel(x_ref, g_ref, b_ref, o_ref):
    o_ref[...] = _ln_rows(x_ref[...], g_ref[...], b_ref[...])


def _layer_norm_cols(x, g, b, *, tr, col_block, width):
    m = x.shape[0]
    return pl.pallas_call(
        _ln_kernel,
        out_shape=jax.ShapeDtypeStruct((m, width), F32),
        grid=(m // tr,),
        in_specs=[pl.BlockSpec((tr, width), lambda i: (i, col_block)),
                  pl.BlockSpec((1, width), lambda i: (0, 0)),
                  pl.BlockSpec((1, width), lambda i: (0, 0))],
        out_specs=pl.BlockSpec((tr, width), lambda i: (i, 0)),
        compiler_params=_params("parallel"),
        name="layer_norm",
    )(x, g.reshape(1, width), b.reshape(1, width))


def _top_gates(logits, n_experts):
    lane_f = lax.broadcasted_iota(I32, logits.shape, 1).astype(F32)
    l = jnp.where(lane_f < float(n_experts), logits, NEG_INF)
    vals, ids = [], []
    for _ in range(TOP_K):
        m = jnp.max(l, axis=-1, keepdims=True)
        idx = jnp.min(jnp.where(l == m, lane_f, float(LANES)), axis=-1, keepdims=True)
        vals.append(m)
        ids.append(idx)
        l = jnp.where(lane_f == idx, NEG_INF, l)
    es = [jnp.exp(v - vals[0]) for v in vals]
    den = es[0]
    for e in es[1:]:
        den = den + e
    return ids, [e / den for e in es]


def _ln_router_kernel(x_ref, mix_ref, g_ref, b_ref, rw_ref, rb_ref, tri_ref, o_ref, id_ref, gate_ref, rank_ref,
                      cnt_ref, *, alpha, n_experts, n_tokens):
    i = pl.program_id(0)
    tr = x_ref.shape[0]
    y = _ln_rows(alpha * x_ref[...] + mix_ref[...], g_ref[...], b_ref[...])
    o_ref[...] = y
    logits = _dot(y.astype(BF16), rw_ref[...]) + rb_ref[...]
    ids, gates = _top_gates(logits, n_experts)

    lane = lax.broadcasted_iota(I32, logits.shape, 1)
    lane_f = lane.astype(F32)
    valid = (i * tr + lax.broadcasted_iota(I32, logits.shape, 0)) < n_tokens
    chosen = jnp.zeros(logits.shape, F32)
    for r in range(TOP_K):
        chosen = chosen + jnp.where(lane_f == ids[r], 1.0, 0.0)
    chosen = jnp.where(valid, chosen, 0.0)

    @pl.when(i == 0)
    def _():
        cnt_ref[...] = jnp.zeros(cnt_ref.shape, F32)

    before = _dot(tri_ref[...], chosen.astype(BF16)) + cnt_ref[...]
    cnt_ref[...] = cnt_ref[...] + jnp.sum(chosen, axis=0, keepdims=True)

    idv = jnp.zeros(logits.shape, I32)
    gv = jnp.zeros(logits.shape, F32)
    rv = jnp.zeros(logits.shape, I32)
    for r in range(TOP_K):
        rank = jnp.sum(jnp.where(lane_f == ids[r], before, 0.0), axis=-1, keepdims=True)
        idv = jnp.where(lane == r, ids[r].astype(I32), idv)
        gv = jnp.where(lane == r, gates[r], gv)
        rv = jnp.where(lane == r, rank.astype(I32), rv)
    id_ref[...] = idv
    gate_ref[...] = gv
    rank_ref[...] = rv


def _ln_router(x, mix, g, b, rw, rb, *, alpha, tr, n_tokens):
    m, d = x.shape
    n_experts = rw.shape[1]
    rw_p = jnp.zeros((d, LANES), BF16).at[:, :n_experts].set(rw.astype(BF16))
    rb_p = jnp.zeros((1, LANES), F32).at[0, :n_experts].set(rb)
    tri = (jnp.arange(tr)[None, :] < jnp.arange(tr)[:, None]).astype(BF16)
    row = lambda i: (i, 0)
    fixed = lambda i: (0, 0)
    lanes_i = jax.ShapeDtypeStruct((m, LANES), I32)
    return pl.pallas_call(
        functools.partial(_ln_router_kernel, alpha=alpha, n_experts=n_experts, n_tokens=n_tokens),
        out_shape=(jax.ShapeDtypeStruct((m, d), F32), lanes_i, jax.ShapeDtypeStruct((m, LANES), F32), lanes_i,
                   jax.ShapeDtypeStruct((1, LANES), F32)),
        grid=(m // tr,),
        in_specs=[pl.BlockSpec((tr, d), row), pl.BlockSpec((tr, d), row),
                  pl.BlockSpec((1, d), fixed), pl.BlockSpec((1, d), fixed),
                  pl.BlockSpec((d, LANES), fixed), pl.BlockSpec((1, LANES), fixed),
                  pl.BlockSpec((tr, tr), fixed)],
        out_specs=(pl.BlockSpec((tr, d), row), pl.BlockSpec((tr, LANES), row), pl.BlockSpec((tr, LANES), row),
                   pl.BlockSpec((tr, LANES), row), pl.BlockSpec((1, LANES), fixed)),
        compiler_params=_params("arbitrary"),
        name="ln_router",
    )(x, mix, g.reshape(1, d), b.reshape(1, d), rw_p, rb_p, tri)


def _sortable_key(x):
    bits = pltpu.bitcast(x, I32)
    return jnp.where(bits < 0, bits ^ jnp.int32(0x7FFFFFFF), bits)


def _count(mask):
    return jnp.sum(jnp.where(mask, 1.0, 0.0), axis=-1, keepdims=True)


def _kth_largest_key(keys, k):
    def body(i, t):
        cand = t + lax.shift_left(jnp.int32(1), jnp.int32(31) - i)
        cnt = _count(keys[0] >= cand)
        for kk in keys[1:]:
            cnt = cnt + _count(kk >= cand)
        return jnp.where(cnt >= float(k), cand, t)

    t0 = jnp.full((keys[0].shape[0], 1), INT_MIN, I32)
    return lax.fori_loop(0, 32, body, t0)


def _strict_upper(n):
    return (jnp.arange(n)[:, None] < jnp.arange(n)[None, :]).astype(BF16)


KEY_CHUNK = 256
KEY_NEG_INF = -2139095041


def _dsa_prompt_kernel(q_ref, k_ref, v_ref, qi_ref, wi_ref, ki_ref, tri_ref, o_ref, bias_s, k_bf, v_bf, ki_bf,
                       *, topk, n_heads, w_scale):
    i = pl.program_id(1)
    tq = q_ref.shape[0]
    s_len = k_ref.shape[0]

    @pl.when(i == 0)
    def _():
        k_bf[...] = k_ref[...].astype(BF16)
        v_bf[...] = v_ref[...].astype(BF16)
        ki_bf[...] = ki_ref[...].astype(BF16)

    w = wi_ref[...][:, :IDX_HEADS] * w_scale
    qi_stack = jnp.concatenate([qi_ref[:, h * IDX_DIM:(h + 1) * IDX_DIM] for h in range(IDX_HEADS)], axis=0)
    pieces = []
    for c0 in range(0, s_len, KEY_CHUNK):
        d = _nt_dot(qi_stack, ki_bf[c0:c0 + KEY_CHUNK, :])
        sc_c = jnp.zeros((tq, KEY_CHUNK), F32)
        for h in range(IDX_HEADS):
            sc_c = sc_c + jnp.maximum(d[h * tq:(h + 1) * tq, :], 0.0) * w[:, h:h + 1]
        pieces.append(sc_c)
    sc = jnp.concatenate(pieces, axis=1)
    row = i * tq + lax.broadcasted_iota(I32, (tq, s_len), 0)
    col = lax.broadcasted_iota(I32, (tq, s_len), 1)
    causal = col <= row
    key = _sortable_key(jnp.where(causal, sc, NEG_INF))
    thr = _kth_largest_key([key], topk)
    gt = key > thr
    eq = key == thr
    need = float(topk) - _count(gt)
    bias_s[...] = jnp.where((key >= thr) & causal, 0.0, NEG_INF)
    tied = (thr > KEY_NEG_INF) & (_count(eq) > need)

    @pl.when(jnp.max(jnp.where(tied, 1.0, 0.0)) > 0.0)
    def _():
        rank = _dot(jnp.where(eq, 1.0, 0.0).astype(BF16), tri_ref[...])
        sel = (gt | (eq & (rank < need))) & causal
        bias_s[...] = jnp.where(sel, 0.0, NEG_INF)

    group = n_heads // N_KV_HEADS
    bias = jnp.concatenate([bias_s[...]] * group, axis=0)
    for kv in range(N_KV_HEADS):
        kh = k_bf[:, kv * HEAD_DIM:(kv + 1) * HEAD_DIM]
        vh = v_bf[:, kv * HEAD_DIM:(kv + 1) * HEAD_DIM]
        heads = range(kv * group, (kv + 1) * group)
        qs = jnp.concatenate([q_ref[:, hh * HEAD_DIM:(hh + 1) * HEAD_DIM] for hh in heads], axis=0).astype(BF16)
        s = _nt_dot(qs, kh) * (HEAD_DIM ** -0.5) + bias
        m = jnp.max(s, axis=-1, keepdims=True)
        p = jnp.exp(s - m)
        l = jnp.sum(p, axis=-1, keepdims=True)
        o = _dot(p.astype(BF16), vh) / l
        for g, hh in enumerate(heads):
            o_ref[:, hh * HEAD_DIM:(hh + 1) * HEAD_DIM] = o[g * tq:(g + 1) * tq, :].astype(o_ref.dtype)


def _dsa_prompt(proj, q_idx, k_idx, lay, *, bsz, seq, tq):
    topk = min(TOPK_MAX, seq // 4)
    q_w = lay["q_w"]
    kv_w = N_KV_HEADS * HEAD_DIM
    nq = seq // tq
    return pl.pallas_call(
        functools.partial(_dsa_prompt_kernel, topk=topk, n_heads=q_w // HEAD_DIM,
                          w_scale=(IDX_HEADS * IDX_DIM) ** -0.5),
        out_shape=jax.ShapeDtypeStruct((bsz * seq, q_w), BF16),
        grid=(bsz, nq),
        in_specs=[pl.BlockSpec((tq, q_w), lambda b, i: (b * nq + i, lay["q"] // q_w)),
                  pl.BlockSpec((seq, kv_w), lambda b, i: (b, lay["k"] // kv_w)),
                  pl.BlockSpec((seq, kv_w), lambda b, i: (b, lay["v"] // kv_w)),
                  pl.BlockSpec((tq, IDX_HEADS * IDX_DIM), lambda b, i: (b * nq + i, 0)),
                  pl.BlockSpec((tq, LANES), lambda b, i: (b * nq + i, lay["wi"] // LANES)),
                  pl.BlockSpec((seq, IDX_DIM), lambda b, i: (b, 0)),
                  pl.BlockSpec((seq, seq), lambda b, i: (0, 0))],
        out_specs=pl.BlockSpec((tq, q_w), lambda b, i: (b * nq + i, 0)),
        scratch_shapes=[pltpu.VMEM((tq, seq), F32), pltpu.VMEM((seq, kv_w), BF16), pltpu.VMEM((seq, kv_w), BF16),
                        pltpu.VMEM((seq, IDX_DIM), BF16)],
        compiler_params=_params("parallel", "arbitrary"),
        name="dsa_prompt",
    )(proj, proj, proj, q_idx, proj, k_idx, _strict_upper(seq))


PAGES_PER_STEP = 8


def _page_scores(qi_rows, w_rows, page_bf, t_new):
    d = jnp.maximum(_nt_dot(qi_rows, page_bf), 0.0) * w_rows
    return jnp.concatenate(
        [jnp.sum(d[t * IDX_HEADS:(t + 1) * IDX_HEADS], axis=0, keepdims=True) for t in range(t_new)], axis=0)


def _dsa_sample_scores_kernel(pt_ref, qi_ref, w_ref, *rest, t_new, page):
    pages = rest[:PAGES_PER_STEP]
    knew_ref, sc_ref, scn_ref = rest[PAGES_PER_STEP:]
    p = pl.program_id(1)
    qi = qi_ref[0]
    w = w_ref[0]
    for g in range(PAGES_PER_STEP):
        sc_ref[0, :, g * page:(g + 1) * page] = _page_scores(qi, w, pages[g][0, 0].astype(BF16), t_new)

    @pl.when(p == pl.num_programs(1) - 1)
    def _():
        sc = _page_scores(qi, w, knew_ref[0].astype(BF16), t_new)
        row = lax.broadcasted_iota(I32, sc.shape, 0)
        col = lax.broadcasted_iota(I32, sc.shape, 1)
        scn_ref[0] = jnp.where(col <= row, sc, NEG_INF)


def _dsa_sample_attn_kernel(pt_ref, sc_ref, scn_ref, q_ref, *rest, t_new, page, topk, n_heads):
    kp = rest[:PAGES_PER_STEP]
    vp = rest[PAGES_PER_STEP:2 * PAGES_PER_STEP]
    knew_ref, vnew_ref, tri_ref, o_ref, thr_s, need_s, cnt_s, m_s, l_s, acc_s, bias_s, flag_s = rest[2 * PAGES_PER_STEP:]
    p = pl.program_id(1)
    chunk = PAGES_PER_STEP * page
    rows = t_new * n_heads

    @pl.when(p == 0)
    def _():
        key_p = _sortable_key(sc_ref[0])
        key_n = _sortable_key(scn_ref[0])
        thr = _kth_largest_key([key_p, key_n], topk)
        n_gt = _count(key_p > thr) + _count(key_n > thr)
        n_eq = _count(key_p == thr) + _count(key_n == thr)
        need = float(topk) - n_gt
        thr_s[...] = jnp.broadcast_to(thr, thr_s.shape)
        need_s[...] = jnp.broadcast_to(need, need_s.shape)
        cnt_s[...] = jnp.zeros(cnt_s.shape, F32)
        flag_s[0] = (jnp.max(n_eq - need) > 0.0).astype(I32)
        m_s[...] = jnp.full(m_s.shape, NEG_INF, F32)
        l_s[...] = jnp.zeros(l_s.shape, F32)
        acc_s[...] = jnp.zeros(acc_s.shape, F32)

    thr = thr_s[:, 0:1]
    q = [q_ref[0, kv] for kv in range(N_KV_HEADS)]

    def select(sc, width):
        key = _sortable_key(sc)
        bias_s[:, :width] = jnp.where(key >= thr, 0.0, NEG_INF)

        @pl.when(flag_s[0] != 0)
        def _():
            eq = key == thr
            eqf = jnp.where(eq, 1.0, 0.0)
            rank = _dot(eqf.astype(BF16), tri_ref[:width, :width]) + cnt_s[:, 0:1]
            sel = (key > thr) | (eq & (rank < need_s[:, 0:1]))
            bias_s[:, :width] = jnp.where(sel, 0.0, NEG_INF)
            cnt_s[...] = cnt_s[...] + jnp.sum(eqf, axis=-1, keepdims=True)

    def attend(k_pages, v_pages, bias_t):
        bias = jnp.concatenate([jnp.broadcast_to(bias_t[t:t + 1], (n_heads, bias_t.shape[1])) for t in range(t_new)],
                               axis=0)
        s_pages = []
        for k_kv in k_pages:
            s = _nt_dot(q[0], k_kv[0])
            for kv in range(1, N_KV_HEADS):
                s = s + _nt_dot(q[kv], k_kv[kv])
            s_pages.append(s)
        s = jnp.concatenate(s_pages, axis=1) * (HEAD_DIM ** -0.5) + bias
        m_old = m_s[:, 0:1]
        m_new = jnp.maximum(m_old, jnp.max(s, axis=-1, keepdims=True))
        m_safe = jnp.where(m_new == NEG_INF, 0.0, m_new)
        alpha = jnp.exp(m_old - m_safe)
        pr = jnp.exp(s - m_safe)
        l_s[...] = alpha * l_s[...] + jnp.sum(pr, axis=-1, keepdims=True)
        pb = pr.astype(BF16)
        for kv in range(N_KV_HEADS):
            cols = slice(kv * HEAD_DIM, (kv + 1) * HEAD_DIM)
            pv = _dot(pb[:, 0:page], v_pages[0][kv])
            for g in range(1, len(v_pages)):
                pv = pv + _dot(pb[:, g * page:(g + 1) * page], v_pages[g][kv])
            acc_s[:, cols] = alpha * acc_s[:, cols] + pv
        m_s[...] = jnp.broadcast_to(m_new, m_s.shape)

    def page_heads(ref):
        return [ref[0, 0, :, kv, :].astype(BF16) for kv in range(N_KV_HEADS)]

    def row_heads(ref):
        return [ref[0, :, kv * HEAD_DIM:(kv + 1) * HEAD_DIM].astype(BF16) for kv in range(N_KV_HEADS)]

    start = pl.multiple_of(p * chunk, chunk)
    select(sc_ref[0, :, pl.ds(start, chunk)], chunk)
    attend([page_heads(r) for r in kp], [page_heads(r) for r in vp], bias_s[...])

    @pl.when(p == pl.num_programs(1) - 1)
    def _():
        select(scn_ref[0], page)
        attend([row_heads(knew_ref)], [row_heads(vnew_ref)], bias_s[:, :page])
        o_ref[0] = acc_s[...] / l_s[:, 0:1]


def _dsa_sample(q, k, v, q_idx, w_idx, k_idx, cache_k, cache_v, cache_kidx, page_table, layer):
    bsz, t_new, q_w = q.shape
    n_heads = q_w // HEAD_DIM
    kv_w = N_KV_HEADS * HEAD_DIM
    page = cache_k.shape[2]
    n_pages = page_table.shape[1]
    past = n_pages * page
    topk = min(TOPK_MAX, (past + t_new) // 4)
    steps = n_pages // PAGES_PER_STEP
    chunk = PAGES_PER_STEP * page
    rows = t_new * n_heads

    qi_rows = q_idx.reshape(bsz, t_new * IDX_HEADS, IDX_DIM).astype(BF16)
    w_rows = (w_idx * (IDX_HEADS * IDX_DIM) ** -0.5).reshape(bsz, t_new * IDX_HEADS, 1)
    pad_rows = lambda a: jnp.pad(a, ((0, 0), (0, page - t_new), (0, 0)))
    kidx_new, k_new, v_new = pad_rows(k_idx), pad_rows(k), pad_rows(v)

    def page_spec(trailing, g):
        zeros = (0,) * (1 + len(trailing))
        return pl.BlockSpec((1, 1, page) + trailing,
                            lambda b, p, pt: (layer, pt[b, p * PAGES_PER_STEP + g]) + zeros)

    per_b = lambda b, p, pt: (b, 0, 0)
    sc_past, sc_new = pl.pallas_call(
        functools.partial(_dsa_sample_scores_kernel, t_new=t_new, page=page),
        out_shape=(jax.ShapeDtypeStruct((bsz, t_new, past), F32),
                   jax.ShapeDtypeStruct((bsz, t_new, page), F32)),
        grid_spec=pltpu.PrefetchScalarGridSpec(
            num_scalar_prefetch=1,
            grid=(bsz, steps),
            in_specs=[pl.BlockSpec((1, t_new * IDX_HEADS, IDX_DIM), per_b),
                      pl.BlockSpec((1, t_new * IDX_HEADS, 1), per_b)]
                     + [page_spec((IDX_DIM,), g) for g in range(PAGES_PER_STEP)]
                     + [pl.BlockSpec((1, page, IDX_DIM), per_b)],
            out_specs=(pl.BlockSpec((1, t_new, chunk), lambda b, p, pt: (b, 0, p)),
                       pl.BlockSpec((1, t_new, page), per_b))),
        compiler_params=_params("parallel", "arbitrary"),
        name="dsa_sample_scores",
    )(page_table, qi_rows, w_rows, *([cache_kidx] * PAGES_PER_STEP), kidx_new)

    group = n_heads // N_KV_HEADS
    head_kv = jnp.arange(n_heads) // group
    blockdiag = (head_kv[:, None] == jnp.arange(N_KV_HEADS)[None, :]).astype(F32)
    q4 = q.reshape(bsz, t_new, n_heads, 1, HEAD_DIM) * blockdiag[None, None, :, :, None]
    q_bd = q4.transpose(0, 3, 1, 2, 4).reshape(bsz, N_KV_HEADS, rows, HEAD_DIM).astype(BF16)

    out = pl.pallas_call(
        functools.partial(_dsa_sample_attn_kernel, t_new=t_new, page=page, topk=topk, n_heads=n_heads),
        out_shape=jax.ShapeDtypeStruct((bsz, rows, kv_w), F32),
        grid_spec=pltpu.PrefetchScalarGridSpec(
            num_scalar_prefetch=1,
            grid=(bsz, steps),
            in_specs=[pl.BlockSpec((1, t_new, past), per_b),
                      pl.BlockSpec((1, t_new, page), per_b),
                      pl.BlockSpec((1, N_KV_HEADS, rows, HEAD_DIM), lambda b, p, pt: (b, 0, 0, 0))]
                     + [page_spec((N_KV_HEADS, HEAD_DIM), g) for g in range(PAGES_PER_STEP)]
                     + [page_spec((N_KV_HEADS, HEAD_DIM), g) for g in range(PAGES_PER_STEP)]
                     + [pl.BlockSpec((1, page, kv_w), per_b), pl.BlockSpec((1, page, kv_w), per_b),
                        pl.BlockSpec((chunk, chunk), lambda b, p, pt: (0, 0))],
            out_specs=pl.BlockSpec((1, rows, kv_w), per_b),
            scratch_shapes=[pltpu.VMEM((t_new, LANES), I32), pltpu.VMEM((t_new, LANES), F32),
                            pltpu.VMEM((t_new, LANES), F32), pltpu.VMEM((rows, LANES), F32),
                            pltpu.VMEM((rows, LANES), F32), pltpu.VMEM((rows, kv_w), F32),
                            pltpu.VMEM((t_new, chunk), F32), pltpu.SMEM((1,), I32)]),
        compiler_params=_params("parallel", "arbitrary"),
        name="dsa_sample_attn",
    )(page_table, sc_past, sc_new, q_bd, *([cache_k] * PAGES_PER_STEP), *([cache_v] * PAGES_PER_STEP),
      k_new, v_new, _strict_upper(chunk))
    o5 = out.reshape(bsz, t_new, N_KV_HEADS, group, N_KV_HEADS, HEAD_DIM)
    o = jnp.stack([o5[:, :, kv, :, kv, :] for kv in range(N_KV_HEADS)], axis=2)
    return o.reshape(bsz, t_new, q_w)


def _softplus(x):
    return jnp.maximum(x, 0.0) + jnp.log1p(jnp.exp(-jnp.abs(x)))


def _rglru_kernel(u_ref, g_ref, cp_ref, h0_ref, cw_ref, cb_ref, wa_ref, ba_ref, wx_ref, bx_ref, lam_ref,
                  y_ref, hl_ref, up_s, a_s, b_s, *, t_valid, t_chunk):
    t_len, wb = u_ref.shape
    lead = SUBLANES - (CONV_W - 1)
    up_s[0:SUBLANES, :] = jnp.zeros((SUBLANES, wb), F32)
    up_s[lead:SUBLANES, :] = cp_ref[0]
    up_s[SUBLANES:SUBLANES + t_len, :] = u_ref[...]
    neg_c_sp = -LRU_C * _softplus(-lam_ref[...])
    for c0 in range(0, t_len, t_chunk):
        xc = cw_ref[0:1, :] * up_s[lead + c0:lead + c0 + t_chunk, :]
        for j in range(1, CONV_W):
            xc = xc + cw_ref[j:j + 1, :] * up_s[lead + j + c0:lead + j + c0 + t_chunk, :]
        xc = cb_ref[...] + xc
        ra, ix = [], []
        for hd in range(wb // LANES):
            xh = xc[:, hd * LANES:(hd + 1) * LANES].astype(BF16)
            ra.append(_dot(xh, wa_ref[hd]))
            ix.append(_dot(xh, wx_ref[hd]))
        r = jax.nn.sigmoid(jnp.concatenate(ra, axis=-1) + ba_ref[...])
        gi = jax.nn.sigmoid(jnp.concatenate(ix, axis=-1) + bx_ref[...])
        log_a = r * neg_c_sp
        a = jnp.exp(log_a)
        a_s[c0:c0 + t_chunk, :] = a
        b_s[c0:c0 + t_chunk, :] = jnp.sqrt(-jnp.tanh(log_a) * (a * a + 1.0)) * (gi * xc)

    def step(t, h):
        h = a_s[pl.ds(t, 1), :] * h + b_s[pl.ds(t, 1), :]
        b_s[pl.ds(t, 1), :] = h
        return h

    lax.fori_loop(0, t_len, step, h0_ref[0], unroll=8)
    hl_ref[0] = b_s[t_valid - 1:t_valid, :]
    for c0 in range(0, t_len, t_chunk):
        rows = slice(c0, c0 + t_chunk)
        y_ref[rows, :] = (b_s[rows, :] * jax.nn.gelu(g_ref[rows, :])).astype(y_ref.dtype)


def _rglru(u_src, g_src, conv_prev, h0, conv_w, conv_b, wa, ba, wx, bx, lam, *, bsz, t_len, t_valid,
           u_col, g_col, width, wb):
    nb = width // wb
    heads_per = wb // LANES
    vec = lambda a: a.reshape(1, width)
    vspec = pl.BlockSpec((1, wb), lambda b, c: (0, c))
    t_chunk = min(t_len, 256)
    y, h_last = pl.pallas_call(
        functools.partial(_rglru_kernel, t_valid=t_valid, t_chunk=t_chunk),
        out_shape=(jax.ShapeDtypeStruct((bsz * t_len, width), BF16),
                   jax.ShapeDtypeStruct((bsz, 1, width), F32)),
        grid=(bsz, nb),
        in_specs=[pl.BlockSpec((t_len, wb), lambda b, c: (b, u_col // wb + c)),
                  pl.BlockSpec((t_len, wb), lambda b, c: (b, g_col // wb + c)),
                  pl.BlockSpec((1, CONV_W - 1, wb), lambda b, c: (b, 0, c)),
                  pl.BlockSpec((1, 1, wb), lambda b, c: (b, 0, c)),
                  pl.BlockSpec((CONV_W, wb), lambda b, c: (0, c)),
                  vspec,
                  pl.BlockSpec((heads_per, LANES, LANES), lambda b, c: (c, 0, 0)),
                  vspec,
                  pl.BlockSpec((heads_per, LANES, LANES), lambda b, c: (c, 0, 0)),
                  vspec, vspec],
        out_specs=(pl.BlockSpec((t_len, wb), lambda b, c: (b, c)),
                   pl.BlockSpec((1, 1, wb), lambda b, c: (b, 0, c))),
        scratch_shapes=[pltpu.VMEM((t_len + SUBLANES, wb), F32), pltpu.VMEM((t_len, wb), F32),
                        pltpu.VMEM((t_len, wb), F32)],
        compiler_params=_params("parallel", "arbitrary"),
        name="rglru",
    )(u_src, g_src, conv_prev, h0.reshape(bsz, 1, width), conv_w, vec(conv_b), wa.astype(BF16), vec(ba),
      wx.astype(BF16), vec(bx), vec(lam))
    return y, h_last.reshape(bsz, width)


def _s5_disc_kernel(are_ref, aim_ref, ls_ref, abr_ref, abi_ref, cfr_ref, cfi_ref):
    a_re, a_im = are_ref[...], aim_ref[...]
    dt = jnp.exp(ls_ref[...])
    mag = jnp.exp(a_re * dt)
    ab_re = mag * jnp.cos(a_im * dt)
    ab_im = mag * jnp.sin(a_im * dt)
    den = a_re * a_re + a_im * a_im
    nr, ni = ab_re - 1.0, ab_im
    abr_ref[...] = ab_re
    abi_ref[...] = ab_im
    cfr_ref[...] = (nr * a_re + ni * a_im) / den
    cfi_ref[...] = (ni * a_re - nr * a_im) / den


def _s5_discretize(a_re, a_im, log_step):
    g, n = a_re.shape
    shp = jax.ShapeDtypeStruct((g, n), F32)
    return pl.pallas_call(_s5_disc_kernel, out_shape=(shp, shp, shp, shp), name="s5_discretize")(
        a_re, a_im, jnp.broadcast_to(log_step[:, None], (g, n)))


def _s5_kernel(u_ref, wb_ref, wc_ref, are_ref, aim_ref, d_ref, h0r_ref, h0i_ref, z_ref, hr_ref, hi_ref, x_s, st_s,
               *, nj, t_chunk):
    c = pl.program_id(1)
    ns = are_ref.shape[2]

    @pl.when(c == 0)
    def _():
        st_s[0] = h0r_ref[0]
        st_s[1] = h0i_ref[0]

    u = u_ref[0]
    rowj = (lax.broadcasted_iota(I32, u.shape, 0) % SUBLANES) // (SUBLANES // nj)
    u_bf = u.astype(BF16)
    if nj > 1:
        zero = jnp.zeros(u_bf.shape, BF16)
        u_bf = jnp.concatenate([jnp.where(rowj == j, u_bf, zero) for j in range(nj)], axis=1)
    x_s[...] = _dot(u_bf, wb_ref[0])
    ar, ai = are_ref[0], aim_ref[0]

    def step(t, carry):
        xr, xi = carry
        r0 = pl.multiple_of(t * SUBLANES, SUBLANES)
        nr = ar * xr - ai * xi + x_s[pl.ds(r0, SUBLANES), 0:ns]
        ni = ar * xi + ai * xr + x_s[pl.ds(r0, SUBLANES), ns:2 * ns]
        x_s[pl.ds(r0, SUBLANES), 0:ns] = nr
        x_s[pl.ds(r0, SUBLANES), ns:2 * ns] = ni
        return nr, ni

    xr, xi = lax.fori_loop(0, t_chunk, step, (st_s[0], st_s[1]), unroll=min(t_chunk, 4))
    st_s[0] = xr
    st_s[1] = xi
    yb = _dot(x_s[...].astype(BF16), wc_ref[0])
    y = yb[:, 0:LANES]
    dvec = jnp.broadcast_to(d_ref[0, 0:1, :], y.shape)
    for j in range(1, nj):
        y = jnp.where(rowj == j, yb[:, j * LANES:(j + 1) * LANES], y)
        dvec = jnp.where(rowj == j, d_ref[0, j:j + 1, :], dvec)
    z_ref[0] = jax.nn.gelu(y + dvec * u).astype(z_ref.dtype)

    @pl.when(c == pl.num_programs(1) - 1)
    def _():
        hr_ref[0] = xr
        hi_ref[0] = xi


def _s5_scan(u, h0_re, h0_im, ab_re, ab_im, bb_re, bb_im, c_re, c_im, d_skip):
    bsz, t_len, d = u.shape
    groups, n_state = ab_re.shape
    gl = LANES // S5_GROUP
    ns = gl * n_state
    nj = SUBLANES // bsz
    nq = d // (LANES * nj)
    t_chunk = min(t_len, 256)
    eye_g = jnp.eye(gl, dtype=F32)

    u_rows = u.reshape(bsz, t_len, nq, nj, LANES).transpose(2, 1, 3, 0, 4).reshape(nq, t_len * SUBLANES, LANES)

    def b_weight(bb):
        w = bb.reshape(nq, nj, gl, n_state, S5_GROUP).transpose(0, 1, 2, 4, 3)
        w = w[:, :, :, :, None, :] * eye_g[None, None, :, None, :, None]
        return w.reshape(nq, nj * LANES, ns)

    w_b = jnp.concatenate([b_weight(bb_re), b_weight(bb_im)], axis=-1).astype(BF16)

    def c_weight(cc):
        w = cc.reshape(nq, nj, gl, S5_GROUP, n_state).transpose(0, 2, 4, 1, 3)
        w = w[:, :, :, :, None, :] * eye_g[None, :, None, None, :, None]
        return w.reshape(nq, ns, nj * LANES)

    w_c = jnp.concatenate([c_weight(c_re), -c_weight(c_im)], axis=1).astype(BF16)

    def coef(a):
        a = a.reshape(nq, nj, 1, ns)
        return jnp.broadcast_to(a, (nq, nj, bsz, ns)).reshape(nq, SUBLANES, ns)

    def state_in(h):
        return h.reshape(bsz, nq, nj, ns).transpose(1, 2, 0, 3).reshape(nq, SUBLANES, ns)

    def state_out(h):
        return h.reshape(nq, nj, bsz, ns).transpose(2, 0, 1, 3).reshape(bsz, groups, n_state)

    rows = t_chunk * SUBLANES
    per_q = lambda q, c: (q, 0, 0)
    st_spec = pl.BlockSpec((1, SUBLANES, ns), per_q)
    st_shape = jax.ShapeDtypeStruct((nq, SUBLANES, ns), F32)
    z, hr, hi = pl.pallas_call(
        functools.partial(_s5_kernel, nj=nj, t_chunk=t_chunk),
        out_shape=(jax.ShapeDtypeStruct((nq, t_len * SUBLANES, LANES), BF16), st_shape, st_shape),
        grid=(nq, t_len // t_chunk),
        in_specs=[pl.BlockSpec((1, rows, LANES), lambda q, c: (q, c, 0)),
                  pl.BlockSpec((1, nj * LANES, 2 * ns), per_q),
                  pl.BlockSpec((1, 2 * ns, nj * LANES), per_q),
                  st_spec, st_spec,
                  pl.BlockSpec((1, nj, LANES), per_q),
                  st_spec, st_spec],
        out_specs=(pl.BlockSpec((1, rows, LANES), lambda q, c: (q, c, 0)), st_spec, st_spec),
        scratch_shapes=[pltpu.VMEM((rows, 2 * ns), F32), pltpu.VMEM((2, SUBLANES, ns), F32)],
        compiler_params=_params("parallel", "arbitrary"),
        name="s5_scan",
    )(u_rows, w_b, w_c, coef(ab_re), coef(ab_im), d_skip.reshape(nq, nj, LANES), state_in(h0_re), state_in(h0_im))
    z = z.reshape(nq, t_len, nj, bsz, LANES).transpose(3, 1, 0, 2, 4).reshape(bsz, t_len, d)
    return z, state_out(hr), state_out(hi)


MOE_TM = 256
COMBINE_ROWS = 32
DISPATCH_ROWS = 128


def _moe_plan(ids, ranks, counts, n_tokens, n_experts):
    rows = ids.shape[0]
    n_tiles = (n_tokens * TOP_K) // MOE_TM + n_experts
    spare_tiles = -(-(rows - n_tokens) * TOP_K // MOE_TM)
    tiles_per = (counts + MOE_TM - 1) // MOE_TM
    tile_end = jnp.cumsum(tiles_per)
    run_start = (tile_end - tiles_per) * MOE_TM
    chose = ids[..., None] == jnp.arange(n_experts, dtype=I32)
    dest = jnp.sum(jnp.where(chose, run_start, 0), axis=-1) + ranks
    r = jnp.arange(rows, dtype=I32)[:, None]
    spare = n_tiles * MOE_TM + (r - n_tokens) * TOP_K + jnp.arange(TOP_K, dtype=I32)[None, :]
    is_token = r < n_tokens
    dest_dispatch = jnp.where(is_token, dest, spare).reshape(-1)
    dest_combine = jnp.where(is_token, dest, 0).reshape(-1)
    n_valid = tile_end[-1]
    tile_ids = jnp.arange(n_tiles + spare_tiles, dtype=I32)
    tile_expert = jnp.sum(tile_end[None, :] <= jnp.minimum(tile_ids[:n_tiles], n_valid - 1)[:, None], axis=1)
    is_last = jnp.any((tiles_per > 0)[None, :] & (tile_end[None, :] - 1 == tile_ids[:, None]), axis=1)
    zero_fill = (is_last | (tile_ids >= n_valid)).astype(I32)
    return (dest_dispatch, dest_combine, tile_expert.astype(I32), n_valid.astype(I32).reshape(1), zero_fill,
            n_tiles, n_tiles + spare_tiles)


def _dispatch_kernel(dest_ref, zf_ref, x_ref, o_hbm, zbuf, sem, zsem, *, total_tiles):
    t = pl.program_id(0)
    tr = x_ref.shape[0]

    def zero_copy(tile):
        start = pl.multiple_of(tile * MOE_TM, MOE_TM)
        return pltpu.make_async_copy(zbuf, o_hbm.at[pl.ds(start, MOE_TM), :], zsem)

    @pl.when(t == 0)
    def _():
        zbuf[...] = jnp.zeros(zbuf.shape, zbuf.dtype)

        def start(tile, c):
            @pl.when(zf_ref[tile] != 0)
            def _():
                zero_copy(tile).start()
            return c

        def wait(tile, c):
            @pl.when(zf_ref[tile] != 0)
            def _():
                zero_copy(tile).wait()
            return c

        lax.fori_loop(0, total_tiles, start, 0)
        lax.fori_loop(0, total_tiles, wait, 0)

    def body(r, c):
        for k in range(TOP_K):
            row = dest_ref[(t * tr + r) * TOP_K + k]
            pltpu.make_async_copy(x_ref.at[pl.ds(r, 1), :], o_hbm.at[pl.ds(row, 1), :], sem).start()
        return c

    lax.fori_loop(0, tr, body, 0)
    for k in range(TOP_K):
        pltpu.make_async_copy(x_ref, o_hbm.at[pl.ds(0, tr), :], sem).wait()


def _dispatch(x, dest, zero_fill, total_tiles):
    m, d = x.shape
    return pl.pallas_call(
        functools.partial(_dispatch_kernel, total_tiles=total_tiles),
        out_shape=jax.ShapeDtypeStruct((total_tiles * MOE_TM, d), x.dtype),
        grid_spec=pltpu.PrefetchScalarGridSpec(
            num_scalar_prefetch=2,
            grid=(m // DISPATCH_ROWS,),
            in_specs=[pl.BlockSpec((DISPATCH_ROWS, d), lambda t, dst, zf: (t, 0))],
            out_specs=pl.BlockSpec(memory_space=pl.ANY),
            scratch_shapes=[pltpu.VMEM((MOE_TM, d), x.dtype), pltpu.SemaphoreType.DMA(()),
                            pltpu.SemaphoreType.DMA(())]),
        compiler_params=_params("arbitrary"),
        name="moe_dispatch",
    )(dest, zero_fill, x)


def _new_expert(te_ref, t):
    return (t == 0) | (te_ref[t] != te_ref[jnp.maximum(t - 1, 0)])


def _moe_gate_up_kernel(te_ref, nv_ref, x_ref, wg_ref, wu_ref, bg_ref, bu_ref, a_ref, wg_bf, wu_bf, *, k_chunk):
    t = pl.program_id(1)

    @pl.when(t < nv_ref[0])
    def _():
        d = x_ref.shape[1]

        @pl.when(_new_expert(te_ref, t))
        def _():
            for k0 in range(0, d, k_chunk):
                wg_bf[k0:k0 + k_chunk, :] = wg_ref[0, 0, k0:k0 + k_chunk, :].astype(BF16)
                wu_bf[k0:k0 + k_chunk, :] = wu_ref[0, 0, k0:k0 + k_chunk, :].astype(BF16)

        hg = jnp.zeros(a_ref.shape, F32)
        hu = jnp.zeros(a_ref.shape, F32)
        for k0 in range(0, d, k_chunk):
            xk = x_ref[:, k0:k0 + k_chunk].astype(BF16)
            hg = hg + _dot(xk, wg_bf[k0:k0 + k_chunk, :])
            hu = hu + _dot(xk, wu_bf[k0:k0 + k_chunk, :])
        gate = jnp.minimum(hg + bg_ref[0, 0], SWIGLU_LIMIT)
        up = jnp.clip(hu + bu_ref[0, 0], -SWIGLU_LIMIT, SWIGLU_LIMIT)
        a_ref[...] = ((up + 1.0) * gate * jax.nn.sigmoid(SWIGLU_ALPHA * gate)).astype(a_ref.dtype)

    @pl.when(t >= nv_ref[0])
    def _():
        a_ref[...] = jnp.zeros(a_ref.shape, a_ref.dtype)


def _moe_down_kernel(te_ref, nv_ref, a_ref, w_ref, b_ref, y_ref, w_bf, *, k_chunk):
    t = pl.program_id(1)

    @pl.when(t < nv_ref[0])
    def _():
        f = a_ref.shape[1]

        @pl.when(_new_expert(te_ref, t))
        def _():
            for k0 in range(0, f, k_chunk):
                w_bf[k0:k0 + k_chunk, :] = w_ref[0, 0, k0:k0 + k_chunk, :].astype(BF16)

        y_ref[...] = _dot(a_ref[...], w_bf[...]) + b_ref[0, 0]

    @pl.when(t >= nv_ref[0])
    def _():
        y_ref[...] = jnp.zeros(y_ref.shape, y_ref.dtype)


def _moe_experts(xs, tile_expert, n_valid, w_gu, b_gu, w_dn, b_dn, layer, n_tiles):
    _, n_experts, d, two_f = w_gu.shape
    f = two_f // 2
    tn1 = min(512, f)
    nb1 = f // tn1
    n_slots = n_tiles * MOE_TM
    b_gu4 = b_gu.reshape(b_gu.shape[0], n_experts, 1, two_f)
    last = lambda t, nv: jnp.minimum(t, nv[0] - 1)
    act = pl.pallas_call(
        functools.partial(_moe_gate_up_kernel, k_chunk=min(1024, d)),
        out_shape=jax.ShapeDtypeStruct((n_slots, f), BF16),
        grid_spec=pltpu.PrefetchScalarGridSpec(
            num_scalar_prefetch=2,
            grid=(nb1, n_tiles),
            in_specs=[pl.BlockSpec((MOE_TM, d), lambda j, t, te, nv: (last(t, nv), 0)),
                      pl.BlockSpec((1, 1, d, tn1), lambda j, t, te, nv: (layer, te[t], 0, j)),
                      pl.BlockSpec((1, 1, d, tn1), lambda j, t, te, nv: (layer, te[t], 0, nb1 + j)),
                      pl.BlockSpec((1, 1, 1, tn1), lambda j, t, te, nv: (layer, te[t], 0, j)),
                      pl.BlockSpec((1, 1, 1, tn1), lambda j, t, te, nv: (layer, te[t], 0, nb1 + j))],
            out_specs=pl.BlockSpec((MOE_TM, tn1), lambda j, t, te, nv: (t, j)),
            scratch_shapes=[pltpu.VMEM((d, tn1), BF16), pltpu.VMEM((d, tn1), BF16)]),
        compiler_params=_params("arbitrary", "arbitrary"),
        name="moe_gate_up",
    )(tile_expert, n_valid, xs, w_gu, w_gu, b_gu4, b_gu4)

    tn2 = min(2048, d)
    nb2 = d // tn2
    return pl.pallas_call(
        functools.partial(_moe_down_kernel, k_chunk=min(512, f)),
        out_shape=jax.ShapeDtypeStruct((n_slots, d), F32),
        grid_spec=pltpu.PrefetchScalarGridSpec(
            num_scalar_prefetch=2,
            grid=(nb2, n_tiles),
            in_specs=[pl.BlockSpec((MOE_TM, f), lambda j, t, te, nv: (last(t, nv), 0)),
                      pl.BlockSpec((1, 1, f, tn2), lambda j, t, te, nv: (layer, te[t], 0, j)),
                      pl.BlockSpec((1, 1, 1, tn2), lambda j, t, te, nv: (layer, te[t], 0, j))],
            out_specs=pl.BlockSpec((MOE_TM, tn2), lambda j, t, te, nv: (t, j)),
            scratch_shapes=[pltpu.VMEM((f, tn2), BF16)]),
        compiler_params=_params("arbitrary", "arbitrary"),
        name="moe_down",
    )(tile_expert, n_valid, act, w_dn, b_dn.reshape(b_dn.shape[0], n_experts, 1, d))


def _combine_ln_kernel(dest_ref, ys_hbm, gate_ref, x_ref, g_ref, b_ref, o_ref, ob_ref, buf, sem, *, alpha):
    t = pl.program_id(0)
    nt = pl.num_programs(0)
    tr = o_ref.shape[0]

    def issue(tile, slot):
        def body(r, carry):
            for k in range(TOP_K):
                row = dest_ref[(tile * tr + r) * TOP_K + k]
                pltpu.make_async_copy(ys_hbm.at[pl.ds(row, 1), :], buf.at[slot, k, pl.ds(r, 1), :],
                                      sem.at[slot]).start()
            return carry
        lax.fori_loop(0, tr, body, 0)

    @pl.when(t == 0)
    def _():
        issue(0, 0)

    @pl.when(t + 1 < nt)
    def _():
        issue(t + 1, (t + 1) % 2)

    slot = t % 2
    for k in range(TOP_K):
        pltpu.make_async_copy(ys_hbm.at[pl.ds(0, tr), :], buf.at[slot, k], sem.at[slot]).wait()
    gates = gate_ref[...]
    f = gates[:, 0:1] * buf[slot, 0]
    for k in range(1, TOP_K):
        f = f + gates[:, k:k + 1] * buf[slot, k]
    y = _ln_rows(alpha * x_ref[...] + f, g_ref[...], b_ref[...])
    o_ref[...] = y
    ob_ref[...] = y.astype(ob_ref.dtype)


def _combine_ln(ys, dest, gates, x1, g, b, *, alpha):
    m, d = x1.shape
    row = lambda t, dst: (t, 0)
    fixed = lambda t, dst: (0, 0)
    return pl.pallas_call(
        functools.partial(_combine_ln_kernel, alpha=alpha),
        out_shape=(jax.ShapeDtypeStruct((m, d), F32), jax.ShapeDtypeStruct((m, d), BF16)),
        grid_spec=pltpu.PrefetchScalarGridSpec(
            num_scalar_prefetch=1,
            grid=(m // COMBINE_ROWS,),
            in_specs=[pl.BlockSpec(memory_space=pl.ANY),
                      pl.BlockSpec((COMBINE_ROWS, LANES), row),
                      pl.BlockSpec((COMBINE_ROWS, d), row),
                      pl.BlockSpec((1, d), fixed), pl.BlockSpec((1, d), fixed)],
            out_specs=(pl.BlockSpec((COMBINE_ROWS, d), row), pl.BlockSpec((COMBINE_ROWS, d), row)),
            scratch_shapes=[pltpu.VMEM((2, TOP_K, COMBINE_ROWS, d), F32), pltpu.SemaphoreType.DMA((2,))]),
        compiler_params=_params("arbitrary"),
        name="moe_combine_ln",
    )(dest, ys, gates, x1, g.reshape(1, d), b.reshape(1, d))


def _moe_block(x, mix, n_tokens, layer, ln1_g, ln1_b, ln2_g, ln2_b, rw, rb, w_gu, b_gu, w_dn, b_dn, *, alpha):
    n_experts = rw.shape[1]
    x1, ids, gates, ranks, counts = _ln_router(x, mix, ln1_g, ln1_b, rw, rb, alpha=alpha, tr=256, n_tokens=n_tokens)
    dest_d, dest_c, tile_expert, n_valid, zero_fill, n_tiles, total_tiles = _moe_plan(
        ids[:, :TOP_K], ranks[:, :TOP_K], counts[0, :n_experts].astype(I32), n_tokens, n_experts)
    xs = _dispatch(x1, dest_d, zero_fill, total_tiles)
    ys = _moe_experts(xs, tile_expert, n_valid, w_gu, b_gu, w_dn, b_dn, layer, n_tiles)
    return _combine_ln(ys, dest_c, gates, x1, ln2_g, ln2_b, alpha=alpha)


ROW_TILE = 768


def kernel(x_prompt, x_sample, cache_k, cache_v, cache_kidx, state_conv, state_lru, state_s5_re, state_s5_im,
           page_table, w_in, w_idx_qb, kidx_ln_g, kidx_ln_b, conv_w, conv_b, lru_wa, lru_ba, lru_wx, lru_bx,
           lru_lambda, w_out, s5_a_re, s5_a_im, s5_log_step, s5_b_re, s5_b_im, s5_c_re, s5_c_im, s5_d,
           glu_w1, glu_b1, glu_w2, glu_b2, ln1_g, ln1_b, ln2_g, ln2_b, router_w, router_b, moe_w_gu, moe_b_gu,
           moe_w_down, moe_b_down):
    bp, sp, dm = x_prompt.shape
    bs, ss, _ = x_sample.shape
    depth = ln1_g.shape[0]
    alpha = (2 * depth) ** 0.25
    n_p, n_s = bp * sp, bs * ss
    n_tok = n_p + n_s
    n_rows = -(-n_tok // ROW_TILE) * ROW_TILE
    lru_w = conv_w.shape[2]
    q_w = w_out.shape[1] - lru_w
    kv_w = N_KV_HEADS * HEAD_DIM
    s5_groups = dm // S5_GROUP

    def unify(a_p, a_s):
        pad = jnp.zeros((n_rows - n_tok, a_p.shape[-1]), a_p.dtype)
        return jnp.concatenate([a_p.reshape(n_p, -1), a_s.reshape(n_s, -1), pad], axis=0)

    x = unify(x_prompt, x_sample)
    xb = x.astype(BF16)
    outs = {name: [] for name in ("k_p", "k_s", "v_p", "v_s", "i_p", "i_s", "c_p", "c_s", "h_p", "h_s",
                                   "r_p", "r_s", "m_p", "m_s")}
    for layer in range(depth):
        j = layer // 2
        if layer % 2 == 0:
            sizes = [q_w, kv_w, kv_w, IDX_Q_RANK, IDX_HEADS, IDX_DIM, lru_w, lru_w]
            cuts = [0]
            for s_ in sizes:
                cuts.append(cuts[-1] + s_)
            seg = lambda i: w_in[j][:, cuts[i]:cuts[i + 1]]
            lay = {"q": 0, "q_w": q_w, "u": q_w, "g": q_w + lru_w, "k": q_w + 2 * lru_w,
                   "v": q_w + 2 * lru_w + kv_w, "ki": q_w + 2 * lru_w + 2 * kv_w}
            lay["wi"] = lay["ki"] + IDX_DIM
            n_used = lay["wi"] + IDX_HEADS
            n_proj = -(-n_used // 512) * 512
            w_main = jnp.concatenate([seg(0), seg(6), seg(7), seg(1), seg(2), seg(5), seg(4),
                                      jnp.zeros((dm, n_proj - n_used), F32)], axis=1).astype(BF16)
            proj = _matmul(xb, w_main, tm=ROW_TILE, tn=512, name="in_proj")
            cq = _matmul(xb, seg(3).astype(BF16), tm=ROW_TILE, tn=IDX_Q_RANK // 2, out_dtype=BF16, name="cq_proj")
            q_idx = _matmul(cq, w_idx_qb[j].astype(BF16), tm=ROW_TILE, tn=1024, out_dtype=BF16, name="q_idx_proj")
            k_idx = _layer_norm_cols(proj, kidx_ln_g[j], kidx_ln_b[j], tr=ROW_TILE, col_block=lay["ki"] // IDX_DIM,
                                     width=IDX_DIM)
            lru_args = (conv_w[j], conv_b[j], lru_wa[j], lru_ba[j], lru_wx[j], lru_bx[j], lru_lambda[j])

            att_p = _dsa_prompt(proj, q_idx, k_idx, lay, bsz=bp, seq=sp, tq=128)
            rec_p, h_p = _rglru(proj, proj, jnp.zeros((bp, CONV_W - 1, lru_w), F32), jnp.zeros((bp, lru_w), F32),
                                *lru_args, bsz=bp, t_len=sp, t_valid=sp, u_col=lay["u"], g_col=lay["g"],
                                width=lru_w, wb=512)
            col = lambda a, name, width: a[:, lay[name]:lay[name] + width]
            proj_p = proj[:n_p]
            outs["k_p"].append(col(proj_p, "k", kv_w).reshape(bp, sp, N_KV_HEADS, HEAD_DIM))
            outs["v_p"].append(col(proj_p, "v", kv_w).reshape(bp, sp, N_KV_HEADS, HEAD_DIM))
            outs["i_p"].append(k_idx[:n_p].reshape(bp, sp, IDX_DIM))
            tail = CONV_W - 1
            outs["c_p"].append(jnp.stack([proj[(b + 1) * sp - tail:(b + 1) * sp, lay["u"]:lay["u"] + lru_w]
                                          for b in range(bp)]))
            outs["h_p"].append(h_p)

            proj_s = proj[n_p:n_tok]
            s3 = lambda a: a.reshape(bs, ss, -1)
            q_s, k_s, v_s = s3(col(proj_s, "q", q_w)), s3(col(proj_s, "k", kv_w)), s3(col(proj_s, "v", kv_w))
            u_s, g_s = s3(col(proj_s, "u", lru_w)), s3(col(proj_s, "g", lru_w))
            wi_s = s3(col(proj_s, "wi", IDX_HEADS))
            kidx_s = s3(k_idx[n_p:n_tok])
            att_s = _dsa_sample(q_s, k_s, v_s, s3(q_idx[n_p:n_tok]), wi_s, kidx_s, cache_k, cache_v, cache_kidx,
                                page_table, j)
            t_pad = -(-ss // SUBLANES) * SUBLANES
            pad_t = lambda a: jnp.pad(a, ((0, 0), (0, t_pad - ss), (0, 0))).reshape(bs * t_pad, -1)
            rec_s, h_s = _rglru(pad_t(u_s), pad_t(g_s), state_conv[j], state_lru[j], *lru_args, bsz=bs, t_len=t_pad,
                                t_valid=ss, u_col=0, g_col=0, width=lru_w, wb=512)
            rec_s = rec_s.reshape(bs, t_pad, lru_w)[:, :ss]
            outs["k_s"].append(k_s.reshape(bs, ss, N_KV_HEADS, HEAD_DIM))
            outs["v_s"].append(v_s.reshape(bs, ss, N_KV_HEADS, HEAD_DIM))
            outs["i_s"].append(kidx_s)
            up_s = jnp.concatenate([state_conv[j], u_s], axis=1)
            outs["c_s"].append(up_s[:, ss:])
            outs["h_s"].append(h_s)

            att = unify(att_p, att_s.astype(BF16))
            rec = unify(rec_p, rec_s)
            mix = _matmul_2in(att, rec, w_out[j].astype(BF16), tm=ROW_TILE, tn=512, name="out_proj")
        else:
            ab_re, ab_im, cf_re, cf_im = _s5_discretize(s5_a_re[j], s5_a_im[j], s5_log_step[j])
            b_re, b_im = s5_b_re[j], s5_b_im[j]
            bb_re = cf_re[..., None] * b_re - cf_im[..., None] * b_im
            bb_im = cf_re[..., None] * b_im + cf_im[..., None] * b_re
            s5_args = (ab_re, ab_im, bb_re, bb_im, s5_c_re[j], s5_c_im[j], s5_d[j])
            n_state = ab_re.shape[1]
            zeros = jnp.zeros((bp, s5_groups, n_state), F32)
            z_p, hr, hi = _s5_scan(x[:n_p].reshape(bp, sp, dm), zeros, zeros, *s5_args)
            outs["r_p"].append(hr)
            outs["m_p"].append(hi)
            z_s, hr, hi = _s5_scan(x[n_p:n_tok].reshape(bs, ss, dm), state_s5_re[j], state_s5_im[j], *s5_args)
            outs["r_s"].append(hr)
            outs["m_s"].append(hi)
            mix = _glu(unify(z_p, z_s), glu_w1[j].astype(BF16), glu_b1[j], glu_w2[j].astype(BF16), glu_b2[j],
                       tm=ROW_TILE, tn=512)
        x, xb = _moe_block(x, mix, n_tok, layer, ln1_g[layer], ln1_b[layer], ln2_g[layer], ln2_b[layer],
                           router_w[layer], router_b[layer], moe_w_gu, moe_b_gu, moe_w_down, moe_b_down,
                           alpha=alpha)
    st = lambda name: jnp.stack(outs[name])
    return (x[:n_p].reshape(bp, sp, dm), x[n_p:n_tok].reshape(bs, ss, dm),
            st("k_p"), st("k_s"), st("v_p"), st("v_s"), st("i_p"), st("i_s"), st("c_p"), st("c_s"),
            st("h_p"), st("h_s"), st("r_p"), st("r_s"), st("m_p"), st("m_s"))
```

```python
import functools
import math

import jax
import jax.numpy as jnp
from jax import lax
from jax.experimental import pallas as pl
from jax.experimental.pallas import tpu as pltpu

F32 = jnp.float32
BF16 = jnp.bfloat16
I32 = jnp.int32

HEAD_DIM = 128
N_KV_HEADS = 4
IDX_HEADS = 32
IDX_DIM = 128
IDX_Q_RANK = 768
TOPK_MAX = 256
LRU_HEADS = 16
CONV_W = 4
LRU_C = 8.0
S5_GROUP = 16
TOP_K = 4
SWIGLU_LIMIT = 7.0
SWIGLU_ALPHA = 1.702
LN_EPS = 1e-5

LANES = 128
SUBLANES = 8
VMEM_LIMIT_BYTES = 56 * 1024 * 1024

NEG_INF = float("-inf")
INT_MIN = -(2 ** 31)


def _params(*sem):
    return pltpu.CompilerParams(dimension_semantics=sem, vmem_limit_bytes=VMEM_LIMIT_BYTES)


def _nt_dot(a, b):
    return lax.dot_general(a, b, (((1,), (1,)), ((), ())), preferred_element_type=F32)


def _dot(a, b):
    return jnp.dot(a, b, preferred_element_type=F32)


def _mm_kernel(x_ref, w_ref, o_ref):
    o_ref[...] = _dot(x_ref[...].astype(BF16), w_ref[...]).astype(o_ref.dtype)


def _matmul(x, w, *, tm, tn, out_dtype=F32, name="matmul"):
    m, k = x.shape
    n = w.shape[1]
    return pl.pallas_call(
        _mm_kernel,
        out_shape=jax.ShapeDtypeStruct((m, n), out_dtype),
        grid=(m // tm, n // tn),
        in_specs=[pl.BlockSpec((tm, k), lambda i, j: (i, 0)),
                  pl.BlockSpec((k, tn), lambda i, j: (0, j))],
        out_specs=pl.BlockSpec((tm, tn), lambda i, j: (i, j)),
        compiler_params=_params("parallel", "arbitrary"),
        name=name,
    )(x, w)


def _mm2_kernel(xa_ref, xb_ref, wa_ref, wb_ref, o_ref):
    o_ref[...] = _dot(xa_ref[...], wa_ref[...]) + _dot(xb_ref[...], wb_ref[...])


def _matmul_2in(xa, xb, w, *, tm, tn, name):
    m, ka = xa.shape
    n = w.shape[1]
    return pl.pallas_call(
        _mm2_kernel,
        out_shape=jax.ShapeDtypeStruct((m, n), F32),
        grid=(m // tm, n // tn),
        in_specs=[pl.BlockSpec((tm, ka), lambda i, j: (i, 0)),
                  pl.BlockSpec((tm, ka), lambda i, j: (i, 0)),
                  pl.BlockSpec((ka, tn), lambda i, j: (0, j)),
                  pl.BlockSpec((ka, tn), lambda i, j: (1, j))],
        out_specs=pl.BlockSpec((tm, tn), lambda i, j: (i, j)),
        compiler_params=_params("parallel", "arbitrary"),
        name=name,
    )(xa, xb, w, w)


def _glu_kernel(z_ref, w1_ref, b1_ref, w2_ref, b2_ref, o_ref):
    z = z_ref[...]
    a = _dot(z, w1_ref[...]) + b1_ref[...]
    b = _dot(z, w2_ref[...]) + b2_ref[...]
    o_ref[...] = a * jax.nn.sigmoid(b)


def _glu(z, w1, b1, w2, b2, *, tm, tn):
    m, k = z.shape
    n = w1.shape[1]
    x_spec = pl.BlockSpec((tm, k), lambda i, j: (i, 0))
    w_spec = pl.BlockSpec((k, tn), lambda i, j: (0, j))
    b_spec = pl.BlockSpec((1, tn), lambda i, j: (0, j))
    return pl.pallas_call(
        _glu_kernel,
        out_shape=jax.ShapeDtypeStruct((m, n), F32),
        grid=(m // tm, n // tn),
        in_specs=[x_spec, w_spec, b_spec, w_spec, b_spec],
        out_specs=pl.BlockSpec((tm, tn), lambda i, j: (i, j)),
        compiler_params=_params("parallel", "arbitrary"),
        name="glu",
    )(z, w1, b1.reshape(1, n), w2, b2.reshape(1, n))


def _ln_rows(x, g, b):
    mu = jnp.mean(x, axis=-1, keepdims=True)
    xc = x - mu
    var = jnp.mean(xc * xc, axis=-1, keepdims=True)
    return xc * lax.rsqrt(var + LN_EPS) * g + b


def _ln_kernel(x_ref, g_ref, b_ref, o_ref):
    o_ref[...] = _ln_rows(x_ref[...], g_ref[...], b_ref[...])


def _layer_norm_cols(x, g, b, *, tr, col_block, width):
    m = x.shape[0]
    return pl.pallas_call(
        _ln_kernel,
        out_shape=jax.ShapeDtypeStruct((m, width), F32),
        grid=(m // tr,),
        in_specs=[pl.BlockSpec((tr, width), lambda i: (i, col_block)),
                  pl.BlockSpec((1, width), lambda i: (0, 0)),
                  pl.BlockSpec((1, width), lambda i: (0, 0))],
        out_specs=pl.BlockSpec((tr, width), lambda i: (i, 0)),
        compiler_params=_params("parallel"),
        name="layer_norm",
    )(x, g.reshape(1, width), b.reshape(1, width))


def _top_gates(logits, n_experts):
    lane_f = lax.broadcasted_iota(I32, logits.shape, 1).astype(F32)
    l = jnp.where(lane_f < float(n_experts), logits, NEG_INF)
    vals, ids = [], []
    for _ in range(TOP_K):
        m = jnp.max(l, axis=-1, keepdims=True)
        idx = jnp.min(jnp.where(l == m, lane_f, float(LANES)), axis=-1, keepdims=True)
        vals.append(m)
        ids.append(idx)
        l = jnp.where(lane_f == idx, NEG_INF, l)
    es = [jnp.exp(v - vals[0]) for v in vals]
    den = es[0]
    for e in es[1:]:
        den = den + e
    return ids, [e / den for e in es]


def _ln_router_kernel(x_ref, mix_ref, g_ref, b_ref, rw_ref, rb_ref, tri_ref, o_ref, id_ref, gate_ref, rank_ref,
                      cnt_ref, *, alpha, n_experts, n_tokens):
    i = pl.program_id(0)
    tr = x_ref.shape[0]
    y = _ln_rows(alpha * x_ref[...] + mix_ref[...], g_ref[...], b_ref[...])
    o_ref[...] = y
    logits = _dot(y.astype(BF16), rw_ref[...]) + rb_ref[...]
    ids, gates = _top_gates(logits, n_experts)

    lane = lax.broadcasted_iota(I32, logits.shape, 1)
    lane_f = lane.astype(F32)
    valid = (i * tr + lax.broadcasted_iota(I32, logits.shape, 0)) < n_tokens
    chosen = jnp.zeros(logits.shape, F32)
    for r in range(TOP_K):
        chosen = chosen + jnp.where(lane_f == ids[r], 1.0, 0.0)
    chosen = jnp.where(valid, chosen, 0.0)

    @pl.when(i == 0)
    def _():
        cnt_ref[...] = jnp.zeros(cnt_ref.shape, F32)

    before = _dot(tri_ref[...], chosen.astype(BF16)) + cnt_ref[...]
    cnt_ref[...] = cnt_ref[...] + jnp.sum(chosen, axis=0, keepdims=True)

    idv = jnp.zeros(logits.shape, I32)
    gv = jnp.zeros(logits.shape, F32)
    rv = jnp.zeros(logits.shape, I32)
    for r in range(TOP_K):
        rank = jnp.sum(jnp.where(lane_f == ids[r], before, 0.0), axis=-1, keepdims=True)
        idv = jnp.where(lane == r, ids[r].astype(I32), idv)
        gv = jnp.where(lane == r, gates[r], gv)
        rv = jnp.where(lane == r, rank.astype(I32), rv)
    id_ref[...] = idv
    gate_ref[...] = gv
    rank_ref[...] = rv


def _ln_router(x, mix, g, b, rw, rb, *, alpha, tr, n_tokens):
    m, d = x.shape
    n_experts = rw.shape[1]
    rw_p = jnp.zeros((d, LANES), BF16).at[:, :n_experts].set(rw.astype(BF16))
    rb_p = jnp.zeros((1, LANES), F32).at[0, :n_experts].set(rb)
    tri = (jnp.arange(tr)[None, :] < jnp.arange(tr)[:, None]).astype(BF16)
    row = lambda i: (i, 0)
    fixed = lambda i: (0, 0)
    lanes_i = jax.ShapeDtypeStruct((m, LANES), I32)
    return pl.pallas_call(
        functools.partial(_ln_router_kernel, alpha=alpha, n_experts=n_experts, n_tokens=n_tokens),
        out_shape=(jax.ShapeDtypeStruct((m, d), F32), lanes_i, jax.ShapeDtypeStruct((m, LANES), F32), lanes_i,
                   jax.ShapeDtypeStruct((1, LANES), F32)),
        grid=(m // tr,),
        in_specs=[pl.BlockSpec((tr, d), row), pl.BlockSpec((tr, d), row),
                  pl.BlockSpec((1, d), fixed), pl.BlockSpec((1, d), fixed),
                  pl.BlockSpec((d, LANES), fixed), pl.BlockSpec((1, LANES), fixed),
                  pl.BlockSpec((tr, tr), fixed)],
        out_specs=(pl.BlockSpec((tr, d), row), pl.BlockSpec((tr, LANES), row), pl.BlockSpec((tr, LANES), row),
                   pl.BlockSpec((tr, LANES), row), pl.BlockSpec((1, LANES), fixed)),
        compiler_params=_params("arbitrary"),
        name="ln_router",
    )(x, mix, g.reshape(1, d), b.reshape(1, d), rw_p, rb_p, tri)


def _sortable_key(x):
    bits = pltpu.bitcast(x, I32)
    return jnp.where(bits < 0, bits ^ jnp.int32(0x7FFFFFFF), bits)


def _count(mask):
    return jnp.sum(jnp.where(mask, 1.0, 0.0), axis=-1, keepdims=True)


def _kth_largest_key(keys, k):
    def body(i, t):
        cand = t + lax.shift_left(jnp.int32(1), jnp.int32(31) - i)
        cnt = _count(keys[0] >= cand)
        for kk in keys[1:]:
            cnt = cnt + _count(kk >= cand)
        return jnp.where(cnt >= float(k), cand, t)

    t0 = jnp.full((keys[0].shape[0], 1), INT_MIN, I32)
    return lax.fori_loop(0, 32, body, t0)


def _strict_upper(n):
    return (jnp.arange(n)[:, None] < jnp.arange(n)[None, :]).astype(BF16)


KEY_CHUNK = 256
ATT_CHUNK = 1024
KEY_NEG_INF = -2139095041


def _dsa_prompt_kernel(q_ref, k_ref, v_ref, qi_ref, wi_ref, ki_ref, tri_ref, o_ref, key_s, bias_s, m_s, l_s, acc_s,
                       k_bf, v_bf, ki_bf, *, topk, n_heads, w_scale):
    i = pl.program_id(1)
    tq = q_ref.shape[0]
    kc = KEY_CHUNK
    n_ch = ((i + 1) * tq + kc - 1) // kc

    @pl.when(i == 0)
    def _():
        k_bf[...] = k_ref[...].astype(BF16)
        v_bf[...] = v_ref[...].astype(BF16)
        ki_bf[...] = ki_ref[...].astype(BF16)

    w = wi_ref[...][:, :IDX_HEADS] * w_scale
    row = i * tq + lax.broadcasted_iota(I32, (tq, kc), 0)
    lane = lax.broadcasted_iota(I32, (tq, kc), 1)
    qi_stack = jnp.concatenate([qi_ref[:, h * IDX_DIM:(h + 1) * IDX_DIM] for h in range(IDX_HEADS)], axis=0)

    def score_chunk(c, carry):
        k0 = pl.multiple_of(c * kc, kc)
        d = _nt_dot(qi_stack, ki_bf[pl.ds(k0, kc), :])
        sc = jnp.zeros((tq, kc), F32)
        for h in range(IDX_HEADS):
            sc = sc + jnp.maximum(d[h * tq:(h + 1) * tq, :], 0.0) * w[:, h:h + 1]
        key_s[c] = _sortable_key(jnp.where(k0 + lane <= row, sc, NEG_INF))
        return carry

    lax.fori_loop(0, n_ch, score_chunk, 0)

    def count(pred):
        def body(c, acc):
            return acc + jnp.where(pred(key_s[c]), 1.0, 0.0)
        return jnp.sum(lax.fori_loop(0, n_ch, body, jnp.zeros((tq, kc), F32)), axis=-1, keepdims=True)

    def bisect(it, t):
        cand = t + lax.shift_left(jnp.int32(1), jnp.int32(31) - it)
        return jnp.where(count(lambda key: key >= cand) >= float(topk), cand, t)

    thr = lax.fori_loop(0, 32, bisect, jnp.full((tq, 1), INT_MIN, I32))

    def bias_chunk(c, carry):
        causal = c * kc + lane <= row
        bias_s[c] = jnp.where((key_s[c] >= thr) & causal, 0.0, NEG_INF)
        return carry

    lax.fori_loop(0, n_ch, bias_chunk, 0)
    tied = (thr > KEY_NEG_INF) & (count(lambda key: key >= thr) > float(topk))

    @pl.when(jnp.max(jnp.where(tied, 1.0, 0.0)) > 0.0)
    def _():
        need = float(topk) - count(lambda key: key > thr)

        def rank_chunk(c, seen):
            key = key_s[c]
            eq = key == thr
            eqf = jnp.where(eq, 1.0, 0.0)
            rank = _dot(eqf.astype(BF16), tri_ref[...]) + seen
            sel = ((key > thr) | (eq & (rank < need))) & (c * kc + lane <= row)
            bias_s[c] = jnp.where(sel, 0.0, NEG_INF)
            return seen + jnp.sum(eqf, axis=-1, keepdims=True)

        lax.fori_loop(0, n_ch, rank_chunk, jnp.zeros((tq, 1), F32))

    ac = min(ATT_CHUNK, k_ref.shape[0])
    per_att = ac // kc
    n_att = (n_ch + per_att - 1) // per_att

    def mask_chunk(c, carry):
        bias_s[c] = jnp.full((tq, kc), NEG_INF, F32)
        return carry

    lax.fori_loop(n_ch, n_att * per_att, mask_chunk, 0)

    group = n_heads // N_KV_HEADS
    for kv in range(N_KV_HEADS):
        cols = slice(kv * HEAD_DIM, (kv + 1) * HEAD_DIM)
        heads = range(kv * group, (kv + 1) * group)
        qs = jnp.concatenate([q_ref[:, hh * HEAD_DIM:(hh + 1) * HEAD_DIM] for hh in heads], axis=0).astype(BF16)
        m_s[...] = jnp.full(m_s.shape, NEG_INF, F32)
        l_s[...] = jnp.zeros(l_s.shape, F32)
        acc_s[...] = jnp.zeros(acc_s.shape, F32)

        def attend_chunk(a, carry):
            k0 = pl.multiple_of(a * ac, ac)
            bias = jnp.concatenate([bias_s[a * per_att + u] for u in range(per_att)], axis=1)
            bias = jnp.concatenate([bias] * group, axis=0)
            s = _nt_dot(qs, k_bf[pl.ds(k0, ac), cols]) * (HEAD_DIM ** -0.5) + bias
            m_old = m_s[:, 0:1]
            m_new = jnp.maximum(m_old, jnp.max(s, axis=-1, keepdims=True))
            m_safe = jnp.where(m_new == NEG_INF, 0.0, m_new)
            alpha = jnp.exp(m_old - m_safe)
            p = jnp.exp(s - m_safe)
            l_s[...] = alpha * l_s[...] + jnp.sum(p, axis=-1, keepdims=True)
            acc_s[...] = alpha * acc_s[...] + _dot(p.astype(BF16), v_bf[pl.ds(k0, ac), cols])
            m_s[...] = jnp.broadcast_to(m_new, m_s.shape)
            return carry

        lax.fori_loop(0, n_att, attend_chunk, 0)
        o = acc_s[...] / l_s[:, 0:1]
        for g, hh in enumerate(heads):
            o_ref[:, hh * HEAD_DIM:(hh + 1) * HEAD_DIM] = o[g * tq:(g + 1) * tq, :].astype(o_ref.dtype)


def _dsa_prompt(proj, q_idx, k_idx, lay, *, bsz, seq, tq):
    topk = min(TOPK_MAX, seq // 4)
    q_w = lay["q_w"]
    kv_w = N_KV_HEADS * HEAD_DIM
    nq = seq // tq
    rows = (q_w // HEAD_DIM // N_KV_HEADS) * tq
    return pl.pallas_call(
        functools.partial(_dsa_prompt_kernel, topk=topk, n_heads=q_w // HEAD_DIM,
                          w_scale=(IDX_HEADS * IDX_DIM) ** -0.5),
        out_shape=jax.ShapeDtypeStruct((bsz * seq, q_w), BF16),
        grid=(bsz, nq),
        in_specs=[pl.BlockSpec((tq, q_w), lambda b, i: (b * nq + i, lay["q"] // q_w)),
                  pl.BlockSpec((seq, kv_w), lambda b, i: (b, lay["k"] // kv_w)),
                  pl.BlockSpec((seq, kv_w), lambda b, i: (b, lay["v"] // kv_w)),
                  pl.BlockSpec((tq, IDX_HEADS * IDX_DIM), lambda b, i: (b * nq + i, 0)),
                  pl.BlockSpec((tq, LANES), lambda b, i: (b * nq + i, lay["wi"] // LANES)),
                  pl.BlockSpec((seq, IDX_DIM), lambda b, i: (b, 0)),
                  pl.BlockSpec((KEY_CHUNK, KEY_CHUNK), lambda b, i: (0, 0))],
        out_specs=pl.BlockSpec((tq, q_w), lambda b, i: (b * nq + i, 0)),
        scratch_shapes=[pltpu.VMEM((seq // KEY_CHUNK, tq, KEY_CHUNK), I32),
                        pltpu.VMEM((seq // KEY_CHUNK, tq, KEY_CHUNK), F32),
                        pltpu.VMEM((rows, LANES), F32), pltpu.VMEM((rows, LANES), F32),
                        pltpu.VMEM((rows, HEAD_DIM), F32),
                        pltpu.VMEM((seq, kv_w), BF16), pltpu.VMEM((seq, kv_w), BF16),
                        pltpu.VMEM((seq, IDX_DIM), BF16)],
        compiler_params=_params("parallel", "arbitrary"),
        name="dsa_prompt",
    )(proj, proj, proj, q_idx, proj, k_idx, _strict_upper(KEY_CHUNK))


PAGES_PER_STEP = 8


def _page_scores(qi_rows, w_rows, page_bf, t_new):
    d = jnp.maximum(_nt_dot(qi_rows, page_bf), 0.0) * w_rows
    return jnp.concatenate(
        [jnp.sum(d[t * IDX_HEADS:(t + 1) * IDX_HEADS], axis=0, keepdims=True) for t in range(t_new)], axis=0)


def _dsa_sample_scores_kernel(pt_ref, qi_ref, w_ref, *rest, t_new, page):
    pages = rest[:PAGES_PER_STEP]
    knew_ref, sc_ref, scn_ref = rest[PAGES_PER_STEP:]
    p = pl.program_id(1)
    qi = qi_ref[0]
    w = w_ref[0]
    for g in range(PAGES_PER_STEP):
        sc_ref[0, :, g * page:(g + 1) * page] = _page_scores(qi, w, pages[g][0, 0].astype(BF16), t_new)

    @pl.when(p == pl.num_programs(1) - 1)
    def _():
        sc = _page_scores(qi, w, knew_ref[0].astype(BF16), t_new)
        row = lax.broadcasted_iota(I32, sc.shape, 0)
        col = lax.broadcasted_iota(I32, sc.shape, 1)
        scn_ref[0] = jnp.where(col <= row, sc, NEG_INF)


def _dsa_sample_attn_kernel(pt_ref, sc_ref, scn_ref, q_ref, *rest, t_new, page, topk, n_heads):
    kp = rest[:PAGES_PER_STEP]
    vp = rest[PAGES_PER_STEP:2 * PAGES_PER_STEP]
    knew_ref, vnew_ref, tri_ref, o_ref, thr_s, need_s, cnt_s, m_s, l_s, acc_s, bias_s, flag_s = rest[2 * PAGES_PER_STEP:]
    p = pl.program_id(1)
    chunk = PAGES_PER_STEP * page
    rows = t_new * n_heads

    @pl.when(p == 0)
    def _():
        key_p = _sortable_key(sc_ref[0])
        key_n = _sortable_key(scn_ref[0])
        thr = _kth_largest_key([key_p, key_n], topk)
        n_gt = _count(key_p > thr) + _count(key_n > thr)
        n_eq = _count(key_p == thr) + _count(key_n == thr)
        need = float(topk) - n_gt
        thr_s[...] = jnp.broadcast_to(thr, thr_s.shape)
        need_s[...] = jnp.broadcast_to(need, need_s.shape)
        cnt_s[...] = jnp.zeros(cnt_s.shape, F32)
        flag_s[0] = (jnp.max(n_eq - need) > 0.0).astype(I32)
        m_s[...] = jnp.full(m_s.shape, NEG_INF, F32)
        l_s[...] = jnp.zeros(l_s.shape, F32)
        acc_s[...] = jnp.zeros(acc_s.shape, F32)

    thr = thr_s[:, 0:1]
    q = [q_ref[0, kv] for kv in range(N_KV_HEADS)]

    def select(sc, width):
        key = _sortable_key(sc)
        bias_s[:, :width] = jnp.where(key >= thr, 0.0, NEG_INF)

        @pl.when(flag_s[0] != 0)
        def _():
            eq = key == thr
            eqf = jnp.where(eq, 1.0, 0.0)
            rank = _dot(eqf.astype(BF16), tri_ref[:width, :width]) + cnt_s[:, 0:1]
            sel = (key > thr) | (eq & (rank < need_s[:, 0:1]))
            bias_s[:, :width] = jnp.where(sel, 0.0, NEG_INF)
            cnt_s[...] = cnt_s[...] + jnp.sum(eqf, axis=-1, keepdims=True)

    def attend(k_pages, v_pages, bias_t):
        bias = jnp.concatenate([jnp.broadcast_to(bias_t[t:t + 1], (n_heads, bias_t.shape[1])) for t in range(t_new)],
                               axis=0)
        s_pages = []
        for k_kv in k_pages:
            s = _nt_dot(q[0], k_kv[0])
            for kv in range(1, N_KV_HEADS):
                s = s + _nt_dot(q[kv], k_kv[kv])
            s_pages.append(s)
        s = jnp.concatenate(s_pages, axis=1) * (HEAD_DIM ** -0.5) + bias
        m_old = m_s[:, 0:1]
        m_new = jnp.maximum(m_old, jnp.max(s, axis=-1, keepdims=True))
        m_safe = jnp.where(m_new == NEG_INF, 0.0, m_new)
        alpha = jnp.exp(m_old - m_safe)
        pr = jnp.exp(s - m_safe)
        l_s[...] = alpha * l_s[...] + jnp.sum(pr, axis=-1, keepdims=True)
        pb = pr.astype(BF16)
        for kv in range(N_KV_HEADS):
            cols = slice(kv * HEAD_DIM, (kv + 1) * HEAD_DIM)
            pv = _dot(pb[:, 0:page], v_pages[0][kv])
            for g in range(1, len(v_pages)):
                pv = pv + _dot(pb[:, g * page:(g + 1) * page], v_pages[g][kv])
            acc_s[:, cols] = alpha * acc_s[:, cols] + pv
        m_s[...] = jnp.broadcast_to(m_new, m_s.shape)

    def page_heads(ref):
        return [ref[0, 0, :, kv, :].astype(BF16) for kv in range(N_KV_HEADS)]

    def row_heads(ref):
        return [ref[0, :, kv * HEAD_DIM:(kv + 1) * HEAD_DIM].astype(BF16) for kv in range(N_KV_HEADS)]

    start = pl.multiple_of(p * chunk, chunk)
    select(sc_ref[0, :, pl.ds(start, chunk)], chunk)
    attend([page_heads(r) for r in kp], [page_heads(r) for r in vp], bias_s[...])

    @pl.when(p == pl.num_programs(1) - 1)
    def _():
        select(scn_ref[0], page)
        attend([row_heads(knew_ref)], [row_heads(vnew_ref)], bias_s[:, :page])
        o_ref[0] = acc_s[...] / l_s[:, 0:1]


def _dsa_sample(q, k, v, q_idx, w_idx, k_idx, cache_k, cache_v, cache_kidx, page_table, layer):
    bsz, t_new, q_w = q.shape
    n_heads = q_w // HEAD_DIM
    kv_w = N_KV_HEADS * HEAD_DIM
    page = cache_k.shape[2]
    n_pages = page_table.shape[1]
    past = n_pages * page
    topk = min(TOPK_MAX, (past + t_new) // 4)
    steps = n_pages // PAGES_PER_STEP
    chunk = PAGES_PER_STEP * page
    rows = t_new * n_heads

    qi_rows = q_idx.reshape(bsz, t_new * IDX_HEADS, IDX_DIM).astype(BF16)
    w_rows = (w_idx * (IDX_HEADS * IDX_DIM) ** -0.5).reshape(bsz, t_new * IDX_HEADS, 1)
    pad_rows = lambda a: jnp.pad(a, ((0, 0), (0, page - t_new), (0, 0)))
    kidx_new, k_new, v_new = pad_rows(k_idx), pad_rows(k), pad_rows(v)

    def page_spec(trailing, g):
        zeros = (0,) * (1 + len(trailing))
        return pl.BlockSpec((1, 1, page) + trailing,
                            lambda b, p, pt: (layer, pt[b, p * PAGES_PER_STEP + g]) + zeros)

    per_b = lambda b, p, pt: (b, 0, 0)
    sc_past, sc_new = pl.pallas_call(
        functools.partial(_dsa_sample_scores_kernel, t_new=t_new, page=page),
        out_shape=(jax.ShapeDtypeStruct((bsz, t_new, past), F32),
                   jax.ShapeDtypeStruct((bsz, t_new, page), F32)),
        grid_spec=pltpu.PrefetchScalarGridSpec(
            num_scalar_prefetch=1,
            grid=(bsz, steps),
            in_specs=[pl.BlockSpec((1, t_new * IDX_HEADS, IDX_DIM), per_b),
                      pl.BlockSpec((1, t_new * IDX_HEADS, 1), per_b)]
                     + [page_spec((IDX_DIM,), g) for g in range(PAGES_PER_STEP)]
                     + [pl.BlockSpec((1, page, IDX_DIM), per_b)],
            out_specs=(pl.BlockSpec((1, t_new, chunk), lambda b, p, pt: (b, 0, p)),
                       pl.BlockSpec((1, t_new, page), per_b))),
        compiler_params=_params("parallel", "arbitrary"),
        name="dsa_sample_scores",
    )(page_table, qi_rows, w_rows, *([cache_kidx] * PAGES_PER_STEP), kidx_new)

    group = n_heads // N_KV_HEADS
    head_kv = jnp.arange(n_heads) // group
    blockdiag = (head_kv[:, None] == jnp.arange(N_KV_HEADS)[None, :]).astype(F32)
    q4 = q.reshape(bsz, t_new, n_heads, 1, HEAD_DIM) * blockdiag[None, None, :, :, None]
    q_bd = q4.transpose(0, 3, 1, 2, 4).reshape(bsz, N_KV_HEADS, rows, HEAD_DIM).astype(BF16)

    out = pl.pallas_call(
        functools.partial(_dsa_sample_attn_kernel, t_new=t_new, page=page, topk=topk, n_heads=n_heads),
        out_shape=jax.ShapeDtypeStruct((bsz, rows, kv_w), F32),
        grid_spec=pltpu.PrefetchScalarGridSpec(
            num_scalar_prefetch=1,
            grid=(bsz, steps),
            in_specs=[pl.BlockSpec((1, t_new, past), per_b),
                      pl.BlockSpec((1, t_new, page), per_b),
                      pl.BlockSpec((1, N_KV_HEADS, rows, HEAD_DIM), lambda b, p, pt: (b, 0, 0, 0))]
                     + [page_spec((N_KV_HEADS, HEAD_DIM), g) for g in range(PAGES_PER_STEP)]
                     + [page_spec((N_KV_HEADS, HEAD_DIM), g) for g in range(PAGES_PER_STEP)]
                     + [pl.BlockSpec((1, page, kv_w), per_b), pl.BlockSpec((1, page, kv_w), per_b),
                        pl.BlockSpec((chunk, chunk), lambda b, p, pt: (0, 0))],
            out_specs=pl.BlockSpec((1, rows, kv_w), per_b),
            scratch_shapes=[pltpu.VMEM((t_new, LANES), I32), pltpu.VMEM((t_new, LANES), F32),
                            pltpu.VMEM((t_new, LANES), F32), pltpu.VMEM((rows, LANES), F32),
                            pltpu.VMEM((rows, LANES), F32), pltpu.VMEM((rows, kv_w), F32),
                            pltpu.VMEM((t_new, chunk), F32), pltpu.SMEM((1,), I32)]),
        compiler_params=_params("parallel", "arbitrary"),
        name="dsa_sample_attn",
    )(page_table, sc_past, sc_new, q_bd, *([cache_k] * PAGES_PER_STEP), *([cache_v] * PAGES_PER_STEP),
      k_new, v_new, _strict_upper(chunk))
    o5 = out.reshape(bsz, t_new, N_KV_HEADS, group, N_KV_HEADS, HEAD_DIM)
    o = jnp.stack([o5[:, :, kv, :, kv, :] for kv in range(N_KV_HEADS)], axis=2)
    return o.reshape(bsz, t_new, q_w)


def _softplus(x):
    return jnp.maximum(x, 0.0) + jnp.log1p(jnp.exp(-jnp.abs(x)))


def _rglru_kernel(u_ref, g_ref, cp_ref, h0_ref, cw_ref, cb_ref, wa_ref, ba_ref, wx_ref, bx_ref, lam_ref,
                  y_ref, hl_ref, up_s, a_s, b_s, *, t_valid, t_chunk):
    t_len, wb = u_ref.shape
    lead = SUBLANES - (CONV_W - 1)
    up_s[0:SUBLANES, :] = jnp.zeros((SUBLANES, wb), F32)
    up_s[lead:SUBLANES, :] = cp_ref[0]
    up_s[SUBLANES:SUBLANES + t_len, :] = u_ref[...]
    neg_c_sp = -LRU_C * _softplus(-lam_ref[...])
    for c0 in range(0, t_len, t_chunk):
        xc = cw_ref[0:1, :] * up_s[lead + c0:lead + c0 + t_chunk, :]
        for j in range(1, CONV_W):
            xc = xc + cw_ref[j:j + 1, :] * up_s[lead + j + c0:lead + j + c0 + t_chunk, :]
        xc = cb_ref[...] + xc
        ra, ix = [], []
        for hd in range(wb // LANES):
            xh = xc[:, hd * LANES:(hd + 1) * LANES].astype(BF16)
            ra.append(_dot(xh, wa_ref[hd]))
            ix.append(_dot(xh, wx_ref[hd]))
        r = jax.nn.sigmoid(jnp.concatenate(ra, axis=-1) + ba_ref[...])
        gi = jax.nn.sigmoid(jnp.concatenate(ix, axis=-1) + bx_ref[...])
        log_a = r * neg_c_sp
        a = jnp.exp(log_a)
        a_s[c0:c0 + t_chunk, :] = a
        b_s[c0:c0 + t_chunk, :] = jnp.sqrt(-jnp.tanh(log_a) * (a * a + 1.0)) * (gi * xc)

    def step(t, h):
        h = a_s[pl.ds(t, 1), :] * h + b_s[pl.ds(t, 1), :]
        b_s[pl.ds(t, 1), :] = h
        return h

    lax.fori_loop(0, t_len, step, h0_ref[0], unroll=8)
    hl_ref[0] = b_s[t_valid - 1:t_valid, :]
    for c0 in range(0, t_len, t_chunk):
        rows = slice(c0, c0 + t_chunk)
        y_ref[rows, :] = (b_s[rows, :] * jax.nn.gelu(g_ref[rows, :])).astype(y_ref.dtype)


def _rglru(u_src, g_src, conv_prev, h0, conv_w, conv_b, wa, ba, wx, bx, lam, *, bsz, t_len, t_valid,
           u_col, g_col, width, wb):
    nb = width // wb
    heads_per = wb // LANES
    vec = lambda a: a.reshape(1, width)
    vspec = pl.BlockSpec((1, wb), lambda b, c: (0, c))
    t_chunk = min(t_len, 256)
    y, h_last = pl.pallas_call(
        functools.partial(_rglru_kernel, t_valid=t_valid, t_chunk=t_chunk),
        out_shape=(jax.ShapeDtypeStruct((bsz * t_len, width), BF16),
                   jax.ShapeDtypeStruct((bsz, 1, width), F32)),
        grid=(bsz, nb),
        in_specs=[pl.BlockSpec((t_len, wb), lambda b, c: (b, u_col // wb + c)),
                  pl.BlockSpec((t_len, wb), lambda b, c: (b, g_col // wb + c)),
                  pl.BlockSpec((1, CONV_W - 1, wb), lambda b, c: (b, 0, c)),
                  pl.BlockSpec((1, 1, wb), lambda b, c: (b, 0, c)),
                  pl.BlockSpec((CONV_W, wb), lambda b, c: (0, c)),
                  vspec,
                  pl.BlockSpec((heads_per, LANES, LANES), lambda b, c: (c, 0, 0)),
                  vspec,
                  pl.BlockSpec((heads_per, LANES, LANES), lambda b, c: (c, 0, 0)),
                  vspec, vspec],
        out_specs=(pl.BlockSpec((t_len, wb), lambda b, c: (b, c)),
                   pl.BlockSpec((1, 1, wb), lambda b, c: (b, 0, c))),
        scratch_shapes=[pltpu.VMEM((t_len + SUBLANES, wb), F32), pltpu.VMEM((t_len, wb), F32),
                        pltpu.VMEM((t_len, wb), F32)],
        compiler_params=_params("parallel", "arbitrary"),
        name="rglru",
    )(u_src, g_src, conv_prev, h0.reshape(bsz, 1, width), conv_w, vec(conv_b), wa.astype(BF16), vec(ba),
      wx.astype(BF16), vec(bx), vec(lam))
    return y, h_last.reshape(bsz, width)


def _s5_disc_kernel(are_ref, aim_ref, ls_ref, abr_ref, abi_ref, cfr_ref, cfi_ref):
    a_re, a_im = are_ref[...], aim_ref[...]
    dt = jnp.exp(ls_ref[...])
    mag = jnp.exp(a_re * dt)
    ab_re = mag * jnp.cos(a_im * dt)
    ab_im = mag * jnp.sin(a_im * dt)
    den = a_re * a_re + a_im * a_im
    nr, ni = ab_re - 1.0, ab_im
    abr_ref[...] = ab_re
    abi_ref[...] = ab_im
    cfr_ref[...] = (nr * a_re + ni * a_im) / den
    cfi_ref[...] = (ni * a_re - nr * a_im) / den


def _s5_discretize(a_re, a_im, log_step):
    g, n = a_re.shape
    shp = jax.ShapeDtypeStruct((g, n), F32)
    return pl.pallas_call(_s5_disc_kernel, out_shape=(shp, shp, shp, shp), name="s5_discretize")(
        a_re, a_im, jnp.broadcast_to(log_step[:, None], (g, n)))


def _s5_kernel(u_ref, wb_ref, wc_ref, are_ref, aim_ref, d_ref, h0r_ref, h0i_ref, z_ref, hr_ref, hi_ref, x_s, st_s,
               *, nj, t_chunk):
    c = pl.program_id(1)
    ns = are_ref.shape[2]

    @pl.when(c == 0)
    def _():
        st_s[0] = h0r_ref[0]
        st_s[1] = h0i_ref[0]

    u = u_ref[0]
    rowj = (lax.broadcasted_iota(I32, u.shape, 0) % SUBLANES) // (SUBLANES // nj)
    u_bf = u.astype(BF16)
    if nj > 1:
        zero = jnp.zeros(u_bf.shape, BF16)
        u_bf = jnp.concatenate([jnp.where(rowj == j, u_bf, zero) for j in range(nj)], axis=1)
    x_s[...] = _dot(u_bf, wb_ref[0])
    ar, ai = are_ref[0], aim_ref[0]

    def step(t, carry):
        xr, xi = carry
        r0 = pl.multiple_of(t * SUBLANES, SUBLANES)
        nr = ar * xr - ai * xi + x_s[pl.ds(r0, SUBLANES), 0:ns]
        ni = ar * xi + ai * xr + x_s[pl.ds(r0, SUBLANES), ns:2 * ns]
        x_s[pl.ds(r0, SUBLANES), 0:ns] = nr
        x_s[pl.ds(r0, SUBLANES), ns:2 * ns] = ni
        return nr, ni

    xr, xi = lax.fori_loop(0, t_chunk, step, (st_s[0], st_s[1]), unroll=min(t_chunk, 4))
    st_s[0] = xr
    st_s[1] = xi
    yb = _dot(x_s[...].astype(BF16), wc_ref[0])
    y = yb[:, 0:LANES]
    dvec = jnp.broadcast_to(d_ref[0, 0:1, :], y.shape)
    for j in range(1, nj):
        y = jnp.where(rowj == j, yb[:, j * LANES:(j + 1) * LANES], y)
        dvec = jnp.where(rowj == j, d_ref[0, j:j + 1, :], dvec)
    z_ref[0] = jax.nn.gelu(y + dvec * u).astype(z_ref.dtype)

    @pl.when(c == pl.num_programs(1) - 1)
    def _():
        hr_ref[0] = xr
        hi_ref[0] = xi


def _s5_scan(u, h0_re, h0_im, ab_re, ab_im, bb_re, bb_im, c_re, c_im, d_skip):
    bsz, t_len, d = u.shape
    groups, n_state = ab_re.shape
    gl = LANES // S5_GROUP
    ns = gl * n_state
    nj = SUBLANES // bsz
    nq = d // (LANES * nj)
    t_chunk = min(t_len, 256)
    eye_g = jnp.eye(gl, dtype=F32)

    u_rows = u.reshape(bsz, t_len, nq, nj, LANES).transpose(2, 1, 3, 0, 4).reshape(nq, t_len * SUBLANES, LANES)

    def b_weight(bb):
        w = bb.reshape(nq, nj, gl, n_state, S5_GROUP).transpose(0, 1, 2, 4, 3)
        w = w[:, :, :, :, None, :] * eye_g[None, None, :, None, :, None]
        return w.reshape(nq, nj * LANES, ns)

    w_b = jnp.concatenate([b_weight(bb_re), b_weight(bb_im)], axis=-1).astype(BF16)

    def c_weight(cc):
        w = cc.reshape(nq, nj, gl, S5_GROUP, n_state).transpose(0, 2, 4, 1, 3)
        w = w[:, :, :, :, None, :] * eye_g[None, :, None, None, :, None]
        return w.reshape(nq, ns, nj * LANES)

    w_c = jnp.concatenate([c_weight(c_re), -c_weight(c_im)], axis=1).astype(BF16)

    def coef(a):
        a = a.reshape(nq, nj, 1, ns)
        return jnp.broadcast_to(a, (nq, nj, bsz, ns)).reshape(nq, SUBLANES, ns)

    def state_in(h):
        return h.reshape(bsz, nq, nj, ns).transpose(1, 2, 0, 3).reshape(nq, SUBLANES, ns)

    def state_out(h):
        return h.reshape(nq, nj, bsz, ns).transpose(2, 0, 1, 3).reshape(bsz, groups, n_state)

    rows = t_chunk * SUBLANES
    per_q = lambda q, c: (q, 0, 0)
    st_spec = pl.BlockSpec((1, SUBLANES, ns), per_q)
    st_shape = jax.ShapeDtypeStruct((nq, SUBLANES, ns), F32)
    z, hr, hi = pl.pallas_call(
        functools.partial(_s5_kernel, nj=nj, t_chunk=t_chunk),
        out_shape=(jax.ShapeDtypeStruct((nq, t_len * SUBLANES, LANES), BF16), st_shape, st_shape),
        grid=(nq, t_len // t_chunk),
        in_specs=[pl.BlockSpec((1, rows, LANES), lambda q, c: (q, c, 0)),
                  pl.BlockSpec((1, nj * LANES, 2 * ns), per_q),
                  pl.BlockSpec((1, 2 * ns, nj * LANES), per_q),
                  st_spec, st_spec,
                  pl.BlockSpec((1, nj, LANES), per_q),
                  st_spec, st_spec],
        out_specs=(pl.BlockSpec((1, rows, LANES), lambda q, c: (q, c, 0)), st_spec, st_spec),
        scratch_shapes=[pltpu.VMEM((rows, 2 * ns), F32), pltpu.VMEM((2, SUBLANES, ns), F32)],
        compiler_params=_params("parallel", "arbitrary"),
        name="s5_scan",
    )(u_rows, w_b, w_c, coef(ab_re), coef(ab_im), d_skip.reshape(nq, nj, LANES), state_in(h0_re), state_in(h0_im))
    z = z.reshape(nq, t_len, nj, bsz, LANES).transpose(3, 1, 0, 2, 4).reshape(bsz, t_len, d)
    return z, state_out(hr), state_out(hi)


MOE_TM = 256
COMBINE_ROWS = 32
DISPATCH_ROWS = 128


def _moe_plan(ids, ranks, counts, n_tokens, n_experts):
    rows = ids.shape[0]
    n_tiles = (n_tokens * TOP_K) // MOE_TM + n_experts
    spare_tiles = -(-(rows - n_tokens) * TOP_K // MOE_TM)
    tiles_per = (counts + MOE_TM - 1) // MOE_TM
    tile_end = jnp.cumsum(tiles_per)
    run_start = (tile_end - tiles_per) * MOE_TM
    chose = ids[..., None] == jnp.arange(n_experts, dtype=I32)
    dest = jnp.sum(jnp.where(chose, run_start, 0), axis=-1) + ranks
    r = jnp.arange(rows, dtype=I32)[:, None]
    spare = n_tiles * MOE_TM + (r - n_tokens) * TOP_K + jnp.arange(TOP_K, dtype=I32)[None, :]
    is_token = r < n_tokens
    dest_dispatch = jnp.where(is_token, dest, spare).reshape(-1)
    dest_combine = jnp.where(is_token, dest, 0).reshape(-1)
    n_valid = tile_end[-1]
    tile_ids = jnp.arange(n_tiles + spare_tiles, dtype=I32)
    tile_expert = jnp.sum(tile_end[None, :] <= jnp.minimum(tile_ids[:n_tiles], n_valid - 1)[:, None], axis=1)
    has_tiles = tiles_per > 0
    is_last = jnp.any(has_tiles[None, :] & (tile_end[None, :] - 1 == tile_ids[:, None]), axis=1)
    zero_fill = (is_last | (tile_ids >= n_valid)).astype(I32)
    e_ids = jnp.arange(n_experts, dtype=I32)
    later = jnp.min(jnp.where(has_tiles[None, :] & (e_ids[None, :] > e_ids[:, None]), e_ids[None, :], n_experts),
                    axis=1)
    first = jnp.min(jnp.where(has_tiles, e_ids, n_experts))
    wraps = later == n_experts
    next_of = jnp.where(wraps, first, later)
    mine = tile_expert[:, None] == e_ids[None, :]
    next_expert = jnp.sum(jnp.where(mine, next_of[None, :], 0), axis=1)
    next_wraps = jnp.sum(jnp.where(mine, wraps[None, :].astype(I32), 0), axis=1)
    runs = (tile_expert.astype(I32), n_valid.astype(I32).reshape(1), next_expert.astype(I32),
            next_wraps.astype(I32))
    return dest_dispatch, dest_combine, runs, zero_fill, n_tiles, n_tiles + spare_tiles


def _dispatch_kernel(dest_ref, zf_ref, x_ref, o_hbm, zbuf, sem, zsem, *, total_tiles):
    t = pl.program_id(0)
    tr = x_ref.shape[0]

    def zero_copy(tile):
        start = pl.multiple_of(tile * MOE_TM, MOE_TM)
        return pltpu.make_async_copy(zbuf, o_hbm.at[pl.ds(start, MOE_TM), :], zsem)

    @pl.when(t == 0)
    def _():
        zbuf[...] = jnp.zeros(zbuf.shape, zbuf.dtype)

        def start(tile, c):
            @pl.when(zf_ref[tile] != 0)
            def _():
                zero_copy(tile).start()
            return c

        def wait(tile, c):
            @pl.when(zf_ref[tile] != 0)
            def _():
                zero_copy(tile).wait()
            return c

        lax.fori_loop(0, total_tiles, start, 0)
        lax.fori_loop(0, total_tiles, wait, 0)

    def body(r, c):
        for k in range(TOP_K):
            row = dest_ref[(t * tr + r) * TOP_K + k]
            pltpu.make_async_copy(x_ref.at[pl.ds(r, 1), :], o_hbm.at[pl.ds(row, 1), :], sem).start()
        return c

    lax.fori_loop(0, tr, body, 0)
    for k in range(TOP_K):
        pltpu.make_async_copy(x_ref, o_hbm.at[pl.ds(0, tr), :], sem).wait()


def _dispatch(x, dest, zero_fill, total_tiles):
    m, d = x.shape
    return pl.pallas_call(
        functools.partial(_dispatch_kernel, total_tiles=total_tiles),
        out_shape=jax.ShapeDtypeStruct((total_tiles * MOE_TM, d), x.dtype),
        grid_spec=pltpu.PrefetchScalarGridSpec(
            num_scalar_prefetch=2,
            grid=(m // DISPATCH_ROWS,),
            in_specs=[pl.BlockSpec((DISPATCH_ROWS, d), lambda t, dst, zf: (t, 0))],
            out_specs=pl.BlockSpec(memory_space=pl.ANY),
            scratch_shapes=[pltpu.VMEM((MOE_TM, d), x.dtype), pltpu.SemaphoreType.DMA(()),
                            pltpu.SemaphoreType.DMA(())]),
        compiler_params=_params("arbitrary"),
        name="moe_dispatch",
    )(dest, zero_fill, x)


def _new_expert(te_ref, t):
    return (t == 0) | (te_ref[t] != te_ref[jnp.maximum(t - 1, 0)])


def _run_weights(te_ref, nx_ref, wrap_ref, copies, n_col_blocks, cast):
    j = pl.program_id(0)
    t = pl.program_id(1)

    @pl.when((j == 0) & (t == 0))
    def _():
        for c in copies(te_ref[t], j):
            c.start()

    for c in copies(te_ref[t], j):
        c.wait()
    cast()
    next_j = j + wrap_ref[t]

    @pl.when(next_j < n_col_blocks)
    def _():
        for c in copies(nx_ref[t], next_j):
            c.start()


def _moe_gate_up_kernel(te_ref, nv_ref, nx_ref, wrap_ref, x_ref, w_hbm, bg_ref, bu_ref, a_ref, stage, wg_bf, wu_bf,
                        sem, *, k_chunk, layer, nb1):
    t = pl.program_id(1)
    tn1 = a_ref.shape[1]

    @pl.when(t < nv_ref[0])
    def _():
        d = x_ref.shape[1]

        def copies(e, jj):
            return [pltpu.make_async_copy(
                w_hbm.at[layer, e, :, pl.ds(pl.multiple_of((half * nb1 + jj) * tn1, tn1), tn1)],
                stage.at[half], sem.at[half]) for half in range(2)]

        def cast():
            for k0 in range(0, d, k_chunk):
                wg_bf[k0:k0 + k_chunk, :] = stage[0, k0:k0 + k_chunk, :].astype(BF16)
                wu_bf[k0:k0 + k_chunk, :] = stage[1, k0:k0 + k_chunk, :].astype(BF16)

        @pl.when(_new_expert(te_ref, t))
        def _():
            _run_weights(te_ref, nx_ref, wrap_ref, copies, nb1, cast)

        hg = jnp.zeros(a_ref.shape, F32)
        hu = jnp.zeros(a_ref.shape, F32)
        for k0 in range(0, d, k_chunk):
            xk = x_ref[:, k0:k0 + k_chunk].astype(BF16)
            hg = hg + _dot(xk, wg_bf[k0:k0 + k_chunk, :])
            hu = hu + _dot(xk, wu_bf[k0:k0 + k_chunk, :])
        gate = jnp.minimum(hg + bg_ref[0, 0], SWIGLU_LIMIT)
        up = jnp.clip(hu + bu_ref[0, 0], -SWIGLU_LIMIT, SWIGLU_LIMIT)
        a_ref[...] = ((up + 1.0) * gate * jax.nn.sigmoid(SWIGLU_ALPHA * gate)).astype(a_ref.dtype)

    @pl.when(t >= nv_ref[0])
    def _():
        a_ref[...] = jnp.zeros(a_ref.shape, a_ref.dtype)


def _moe_down_kernel(te_ref, nv_ref, nx_ref, wrap_ref, a_ref, w_hbm, b_ref, y_ref, stage, w_bf, sem,
                     *, k_chunk, layer, nb2):
    t = pl.program_id(1)
    tn2 = y_ref.shape[1]

    @pl.when(t < nv_ref[0])
    def _():
        f = a_ref.shape[1]

        def copies(e, jj):
            return [pltpu.make_async_copy(w_hbm.at[layer, e, :, pl.ds(pl.multiple_of(jj * tn2, tn2), tn2)],
                                          stage, sem)]

        def cast():
            for k0 in range(0, f, k_chunk):
                w_bf[k0:k0 + k_chunk, :] = stage[k0:k0 + k_chunk, :].astype(BF16)

        @pl.when(_new_expert(te_ref, t))
        def _():
            _run_weights(te_ref, nx_ref, wrap_ref, copies, nb2, cast)

        y_ref[...] = _dot(a_ref[...], w_bf[...]) + b_ref[0, 0]

    @pl.when(t >= nv_ref[0])
    def _():
        y_ref[...] = jnp.zeros(y_ref.shape, y_ref.dtype)


def _moe_experts(xs, runs, w_gu, b_gu, w_dn, b_dn, layer, n_tiles):
    _, n_experts, d, two_f = w_gu.shape
    f = two_f // 2
    tn1 = min(512, f)
    nb1 = f // tn1
    n_slots = n_tiles * MOE_TM
    b_gu4 = b_gu.reshape(b_gu.shape[0], n_experts, 1, two_f)
    last = lambda t, nv: jnp.minimum(t, nv[0] - 1)
    act = pl.pallas_call(
        functools.partial(_moe_gate_up_kernel, k_chunk=min(1024, d), layer=layer, nb1=nb1),
        out_shape=jax.ShapeDtypeStruct((n_slots, f), BF16),
        grid_spec=pltpu.PrefetchScalarGridSpec(
            num_scalar_prefetch=4,
            grid=(nb1, n_tiles),
            in_specs=[pl.BlockSpec((MOE_TM, d), lambda j, t, te, nv, nx, wr: (last(t, nv), 0)),
                      pl.BlockSpec(memory_space=pl.ANY),
                      pl.BlockSpec((1, 1, 1, tn1), lambda j, t, te, nv, nx, wr: (layer, te[t], 0, j)),
                      pl.BlockSpec((1, 1, 1, tn1), lambda j, t, te, nv, nx, wr: (layer, te[t], 0, nb1 + j))],
            out_specs=pl.BlockSpec((MOE_TM, tn1), lambda j, t, te, nv, nx, wr: (t, j)),
            scratch_shapes=[pltpu.VMEM((2, d, tn1), F32), pltpu.VMEM((d, tn1), BF16), pltpu.VMEM((d, tn1), BF16),
                            pltpu.SemaphoreType.DMA((2,))]),
        compiler_params=_params("arbitrary", "arbitrary"),
        name="moe_gate_up",
    )(*runs, xs, w_gu, b_gu4, b_gu4)

    tn2 = min(2048, d)
    nb2 = d // tn2
    return pl.pallas_call(
        functools.partial(_moe_down_kernel, k_chunk=min(512, f), layer=layer, nb2=nb2),
        out_shape=jax.ShapeDtypeStruct((n_slots, d), F32),
        grid_spec=pltpu.PrefetchScalarGridSpec(
            num_scalar_prefetch=4,
            grid=(nb2, n_tiles),
            in_specs=[pl.BlockSpec((MOE_TM, f), lambda j, t, te, nv, nx, wr: (last(t, nv), 0)),
                      pl.BlockSpec(memory_space=pl.ANY),
                      pl.BlockSpec((1, 1, 1, tn2), lambda j, t, te, nv, nx, wr: (layer, te[t], 0, j))],
            out_specs=pl.BlockSpec((MOE_TM, tn2), lambda j, t, te, nv, nx, wr: (t, j)),
            scratch_shapes=[pltpu.VMEM((f, tn2), F32), pltpu.VMEM((f, tn2), BF16), pltpu.SemaphoreType.DMA(())]),
        compiler_params=_params("arbitrary", "arbitrary"),
        name="moe_down",
    )(*runs, act, w_dn, b_dn.reshape(b_dn.shape[0], n_experts, 1, d))


def _combine_ln_kernel(dest_ref, ys_hbm, gate_ref, x_ref, g_ref, b_ref, o_ref, ob_ref, buf, sem, *, alpha):
    t = pl.program_id(0)
    nt = pl.num_programs(0)
    tr = o_ref.shape[0]

    def issue(tile, slot):
        def body(r, carry):
            for k in range(TOP_K):
                row = dest_ref[(tile * tr + r) * TOP_K + k]
                pltpu.make_async_copy(ys_hbm.at[pl.ds(row, 1), :], buf.at[slot, k, pl.ds(r, 1), :],
                                      sem.at[slot]).start()
            return carry
        lax.fori_loop(0, tr, body, 0)

    @pl.when(t == 0)
    def _():
        issue(0, 0)

    @pl.when(t + 1 < nt)
    def _():
        issue(t + 1, (t + 1) % 2)

    slot = t % 2
    for k in range(TOP_K):
        pltpu.make_async_copy(ys_hbm.at[pl.ds(0, tr), :], buf.at[slot, k], sem.at[slot]).wait()
    gates = gate_ref[...]
    f = gates[:, 0:1] * buf[slot, 0]
    for k in range(1, TOP_K):
        f = f + gates[:, k:k + 1] * buf[slot, k]
    y = _ln_rows(alpha * x_ref[...] + f, g_ref[...], b_ref[...])
    o_ref[...] = y
    ob_ref[...] = y.astype(ob_ref.dtype)


def _combine_ln(ys, dest, gates, x1, g, b, *, alpha):
    m, d = x1.shape
    row = lambda t, dst: (t, 0)
    fixed = lambda t, dst: (0, 0)
    return pl.pallas_call(
        functools.partial(_combine_ln_kernel, alpha=alpha),
        out_shape=(jax.ShapeDtypeStruct((m, d), F32), jax.ShapeDtypeStruct((m, d), BF16)),
        grid_spec=pltpu.PrefetchScalarGridSpec(
            num_scalar_prefetch=1,
            grid=(m // COMBINE_ROWS,),
            in_specs=[pl.BlockSpec(memory_space=pl.ANY),
                      pl.BlockSpec((COMBINE_ROWS, LANES), row),
                      pl.BlockSpec((COMBINE_ROWS, d), row),
                      pl.BlockSpec((1, d), fixed), pl.BlockSpec((1, d), fixed)],
            out_specs=(pl.BlockSpec((COMBINE_ROWS, d), row), pl.BlockSpec((COMBINE_ROWS, d), row)),
            scratch_shapes=[pltpu.VMEM((2, TOP_K, COMBINE_ROWS, d), F32), pltpu.SemaphoreType.DMA((2,))]),
        compiler_params=_params("arbitrary"),
        name="moe_combine_ln",
    )(dest, ys, gates, x1, g.reshape(1, d), b.reshape(1, d))


def _moe_block(x, mix, n_tokens, layer, ln1_g, ln1_b, ln2_g, ln2_b, rw, rb, w_gu, b_gu, w_dn, b_dn, *, alpha):
    n_experts = rw.shape[1]
    x1, ids, gates, ranks, counts = _ln_router(x, mix, ln1_g, ln1_b, rw, rb, alpha=alpha, tr=256, n_tokens=n_tokens)
    dest_d, dest_c, runs, zero_fill, n_tiles, total_tiles = _moe_plan(
        ids[:, :TOP_K], ranks[:, :TOP_K], counts[0, :n_experts].astype(I32), n_tokens, n_experts)
    xs = _dispatch(x1, dest_d, zero_fill, total_tiles)
    ys = _moe_experts(xs, runs, w_gu, b_gu, w_dn, b_dn, layer, n_tiles)
    return _combine_ln(ys, dest_c, gates, x1, ln2_g, ln2_b, alpha=alpha)


ROW_TILE = 768


def kernel(x_prompt, x_sample, cache_k, cache_v, cache_kidx, state_conv, state_lru, state_s5_re, state_s5_im,
           page_table, w_in, w_idx_qb, kidx_ln_g, kidx_ln_b, conv_w, conv_b, lru_wa, lru_ba, lru_wx, lru_bx,
           lru_lambda, w_out, s5_a_re, s5_a_im, s5_log_step, s5_b_re, s5_b_im, s5_c_re, s5_c_im, s5_d,
           glu_w1, glu_b1, glu_w2, glu_b2, ln1_g, ln1_b, ln2_g, ln2_b, router_w, router_b, moe_w_gu, moe_b_gu,
           moe_w_down, moe_b_down):
    bp, sp, dm = x_prompt.shape
    bs, ss, _ = x_sample.shape
    depth = ln1_g.shape[0]
    alpha = (2 * depth) ** 0.25
    n_p, n_s = bp * sp, bs * ss
    n_tok = n_p + n_s
    n_rows = -(-n_tok // ROW_TILE) * ROW_TILE
    lru_w = conv_w.shape[2]
    q_w = w_out.shape[1] - lru_w
    kv_w = N_KV_HEADS * HEAD_DIM
    s5_groups = dm // S5_GROUP

    def unify(a_p, a_s):
        pad = jnp.zeros((n_rows - n_tok, a_p.shape[-1]), a_p.dtype)
        return jnp.concatenate([a_p.reshape(n_p, -1), a_s.reshape(n_s, -1), pad], axis=0)

    x = unify(x_prompt, x_sample)
    xb = x.astype(BF16)
    outs = {name: [] for name in ("k_p", "k_s", "v_p", "v_s", "i_p", "i_s", "c_p", "c_s", "h_p", "h_s",
                                   "r_p", "r_s", "m_p", "m_s")}
    for layer in range(depth):
        j = layer // 2
        if layer % 2 == 0:
            sizes = [q_w, kv_w, kv_w, IDX_Q_RANK, IDX_HEADS, IDX_DIM, lru_w, lru_w]
            cuts = [0]
            for s_ in sizes:
                cuts.append(cuts[-1] + s_)
            seg = lambda i: w_in[j][:, cuts[i]:cuts[i + 1]]
            lay = {"q": 0, "q_w": q_w, "u": q_w, "g": q_w + lru_w, "k": q_w + 2 * lru_w,
                   "v": q_w + 2 * lru_w + kv_w, "ki": q_w + 2 * lru_w + 2 * kv_w}
            lay["wi"] = lay["ki"] + IDX_DIM
            n_used = lay["wi"] + IDX_HEADS
            n_proj = -(-n_used // 512) * 512
            w_main = jnp.concatenate([seg(0), seg(6), seg(7), seg(1), seg(2), seg(5), seg(4),
                                      jnp.zeros((dm, n_proj - n_used), F32)], axis=1).astype(BF16)
            proj = _matmul(xb, w_main, tm=ROW_TILE, tn=512, name="in_proj")
            cq = _matmul(xb, seg(3).astype(BF16), tm=ROW_TILE, tn=IDX_Q_RANK // 2, out_dtype=BF16, name="cq_proj")
            q_idx = _matmul(cq, w_idx_qb[j].astype(BF16), tm=ROW_TILE, tn=1024, out_dtype=BF16, name="q_idx_proj")
            k_idx = _layer_norm_cols(proj, kidx_ln_g[j], kidx_ln_b[j], tr=ROW_TILE, col_block=lay["ki"] // IDX_DIM,
                                     width=IDX_DIM)
            lru_args = (conv_w[j], conv_b[j], lru_wa[j], lru_ba[j], lru_wx[j], lru_bx[j], lru_lambda[j])

            att_p = _dsa_prompt(proj, q_idx, k_idx, lay, bsz=bp, seq=sp, tq=128)
            rec_p, h_p = _rglru(proj, proj, jnp.zeros((bp, CONV_W - 1, lru_w), F32), jnp.zeros((bp, lru_w), F32),
                                *lru_args, bsz=bp, t_len=sp, t_valid=sp, u_col=lay["u"], g_col=lay["g"],
                                width=lru_w, wb=512)
            col = lambda a, name, width: a[:, lay[name]:lay[name] + width]
            proj_p = proj[:n_p]
            outs["k_p"].append(col(proj_p, "k", kv_w).reshape(bp, sp, N_KV_HEADS, HEAD_DIM))
            outs["v_p"].append(col(proj_p, "v", kv_w).reshape(bp, sp, N_KV_HEADS, HEAD_DIM))
            outs["i_p"].append(k_idx[:n_p].reshape(bp, sp, IDX_DIM))
            tail = CONV_W - 1
            outs["c_p"].append(jnp.stack([proj[(b + 1) * sp - tail:(b + 1) * sp, lay["u"]:lay["u"] + lru_w]
                                          for b in range(bp)]))
            outs["h_p"].append(h_p)

            proj_s = proj[n_p:n_tok]
            s3 = lambda a: a.reshape(bs, ss, -1)
            q_s, k_s, v_s = s3(col(proj_s, "q", q_w)), s3(col(proj_s, "k", kv_w)), s3(col(proj_s, "v", kv_w))
            u_s, g_s = s3(col(proj_s, "u", lru_w)), s3(col(proj_s, "g", lru_w))
            wi_s = s3(col(proj_s, "wi", IDX_HEADS))
            kidx_s = s3(k_idx[n_p:n_tok])
            att_s = _dsa_sample(q_s, k_s, v_s, s3(q_idx[n_p:n_tok]), wi_s, kidx_s, cache_k, cache_v, cache_kidx,
                                page_table, j)
            t_pad = -(-ss // SUBLANES) * SUBLANES
            pad_t = lambda a: jnp.pad(a, ((0, 0), (0, t_pad - ss), (0, 0))).reshape(bs * t_pad, -1)
            rec_s, h_s = _rglru(pad_t(u_s), pad_t(g_s), state_conv[j], state_lru[j], *lru_args, bsz=bs, t_len=t_pad,
                                t_valid=ss, u_col=0, g_col=0, width=lru_w, wb=512)
            rec_s = rec_s.reshape(bs, t_pad, lru_w)[:, :ss]
            outs["k_s"].append(k_s.reshape(bs, ss, N_KV_HEADS, HEAD_DIM))
            outs["v_s"].append(v_s.reshape(bs, ss, N_KV_HEADS, HEAD_DIM))
            outs["i_s"].append(kidx_s)
            up_s = jnp.concatenate([state_conv[j], u_s], axis=1)
            outs["c_s"].append(up_s[:, ss:])
            outs["h_s"].append(h_s)

            att = unify(att_p, att_s.astype(BF16))
            rec = unify(rec_p, rec_s)
            mix = _matmul_2in(att, rec, w_out[j].astype(BF16), tm=ROW_TILE, tn=512, name="out_proj")
        else:
            ab_re, ab_im, cf_re, cf_im = _s5_discretize(s5_a_re[j], s5_a_im[j], s5_log_step[j])
            b_re, b_im = s5_b_re[j], s5_b_im[j]
            bb_re = cf_re[..., None] * b_re - cf_im[..., None] * b_im
            bb_im = cf_re[..., None] * b_im + cf_im[..., None] * b_re
            s5_args = (ab_re, ab_im, bb_re, bb_im, s5_c_re[j], s5_c_im[j], s5_d[j])
            n_state = ab_re.shape[1]
            zeros = jnp.zeros((bp, s5_groups, n_state), F32)
            z_p, hr, hi = _s5_scan(x[:n_p].reshape(bp, sp, dm), zeros, zeros, *s5_args)
            outs["r_p"].append(hr)
            outs["m_p"].append(hi)
            z_s, hr, hi = _s5_scan(x[n_p:n_tok].reshape(bs, ss, dm), state_s5_re[j], state_s5_im[j], *s5_args)
            outs["r_s"].append(hr)
            outs["m_s"].append(hi)
            mix = _glu(unify(z_p, z_s), glu_w1[j].astype(BF16), glu_b1[j], glu_w2[j].astype(BF16), glu_b2[j],
                       tm=ROW_TILE, tn=512)
        x, xb = _moe_block(x, mix, n_tok, layer, ln1_g[layer], ln1_b[layer], ln2_g[layer], ln2_b[layer],
                           router_w[layer], router_b[layer], moe_w_gu, moe_b_gu, moe_w_down, moe_b_down,
                           alpha=alpha)
    st = lambda name: jnp.stack(outs[name])
    return (x[:n_p].reshape(bp, sp, dm), x[n_p:n_tok].reshape(bs, ss, dm),
            st("k_p"), st("k_s"), st("v_p"), st("v_s"), st("i_p"), st("i_s"), st("c_p"), st("c_s"),
            st("h_p"), st("h_s"), st("r_p"), st("r_s"), st("m_p"), st("m_s"))
```

```python
import functools
import math

import jax
import jax.numpy as jnp
from jax import lax
from jax.experimental import pallas as pl
from jax.experimental.pallas import tpu as pltpu

F32 = jnp.float32
BF16 = jnp.bfloat16
I32 = jnp.int32

HEAD_DIM = 128
N_KV_HEADS = 4
IDX_HEADS = 32
IDX_DIM = 128
IDX_Q_RANK = 768
TOPK_MAX = 256
LRU_HEADS = 16
CONV_W = 4
LRU_C = 8.0
S5_GROUP = 16
TOP_K = 4
SWIGLU_LIMIT = 7.0
SWIGLU_ALPHA = 1.702
LN_EPS = 1e-5

LANES = 128
SUBLANES = 8
VMEM_LIMIT_BYTES = 56 * 1024 * 1024

NEG_INF = float("-inf")
INT_MIN = -(2 ** 31)


def _params(*sem):
    return pltpu.CompilerParams(dimension_semantics=sem, vmem_limit_bytes=VMEM_LIMIT_BYTES)


def _nt_dot(a, b):
    return lax.dot_general(a, b, (((1,), (1,)), ((), ())), preferred_element_type=F32)


def _dot(a, b):
    return jnp.dot(a, b, preferred_element_type=F32)


def _mm_kernel(x_ref, w_ref, o_ref):
    o_ref[...] = _dot(x_ref[...].astype(BF16), w_ref[...]).astype(o_ref.dtype)


def _matmul(x, w, *, tm, tn, out_dtype=F32, name="matmul"):
    m, k = x.shape
    n = w.shape[1]
    return pl.pallas_call(
        _mm_kernel,
        out_shape=jax.ShapeDtypeStruct((m, n), out_dtype),
        grid=(m // tm, n // tn),
        in_specs=[pl.BlockSpec((tm, k), lambda i, j: (i, 0)),
                  pl.BlockSpec((k, tn), lambda i, j: (0, j))],
        out_specs=pl.BlockSpec((tm, tn), lambda i, j: (i, j)),
        compiler_params=_params("parallel", "arbitrary"),
        name=name,
    )(x, w)


def _mm2_kernel(xa_ref, xb_ref, wa_ref, wb_ref, o_ref):
    o_ref[...] = _dot(xa_ref[...], wa_ref[...]) + _dot(xb_ref[...], wb_ref[...])


def _matmul_2in(xa, xb, w, *, tm, tn, name):
    m, ka = xa.shape
    n = w.shape[1]
    return pl.pallas_call(
        _mm2_kernel,
        out_shape=jax.ShapeDtypeStruct((m, n), F32),
        grid=(m // tm, n // tn),
        in_specs=[pl.BlockSpec((tm, ka), lambda i, j: (i, 0)),
                  pl.BlockSpec((tm, ka), lambda i, j: (i, 0)),
                  pl.BlockSpec((ka, tn), lambda i, j: (0, j)),
                  pl.BlockSpec((ka, tn), lambda i, j: (1, j))],
        out_specs=pl.BlockSpec((tm, tn), lambda i, j: (i, j)),
        compiler_params=_params("parallel", "arbitrary"),
        name=name,
    )(xa, xb, w, w)


def _glu_kernel(z_ref, w1_ref, b1_ref, w2_ref, b2_ref, o_ref):
    z = z_ref[...]
    a = _dot(z, w1_ref[...]) + b1_ref[...]
    b = _dot(z, w2_ref[...]) + b2_ref[...]
    o_ref[...] = a * jax.nn.sigmoid(b)


def _glu(z, w1, b1, w2, b2, *, tm, tn):
    m, k = z.shape
    n = w1.shape[1]
    x_spec = pl.BlockSpec((tm, k), lambda i, j: (i, 0))
    w_spec = pl.BlockSpec((k, tn), lambda i, j: (0, j))
    b_spec = pl.BlockSpec((1, tn), lambda i, j: (0, j))
    return pl.pallas_call(
        _glu_kernel,
        out_shape=jax.ShapeDtypeStruct((m, n), F32),
        grid=(m // tm, n // tn),
        in_specs=[x_spec, w_spec, b_spec, w_spec, b_spec],
        out_specs=pl.BlockSpec((tm, tn), lambda i, j: (i, j)),
        compiler_params=_params("parallel", "arbitrary"),
        name="glu",
    )(z, w1, b1.reshape(1, n), w2, b2.reshape(1, n))


def _ln_rows(x, g, b):
    mu = jnp.mean(x, axis=-1, keepdims=True)
    xc = x - mu
    var = jnp.mean(xc * xc, axis=-1, keepdims=True)
    return xc * lax.rsqrt(var + LN_EPS) * g + b


def _ln_kernel(x_ref, g_ref, b_ref, o_ref):
    o_ref[...] = _ln_rows(x_ref[...], g_ref[...], b_ref[...])


def _layer_norm_cols(x, g, b, *, tr, col_block, width):
    m = x.shape[0]
    return pl.pallas_call(
        _ln_kernel,
        out_shape=jax.ShapeDtypeStruct((m, width), F32),
        grid=(m // tr,),
        in_specs=[pl.BlockSpec((tr, width), lambda i: (i, col_block)),
                  pl.BlockSpec((1, width), lambda i: (0, 0)),
                  pl.BlockSpec((1, width), lambda i: (0, 0))],
        out_specs=pl.BlockSpec((tr, width), lambda i: (i, 0)),
        compiler_params=_params("parallel"),
        name="layer_norm",
    )(x, g.reshape(1, width), b.reshape(1, width))


def _top_gates(logits, n_experts):
    lane_f = lax.broadcasted_iota(I32, logits.shape, 1).astype(F32)
    l = jnp.where(lane_f < float(n_experts), logits, NEG_INF)
    vals, ids = [], []
    for _ in range(TOP_K):
        m = jnp.max(l, axis=-1, keepdims=True)
        idx = jnp.min(jnp.where(l == m, lane_f, float(LANES)), axis=-1, keepdims=True)
        vals.append(m)
        ids.append(idx)
        l = jnp.where(lane_f == idx, NEG_INF, l)
    es = [jnp.exp(v - vals[0]) for v in vals]
    den = es[0]
    for e in es[1:]:
        den = den + e
    return ids, [e / den for e in es]


def _ln_router_kernel(x_ref, mix_ref, g_ref, b_ref, rw_ref, rb_ref, tri_ref, o_ref, id_ref, gate_ref, rank_ref,
                      cnt_ref, *, alpha, n_experts, n_tokens):
    i = pl.program_id(0)
    tr = x_ref.shape[0]
    y = _ln_rows(alpha * x_ref[...] + mix_ref[...], g_ref[...], b_ref[...])
    o_ref[...] = y
    logits = _dot(y.astype(BF16), rw_ref[...]) + rb_ref[...]
    ids, gates = _top_gates(logits, n_experts)

    lane = lax.broadcasted_iota(I32, logits.shape, 1)
    lane_f = lane.astype(F32)
    valid = (i * tr + lax.broadcasted_iota(I32, logits.shape, 0)) < n_tokens
    chosen = jnp.zeros(logits.shape, F32)
    for r in range(TOP_K):
        chosen = chosen + jnp.where(lane_f == ids[r], 1.0, 0.0)
    chosen = jnp.where(valid, chosen, 0.0)

    @pl.when(i == 0)
    def _():
        cnt_ref[...] = jnp.zeros(cnt_ref.shape, F32)

    before = _dot(tri_ref[...], chosen.astype(BF16)) + cnt_ref[...]
    cnt_ref[...] = cnt_ref[...] + jnp.sum(chosen, axis=0, keepdims=True)

    idv = jnp.zeros(logits.shape, I32)
    gv = jnp.zeros(logits.shape, F32)
    rv = jnp.zeros(logits.shape, I32)
    for r in range(TOP_K):
        rank = jnp.sum(jnp.where(lane_f == ids[r], before, 0.0), axis=-1, keepdims=True)
        idv = jnp.where(lane == r, ids[r].astype(I32), idv)
        gv = jnp.where(lane == r, gates[r], gv)
        rv = jnp.where(lane == r, rank.astype(I32), rv)
    id_ref[...] = idv
    gate_ref[...] = gv
    rank_ref[...] = rv


def _ln_router(x, mix, g, b, rw, rb, *, alpha, tr, n_tokens):
    m, d = x.shape
    n_experts = rw.shape[1]
    rw_p = jnp.zeros((d, LANES), BF16).at[:, :n_experts].set(rw.astype(BF16))
    rb_p = jnp.zeros((1, LANES), F32).at[0, :n_experts].set(rb)
    tri = (jnp.arange(tr)[None, :] < jnp.arange(tr)[:, None]).astype(BF16)
    row = lambda i: (i, 0)
    fixed = lambda i: (0, 0)
    lanes_i = jax.ShapeDtypeStruct((m, LANES), I32)
    return pl.pallas_call(
        functools.partial(_ln_router_kernel, alpha=alpha, n_experts=n_experts, n_tokens=n_tokens),
        out_shape=(jax.ShapeDtypeStruct((m, d), F32), lanes_i, jax.ShapeDtypeStruct((m, LANES), F32), lanes_i,
                   jax.ShapeDtypeStruct((1, LANES), F32)),
        grid=(m // tr,),
        in_specs=[pl.BlockSpec((tr, d), row), pl.BlockSpec((tr, d), row),
                  pl.BlockSpec((1, d), fixed), pl.BlockSpec((1, d), fixed),
                  pl.BlockSpec((d, LANES), fixed), pl.BlockSpec((1, LANES), fixed),
                  pl.BlockSpec((tr, tr), fixed)],
        out_specs=(pl.BlockSpec((tr, d), row), pl.BlockSpec((tr, LANES), row), pl.BlockSpec((tr, LANES), row),
                   pl.BlockSpec((tr, LANES), row), pl.BlockSpec((1, LANES), fixed)),
        compiler_params=_params("arbitrary"),
        name="ln_router",
    )(x, mix, g.reshape(1, d), b.reshape(1, d), rw_p, rb_p, tri)


def _sortable_key(x):
    bits = pltpu.bitcast(x, I32)
    return jnp.where(bits < 0, bits ^ jnp.int32(0x7FFFFFFF), bits)


def _count(mask):
    return jnp.sum(jnp.where(mask, 1.0, 0.0), axis=-1, keepdims=True)


def _kth_largest_key(keys, k):
    def body(i, t):
        cand = t + lax.shift_left(jnp.int32(1), jnp.int32(31) - i)
        cnt = _count(keys[0] >= cand)
        for kk in keys[1:]:
            cnt = cnt + _count(kk >= cand)
        return jnp.where(cnt >= float(k), cand, t)

    t0 = jnp.full((keys[0].shape[0], 1), INT_MIN, I32)
    return lax.fori_loop(0, 32, body, t0)


def _strict_upper(n):
    return (jnp.arange(n)[:, None] < jnp.arange(n)[None, :]).astype(BF16)


KEY_CHUNK = 256
KEY_NEG_INF = -2139095041


def _dsa_prompt_kernel(q_ref, k_ref, v_ref, qi_ref, wi_ref, ki_ref, tri_ref, o_ref, bias_s, k_bf, v_bf, ki_bf,
                       *, topk, n_heads, w_scale):
    i = pl.program_id(1)
    tq = q_ref.shape[0]
    s_len = k_ref.shape[0]

    @pl.when(i == 0)
    def _():
        k_bf[...] = k_ref[...].astype(BF16)
        v_bf[...] = v_ref[...].astype(BF16)
        ki_bf[...] = ki_ref[...].astype(BF16)

    w = wi_ref[...][:, :IDX_HEADS] * w_scale
    qi_stack = jnp.concatenate([qi_ref[:, h * IDX_DIM:(h + 1) * IDX_DIM] for h in range(IDX_HEADS)], axis=0)
    pieces = []
    for c0 in range(0, s_len, KEY_CHUNK):
        d = _nt_dot(qi_stack, ki_bf[c0:c0 + KEY_CHUNK, :])
        sc_c = jnp.zeros((tq, KEY_CHUNK), F32)
        for h in range(IDX_HEADS):
            sc_c = sc_c + jnp.maximum(d[h * tq:(h + 1) * tq, :], 0.0) * w[:, h:h + 1]
        pieces.append(sc_c)
    sc = jnp.concatenate(pieces, axis=1)
    row = i * tq + lax.broadcasted_iota(I32, (tq, s_len), 0)
    col = lax.broadcasted_iota(I32, (tq, s_len), 1)
    causal = col <= row
    key = _sortable_key(jnp.where(causal, sc, NEG_INF))
    thr = _kth_largest_key([key], topk)
    gt = key > thr
    eq = key == thr
    need = float(topk) - _count(gt)
    bias_s[...] = jnp.where((key >= thr) & causal, 0.0, NEG_INF)
    tied = (thr > KEY_NEG_INF) & (_count(eq) > need)

    @pl.when(jnp.max(jnp.where(tied, 1.0, 0.0)) > 0.0)
    def _():
        rank = _dot(jnp.where(eq, 1.0, 0.0).astype(BF16), tri_ref[...])
        sel = (gt | (eq & (rank < need))) & causal
        bias_s[...] = jnp.where(sel, 0.0, NEG_INF)

    group = n_heads // N_KV_HEADS
    bias = jnp.concatenate([bias_s[...]] * group, axis=0)
    for kv in range(N_KV_HEADS):
        kh = k_bf[:, kv * HEAD_DIM:(kv + 1) * HEAD_DIM]
        vh = v_bf[:, kv * HEAD_DIM:(kv + 1) * HEAD_DIM]
        heads = range(kv * group, (kv + 1) * group)
        qs = jnp.concatenate([q_ref[:, hh * HEAD_DIM:(hh + 1) * HEAD_DIM] for hh in heads], axis=0).astype(BF16)
        s = _nt_dot(qs, kh) * (HEAD_DIM ** -0.5) + bias
        m = jnp.max(s, axis=-1, keepdims=True)
        p = jnp.exp(s - m)
        l = jnp.sum(p, axis=-1, keepdims=True)
        o = _dot(p.astype(BF16), vh) / l
        for g, hh in enumerate(heads):
            o_ref[:, hh * HEAD_DIM:(hh + 1) * HEAD_DIM] = o[g * tq:(g + 1) * tq, :].astype(o_ref.dtype)


def _dsa_prompt(proj, q_idx, k_idx, lay, *, bsz, seq, tq):
    topk = min(TOPK_MAX, seq // 4)
    q_w = lay["q_w"]
    kv_w = N_KV_HEADS * HEAD_DIM
    nq = seq // tq
    return pl.pallas_call(
        functools.partial(_dsa_prompt_kernel, topk=topk, n_heads=q_w // HEAD_DIM,
                          w_scale=(IDX_HEADS * IDX_DIM) ** -0.5),
        out_shape=jax.ShapeDtypeStruct((bsz * seq, q_w), BF16),
        grid=(bsz, nq),
        in_specs=[pl.BlockSpec((tq, q_w), lambda b, i: (b * nq + i, lay["q"] // q_w)),
                  pl.BlockSpec((seq, kv_w), lambda b, i: (b, lay["k"] // kv_w)),
                  pl.BlockSpec((seq, kv_w), lambda b, i: (b, lay["v"] // kv_w)),
                  pl.BlockSpec((tq, IDX_HEADS * IDX_DIM), lambda b, i: (b * nq + i, 0)),
                  pl.BlockSpec((tq, LANES), lambda b, i: (b * nq + i, lay["wi"] // LANES)),
                  pl.BlockSpec((seq, IDX_DIM), lambda b, i: (b, 0)),
                  pl.BlockSpec((seq, seq), lambda b, i: (0, 0))],
        out_specs=pl.BlockSpec((tq, q_w), lambda b, i: (b * nq + i, 0)),
        scratch_shapes=[pltpu.VMEM((tq, seq), F32), pltpu.VMEM((seq, kv_w), BF16), pltpu.VMEM((seq, kv_w), BF16),
                        pltpu.VMEM((seq, IDX_DIM), BF16)],
        compiler_params=_params("parallel", "arbitrary"),
        name="dsa_prompt",
    )(proj, proj, proj, q_idx, proj, k_idx, _strict_upper(seq))


PAGES_PER_STEP = 8


def _page_scores(qi_rows, w_rows, page_bf, t_new):
    d = jnp.maximum(_nt_dot(qi_rows, page_bf), 0.0) * w_rows
    return jnp.concatenate(
        [jnp.sum(d[t * IDX_HEADS:(t + 1) * IDX_HEADS], axis=0, keepdims=True) for t in range(t_new)], axis=0)


def _dsa_sample_scores_kernel(pt_ref, qi_ref, w_ref, *rest, t_new, page):
    pages = rest[:PAGES_PER_STEP]
    knew_ref, sc_ref, scn_ref = rest[PAGES_PER_STEP:]
    p = pl.program_id(1)
    qi = qi_ref[0]
    w = w_ref[0]
    for g in range(PAGES_PER_STEP):
        sc_ref[0, :, g * page:(g + 1) * page] = _page_scores(qi, w, pages[g][0, 0].astype(BF16), t_new)

    @pl.when(p == pl.num_programs(1) - 1)
    def _():
        sc = _page_scores(qi, w, knew_ref[0].astype(BF16), t_new)
        row = lax.broadcasted_iota(I32, sc.shape, 0)
        col = lax.broadcasted_iota(I32, sc.shape, 1)
        scn_ref[0] = jnp.where(col <= row, sc, NEG_INF)


def _dsa_sample_attn_kernel(pt_ref, sc_ref, scn_ref, q_ref, *rest, t_new, page, topk, n_heads):
    kp = rest[:PAGES_PER_STEP]
    vp = rest[PAGES_PER_STEP:2 * PAGES_PER_STEP]
    knew_ref, vnew_ref, tri_ref, o_ref, thr_s, need_s, cnt_s, m_s, l_s, acc_s, bias_s, flag_s = rest[2 * PAGES_PER_STEP:]
    p = pl.program_id(1)
    chunk = PAGES_PER_STEP * page
    rows = t_new * n_heads

    @pl.when(p == 0)
    def _():
        key_p = _sortable_key(sc_ref[0])
        key_n = _sortable_key(scn_ref[0])
        thr = _kth_largest_key([key_p, key_n], topk)
        n_gt = _count(key_p > thr) + _count(key_n > thr)
        n_eq = _count(key_p == thr) + _count(key_n == thr)
        need = float(topk) - n_gt
        thr_s[...] = jnp.broadcast_to(thr, thr_s.shape)
        need_s[...] = jnp.broadcast_to(need, need_s.shape)
        cnt_s[...] = jnp.zeros(cnt_s.shape, F32)
        flag_s[0] = (jnp.max(n_eq - need) > 0.0).astype(I32)
        m_s[...] = jnp.full(m_s.shape, NEG_INF, F32)
        l_s[...] = jnp.zeros(l_s.shape, F32)
        acc_s[...] = jnp.zeros(acc_s.shape, F32)

    thr = thr_s[:, 0:1]
    q = [q_ref[0, kv] for kv in range(N_KV_HEADS)]

    def select(sc, width):
        key = _sortable_key(sc)
        bias_s[:, :width] = jnp.where(key >= thr, 0.0, NEG_INF)

        @pl.when(flag_s[0] != 0)
        def _():
            eq = key == thr
            eqf = jnp.where(eq, 1.0, 0.0)
            rank = _dot(eqf.astype(BF16), tri_ref[:width, :width]) + cnt_s[:, 0:1]
            sel = (key > thr) | (eq & (rank < need_s[:, 0:1]))
            bias_s[:, :width] = jnp.where(sel, 0.0, NEG_INF)
            cnt_s[...] = cnt_s[...] + jnp.sum(eqf, axis=-1, keepdims=True)

    def attend(k_pages, v_pages, bias_t):
        bias = jnp.concatenate([jnp.broadcast_to(bias_t[t:t + 1], (n_heads, bias_t.shape[1])) for t in range(t_new)],
                               axis=0)
        s_pages = []
        for k_kv in k_pages:
            s = _nt_dot(q[0], k_kv[0])
            for kv in range(1, N_KV_HEADS):
                s = s + _nt_dot(q[kv], k_kv[kv])
            s_pages.append(s)
        s = jnp.concatenate(s_pages, axis=1) * (HEAD_DIM ** -0.5) + bias
        m_old = m_s[:, 0:1]
        m_new = jnp.maximum(m_old, jnp.max(s, axis=-1, keepdims=True))
        m_safe = jnp.where(m_new == NEG_INF, 0.0, m_new)
        alpha = jnp.exp(m_old - m_safe)
        pr = jnp.exp(s - m_safe)
        l_s[...] = alpha * l_s[...] + jnp.sum(pr, axis=-1, keepdims=True)
        pb = pr.astype(BF16)
        for kv in range(N_KV_HEADS):
            cols = slice(kv * HEAD_DIM, (kv + 1) * HEAD_DIM)
            pv = _dot(pb[:, 0:page], v_pages[0][kv])
            for g in range(1, len(v_pages)):
                pv = pv + _dot(pb[:, g * page:(g + 1) * page], v_pages[g][kv])
            acc_s[:, cols] = alpha * acc_s[:, cols] + pv
        m_s[...] = jnp.broadcast_to(m_new, m_s.shape)

    def page_heads(ref):
        return [ref[0, 0, :, kv, :].astype(BF16) for kv in range(N_KV_HEADS)]

    def row_heads(ref):
        return [ref[0, :, kv * HEAD_DIM:(kv + 1) * HEAD_DIM].astype(BF16) for kv in range(N_KV_HEADS)]

    start = pl.multiple_of(p * chunk, chunk)
    select(sc_ref[0, :, pl.ds(start, chunk)], chunk)
    attend([page_heads(r) for r in kp], [page_heads(r) for r in vp], bias_s[...])

    @pl.when(p == pl.num_programs(1) - 1)
    def _():
        select(scn_ref[0], page)
        attend([row_heads(knew_ref)], [row_heads(vnew_ref)], bias_s[:, :page])
        o_ref[0] = acc_s[...] / l_s[:, 0:1]


def _dsa_sample(q, k, v, q_idx, w_idx, k_idx, cache_k, cache_v, cache_kidx, page_table, layer):
    bsz, t_new, q_w = q.shape
    n_heads = q_w // HEAD_DIM
    kv_w = N_KV_HEADS * HEAD_DIM
    page = cache_k.shape[2]
    n_pages = page_table.shape[1]
    past = n_pages * page
    topk = min(TOPK_MAX, (past + t_new) // 4)
    steps = n_pages // PAGES_PER_STEP
    chunk = PAGES_PER_STEP * page
    rows = t_new * n_heads

    qi_rows = q_idx.reshape(bsz, t_new * IDX_HEADS, IDX_DIM).astype(BF16)
    w_rows = (w_idx * (IDX_HEADS * IDX_DIM) ** -0.5).reshape(bsz, t_new * IDX_HEADS, 1)
    pad_rows = lambda a: jnp.pad(a, ((0, 0), (0, page - t_new), (0, 0)))
    kidx_new, k_new, v_new = pad_rows(k_idx), pad_rows(k), pad_rows(v)

    def page_spec(trailing, g):
        zeros = (0,) * (1 + len(trailing))
        return pl.BlockSpec((1, 1, page) + trailing,
                            lambda b, p, pt: (layer, pt[b, p * PAGES_PER_STEP + g]) + zeros)

    per_b = lambda b, p, pt: (b, 0, 0)
    sc_past, sc_new = pl.pallas_call(
        functools.partial(_dsa_sample_scores_kernel, t_new=t_new, page=page),
        out_shape=(jax.ShapeDtypeStruct((bsz, t_new, past), F32),
                   jax.ShapeDtypeStruct((bsz, t_new, page), F32)),
        grid_spec=pltpu.PrefetchScalarGridSpec(
            num_scalar_prefetch=1,
            grid=(bsz, steps),
            in_specs=[pl.BlockSpec((1, t_new * IDX_HEADS, IDX_DIM), per_b),
                      pl.BlockSpec((1, t_new * IDX_HEADS, 1), per_b)]
                     + [page_spec((IDX_DIM,), g) for g in range(PAGES_PER_STEP)]
                     + [pl.BlockSpec((1, page, IDX_DIM), per_b)],
            out_specs=(pl.BlockSpec((1, t_new, chunk), lambda b, p, pt: (b, 0, p)),
                       pl.BlockSpec((1, t_new, page), per_b))),
        compiler_params=_params("parallel", "arbitrary"),
        name="dsa_sample_scores",
    )(page_table, qi_rows, w_rows, *([cache_kidx] * PAGES_PER_STEP), kidx_new)

    group = n_heads // N_KV_HEADS
    head_kv = jnp.arange(n_heads) // group
    blockdiag = (head_kv[:, None] == jnp.arange(N_KV_HEADS)[None, :]).astype(F32)
    q4 = q.reshape(bsz, t_new, n_heads, 1, HEAD_DIM) * blockdiag[None, None, :, :, None]
    q_bd = q4.transpose(0, 3, 1, 2, 4).reshape(bsz, N_KV_HEADS, rows, HEAD_DIM).astype(BF16)

    out = pl.pallas_call(
        functools.partial(_dsa_sample_attn_kernel, t_new=t_new, page=page, topk=topk, n_heads=n_heads),
        out_shape=jax.ShapeDtypeStruct((bsz, rows, kv_w), F32),
        grid_spec=pltpu.PrefetchScalarGridSpec(
            num_scalar_prefetch=1,
            grid=(bsz, steps),
            in_specs=[pl.BlockSpec((1, t_new, past), per_b),
                      pl.BlockSpec((1, t_new, page), per_b),
                      pl.BlockSpec((1, N_KV_HEADS, rows, HEAD_DIM), lambda b, p, pt: (b, 0, 0, 0))]
                     + [page_spec((N_KV_HEADS, HEAD_DIM), g) for g in range(PAGES_PER_STEP)]
                     + [page_spec((N_KV_HEADS, HEAD_DIM), g) for g in range(PAGES_PER_STEP)]
                     + [pl.BlockSpec((1, page, kv_w), per_b), pl.BlockSpec((1, page, kv_w), per_b),
                        pl.BlockSpec((chunk, chunk), lambda b, p, pt: (0, 0))],
            out_specs=pl.BlockSpec((1, rows, kv_w), per_b),
            scratch_shapes=[pltpu.VMEM((t_new, LANES), I32), pltpu.VMEM((t_new, LANES), F32),
                            pltpu.VMEM((t_new, LANES), F32), pltpu.VMEM((rows, LANES), F32),
                            pltpu.VMEM((rows, LANES), F32), pltpu.VMEM((rows, kv_w), F32),
                            pltpu.VMEM((t_new, chunk), F32), pltpu.SMEM((1,), I32)]),
        compiler_params=_params("parallel", "arbitrary"),
        name="dsa_sample_attn",
    )(page_table, sc_past, sc_new, q_bd, *([cache_k] * PAGES_PER_STEP), *([cache_v] * PAGES_PER_STEP),
      k_new, v_new, _strict_upper(chunk))
    o5 = out.reshape(bsz, t_new, N_KV_HEADS, group, N_KV_HEADS, HEAD_DIM)
    o = jnp.stack([o5[:, :, kv, :, kv, :] for kv in range(N_KV_HEADS)], axis=2)
    return o.reshape(bsz, t_new, q_w)


def _softplus(x):
    return jnp.maximum(x, 0.0) + jnp.log1p(jnp.exp(-jnp.abs(x)))


def _rglru_kernel(u_ref, g_ref, cp_ref, h0_ref, cw_ref, cb_ref, wa_ref, ba_ref, wx_ref, bx_ref, lam_ref,
                  y_ref, hl_ref, up_s, a_s, b_s, *, t_valid, t_chunk):
    t_len, wb = u_ref.shape
    lead = SUBLANES - (CONV_W - 1)
    up_s[0:SUBLANES, :] = jnp.zeros((SUBLANES, wb), F32)
    up_s[lead:SUBLANES, :] = cp_ref[0]
    up_s[SUBLANES:SUBLANES + t_len, :] = u_ref[...]
    neg_c_sp = -LRU_C * _softplus(-lam_ref[...])
    for c0 in range(0, t_len, t_chunk):
        xc = cw_ref[0:1, :] * up_s[lead + c0:lead + c0 + t_chunk, :]
        for j in range(1, CONV_W):
            xc = xc + cw_ref[j:j + 1, :] * up_s[lead + j + c0:lead + j + c0 + t_chunk, :]
        xc = cb_ref[...] + xc
        ra, ix = [], []
        for hd in range(wb // LANES):
            xh = xc[:, hd * LANES:(hd + 1) * LANES].astype(BF16)
            ra.append(_dot(xh, wa_ref[hd]))
            ix.append(_dot(xh, wx_ref[hd]))
        r = jax.nn.sigmoid(jnp.concatenate(ra, axis=-1) + ba_ref[...])
        gi = jax.nn.sigmoid(jnp.concatenate(ix, axis=-1) + bx_ref[...])
        log_a = r * neg_c_sp
        a = jnp.exp(log_a)
        a_s[c0:c0 + t_chunk, :] = a
        b_s[c0:c0 + t_chunk, :] = jnp.sqrt(-jnp.tanh(log_a) * (a * a + 1.0)) * (gi * xc)

    def step(t, h):
        h = a_s[pl.ds(t, 1), :] * h + b_s[pl.ds(t, 1), :]
        b_s[pl.ds(t, 1), :] = h
        return h

    lax.fori_loop(0, t_len, step, h0_ref[0], unroll=8)
    hl_ref[0] = b_s[t_valid - 1:t_valid, :]
    for c0 in range(0, t_len, t_chunk):
        rows = slice(c0, c0 + t_chunk)
        y_ref[rows, :] = (b_s[rows, :] * jax.nn.gelu(g_ref[rows, :])).astype(y_ref.dtype)


def _rglru(u_src, g_src, conv_prev, h0, conv_w, conv_b, wa, ba, wx, bx, lam, *, bsz, t_len, t_valid,
           u_col, g_col, width, wb):
    nb = width // wb
    heads_per = wb // LANES
    vec = lambda a: a.reshape(1, width)
    vspec = pl.BlockSpec((1, wb), lambda b, c: (0, c))
    t_chunk = min(t_len, 256)
    y, h_last = pl.pallas_call(
        functools.partial(_rglru_kernel, t_valid=t_valid, t_chunk=t_chunk),
        out_shape=(jax.ShapeDtypeStruct((bsz * t_len, width), BF16),
                   jax.ShapeDtypeStruct((bsz, 1, width), F32)),
        grid=(bsz, nb),
        in_specs=[pl.BlockSpec((t_len, wb), lambda b, c: (b, u_col // wb + c)),
                  pl.BlockSpec((t_len, wb), lambda b, c: (b, g_col // wb + c)),
                  pl.BlockSpec((1, CONV_W - 1, wb), lambda b, c: (b, 0, c)),
                  pl.BlockSpec((1, 1, wb), lambda b, c: (b, 0, c)),
                  pl.BlockSpec((CONV_W, wb), lambda b, c: (0, c)),
                  vspec,
                  pl.BlockSpec((heads_per, LANES, LANES), lambda b, c: (c, 0, 0)),
                  vspec,
                  pl.BlockSpec((heads_per, LANES, LANES), lambda b, c: (c, 0, 0)),
                  vspec, vspec],
        out_specs=(pl.BlockSpec((t_len, wb), lambda b, c: (b, c)),
                   pl.BlockSpec((1, 1, wb), lambda b, c: (b, 0, c))),
        scratch_shapes=[pltpu.VMEM((t_len + SUBLANES, wb), F32), pltpu.VMEM((t_len, wb), F32),
                        pltpu.VMEM((t_len, wb), F32)],
        compiler_params=_params("parallel", "arbitrary"),
        name="rglru",
    )(u_src, g_src, conv_prev, h0.reshape(bsz, 1, width), conv_w, vec(conv_b), wa.astype(BF16), vec(ba),
      wx.astype(BF16), vec(bx), vec(lam))
    return y, h_last.reshape(bsz, width)


def _s5_disc_kernel(are_ref, aim_ref, ls_ref, abr_ref, abi_ref, cfr_ref, cfi_ref):
    a_re, a_im = are_ref[...], aim_ref[...]
    dt = jnp.exp(ls_ref[...])
    mag = jnp.exp(a_re * dt)
    ab_re = mag * jnp.cos(a_im * dt)
    ab_im = mag * jnp.sin(a_im * dt)
    den = a_re * a_re + a_im * a_im
    nr, ni = ab_re - 1.0, ab_im
    abr_ref[...] = ab_re
    abi_ref[...] = ab_im
    cfr_ref[...] = (nr * a_re + ni * a_im) / den
    cfi_ref[...] = (ni * a_re - nr * a_im) / den


def _s5_discretize(a_re, a_im, log_step):
    g, n = a_re.shape
    shp = jax.ShapeDtypeStruct((g, n), F32)
    return pl.pallas_call(_s5_disc_kernel, out_shape=(shp, shp, shp, shp), name="s5_discretize")(
        a_re, a_im, jnp.broadcast_to(log_step[:, None], (g, n)))


def _s5_kernel(u_ref, wb_ref, wc_ref, are_ref, aim_ref, d_ref, h0r_ref, h0i_ref, z_ref, hr_ref, hi_ref, x_s, st_s,
               *, nj, t_chunk):
    c = pl.program_id(1)
    ns = are_ref.shape[2]

    @pl.when(c == 0)
    def _():
        st_s[0] = h0r_ref[0]
        st_s[1] = h0i_ref[0]

    u = u_ref[0]
    rowj = (lax.broadcasted_iota(I32, u.shape, 0) % SUBLANES) // (SUBLANES // nj)
    u_bf = u.astype(BF16)
    if nj > 1:
        zero = jnp.zeros(u_bf.shape, BF16)
        u_bf = jnp.concatenate([jnp.where(rowj == j, u_bf, zero) for j in range(nj)], axis=1)
    x_s[...] = _dot(u_bf, wb_ref[0])
    ar, ai = are_ref[0], aim_ref[0]

    def step(t, carry):
        xr, xi = carry
        r0 = pl.multiple_of(t * SUBLANES, SUBLANES)
        nr = ar * xr - ai * xi + x_s[pl.ds(r0, SUBLANES), 0:ns]
        ni = ar * xi + ai * xr + x_s[pl.ds(r0, SUBLANES), ns:2 * ns]
        x_s[pl.ds(r0, SUBLANES), 0:ns] = nr
        x_s[pl.ds(r0, SUBLANES), ns:2 * ns] = ni
        return nr, ni

    xr, xi = lax.fori_loop(0, t_chunk, step, (st_s[0], st_s[1]), unroll=min(t_chunk, 4))
    st_s[0] = xr
    st_s[1] = xi
    yb = _dot(x_s[...].astype(BF16), wc_ref[0])
    y = yb[:, 0:LANES]
    dvec = jnp.broadcast_to(d_ref[0, 0:1, :], y.shape)
    for j in range(1, nj):
        y = jnp.where(rowj == j, yb[:, j * LANES:(j + 1) * LANES], y)
        dvec = jnp.where(rowj == j, d_ref[0, j:j + 1, :], dvec)
    z_ref[0] = jax.nn.gelu(y + dvec * u).astype(z_ref.dtype)

    @pl.when(c == pl.num_programs(1) - 1)
    def _():
        hr_ref[0] = xr
        hi_ref[0] = xi


def _s5_scan(u, h0_re, h0_im, ab_re, ab_im, bb_re, bb_im, c_re, c_im, d_skip):
    bsz, t_len, d = u.shape
    groups, n_state = ab_re.shape
    gl = LANES // S5_GROUP
    ns = gl * n_state
    nj = SUBLANES // bsz
    nq = d // (LANES * nj)
    t_chunk = min(t_len, 256)
    eye_g = jnp.eye(gl, dtype=F32)

    u_rows = u.reshape(bsz, t_len, nq, nj, LANES).transpose(2, 1, 3, 0, 4).reshape(nq, t_len * SUBLANES, LANES)

    def b_weight(bb):
        w = bb.reshape(nq, nj, gl, n_state, S5_GROUP).transpose(0, 1, 2, 4, 3)
        w = w[:, :, :, :, None, :] * eye_g[None, None, :, None, :, None]
        return w.reshape(nq, nj * LANES, ns)

    w_b = jnp.concatenate([b_weight(bb_re), b_weight(bb_im)], axis=-1).astype(BF16)

    def c_weight(cc):
        w = cc.reshape(nq, nj, gl, S5_GROUP, n_state).transpose(0, 2, 4, 1, 3)
        w = w[:, :, :, :, None, :] * eye_g[None, :, None, None, :, None]
        return w.reshape(nq, ns, nj * LANES)

    w_c = jnp.concatenate([c_weight(c_re), -c_weight(c_im)], axis=1).astype(BF16)

    def coef(a):
        a = a.reshape(nq, nj, 1, ns)
        return jnp.broadcast_to(a, (nq, nj, bsz, ns)).reshape(nq, SUBLANES, ns)

    def state_in(h):
        return h.reshape(bsz, nq, nj, ns).transpose(1, 2, 0, 3).reshape(nq, SUBLANES, ns)

    def state_out(h):
        return h.reshape(nq, nj, bsz, ns).transpose(2, 0, 1, 3).reshape(bsz, groups, n_state)

    rows = t_chunk * SUBLANES
    per_q = lambda q, c: (q, 0, 0)
    st_spec = pl.BlockSpec((1, SUBLANES, ns), per_q)
    st_shape = jax.ShapeDtypeStruct((nq, SUBLANES, ns), F32)
    z, hr, hi = pl.pallas_call(
        functools.partial(_s5_kernel, nj=nj, t_chunk=t_chunk),
        out_shape=(jax.ShapeDtypeStruct((nq, t_len * SUBLANES, LANES), BF16), st_shape, st_shape),
        grid=(nq, t_len // t_chunk),
        in_specs=[pl.BlockSpec((1, rows, LANES), lambda q, c: (q, c, 0)),
                  pl.BlockSpec((1, nj * LANES, 2 * ns), per_q),
                  pl.BlockSpec((1, 2 * ns, nj * LANES), per_q),
                  st_spec, st_spec,
                  pl.BlockSpec((1, nj, LANES), per_q),
                  st_spec, st_spec],
        out_specs=(pl.BlockSpec((1, rows, LANES), lambda q, c: (q, c, 0)), st_spec, st_spec),
        scratch_shapes=[pltpu.VMEM((rows, 2 * ns), F32), pltpu.VMEM((2, SUBLANES, ns), F32)],
        compiler_params=_params("parallel", "arbitrary"),
        name="s5_scan",
    )(u_rows, w_b, w_c, coef(ab_re), coef(ab_im), d_skip.reshape(nq, nj, LANES), state_in(h0_re), state_in(h0_im))
    z = z.reshape(nq, t_len, nj, bsz, LANES).transpose(3, 1, 0, 2, 4).reshape(bsz, t_len, d)
    return z, state_out(hr), state_out(hi)


MOE_TM = 512
COMBINE_ROWS = 32
DISPATCH_ROWS = 128


def _moe_plan(ids, ranks, counts, n_tokens, n_experts):
    rows = ids.shape[0]
    n_tiles = (n_tokens * TOP_K) // MOE_TM + n_experts
    spare_tiles = -(-(rows - n_tokens) * TOP_K // MOE_TM)
    tiles_per = (counts + MOE_TM - 1) // MOE_TM
    tile_end = jnp.cumsum(tiles_per)
    run_start = (tile_end - tiles_per) * MOE_TM
    chose = ids[..., None] == jnp.arange(n_experts, dtype=I32)
    dest = jnp.sum(jnp.where(chose, run_start, 0), axis=-1) + ranks
    r = jnp.arange(rows, dtype=I32)[:, None]
    spare = n_tiles * MOE_TM + (r - n_tokens) * TOP_K + jnp.arange(TOP_K, dtype=I32)[None, :]
    is_token = r < n_tokens
    dest_dispatch = jnp.where(is_token, dest, spare).reshape(-1)
    dest_combine = jnp.where(is_token, dest, 0).reshape(-1)
    n_valid = tile_end[-1]
    tile_ids = jnp.arange(n_tiles + spare_tiles, dtype=I32)
    tile_expert = jnp.sum(tile_end[None, :] <= jnp.minimum(tile_ids[:n_tiles], n_valid - 1)[:, None], axis=1)
    has_tiles = tiles_per > 0
    is_last = jnp.any(has_tiles[None, :] & (tile_end[None, :] - 1 == tile_ids[:, None]), axis=1)
    zero_fill = (is_last | (tile_ids >= n_valid)).astype(I32)
    e_ids = jnp.arange(n_experts, dtype=I32)
    later = jnp.min(jnp.where(has_tiles[None, :] & (e_ids[None, :] > e_ids[:, None]), e_ids[None, :], n_experts),
                    axis=1)
    first = jnp.min(jnp.where(has_tiles, e_ids, n_experts))
    wraps = later == n_experts
    next_of = jnp.where(wraps, first, later)
    mine = tile_expert[:, None] == e_ids[None, :]
    next_expert = jnp.sum(jnp.where(mine, next_of[None, :], 0), axis=1)
    next_wraps = jnp.sum(jnp.where(mine, wraps[None, :].astype(I32), 0), axis=1)
    runs = (tile_expert.astype(I32), n_valid.astype(I32).reshape(1), next_expert.astype(I32),
            next_wraps.astype(I32))
    return dest_dispatch, dest_combine, runs, zero_fill, n_tiles, n_tiles + spare_tiles


def _dispatch_kernel(dest_ref, zf_ref, x_ref, o_hbm, zbuf, sem, zsem, *, total_tiles):
    t = pl.program_id(0)
    tr = x_ref.shape[0]

    def zero_copy(tile):
        start = pl.multiple_of(tile * MOE_TM, MOE_TM)
        return pltpu.make_async_copy(zbuf, o_hbm.at[pl.ds(start, MOE_TM), :], zsem)

    @pl.when(t == 0)
    def _():
        zbuf[...] = jnp.zeros(zbuf.shape, zbuf.dtype)

        def start(tile, c):
            @pl.when(zf_ref[tile] != 0)
            def _():
                zero_copy(tile).start()
            return c

        def wait(tile, c):
            @pl.when(zf_ref[tile] != 0)
            def _():
                zero_copy(tile).wait()
            return c

        lax.fori_loop(0, total_tiles, start, 0)
        lax.fori_loop(0, total_tiles, wait, 0)

    def body(r, c):
        for k in range(TOP_K):
            row = dest_ref[(t * tr + r) * TOP_K + k]
            pltpu.make_async_copy(x_ref.at[pl.ds(r, 1), :], o_hbm.at[pl.ds(row, 1), :], sem).start()
        return c

    lax.fori_loop(0, tr, body, 0)
    for k in range(TOP_K):
        pltpu.make_async_copy(x_ref, o_hbm.at[pl.ds(0, tr), :], sem).wait()


def _dispatch(x, dest, zero_fill, total_tiles):
    m, d = x.shape
    return pl.pallas_call(
        functools.partial(_dispatch_kernel, total_tiles=total_tiles),
        out_shape=jax.ShapeDtypeStruct((total_tiles * MOE_TM, d), x.dtype),
        grid_spec=pltpu.PrefetchScalarGridSpec(
            num_scalar_prefetch=2,
            grid=(m // DISPATCH_ROWS,),
            in_specs=[pl.BlockSpec((DISPATCH_ROWS, d), lambda t, dst, zf: (t, 0))],
            out_specs=pl.BlockSpec(memory_space=pl.ANY),
            scratch_shapes=[pltpu.VMEM((MOE_TM, d), x.dtype), pltpu.SemaphoreType.DMA(()),
                            pltpu.SemaphoreType.DMA(())]),
        compiler_params=_params("arbitrary"),
        name="moe_dispatch",
    )(dest, zero_fill, x)


def _new_expert(te_ref, t):
    return (t == 0) | (te_ref[t] != te_ref[jnp.maximum(t - 1, 0)])


def _run_weights(te_ref, nx_ref, wrap_ref, copies, n_col_blocks, cast):
    j = pl.program_id(0)
    t = pl.program_id(1)

    @pl.when((j == 0) & (t == 0))
    def _():
        for c in copies(te_ref[t], j):
            c.start()

    for c in copies(te_ref[t], j):
        c.wait()
    cast()
    next_j = j + wrap_ref[t]

    @pl.when(next_j < n_col_blocks)
    def _():
        for c in copies(nx_ref[t], next_j):
            c.start()


def _moe_gate_up_kernel(te_ref, nv_ref, nx_ref, wrap_ref, x_ref, w_hbm, bg_ref, bu_ref, a_ref, stage, wg_bf, wu_bf,
                        sem, *, k_chunk, layer, nb1):
    t = pl.program_id(1)
    tn1 = a_ref.shape[1]

    @pl.when(t < nv_ref[0])
    def _():
        d = x_ref.shape[1]

        def copies(e, jj):
            return [pltpu.make_async_copy(
                w_hbm.at[layer, e, :, pl.ds(pl.multiple_of((half * nb1 + jj) * tn1, tn1), tn1)],
                stage.at[half], sem.at[half]) for half in range(2)]

        def cast():
            for k0 in range(0, d, k_chunk):
                wg_bf[k0:k0 + k_chunk, :] = stage[0, k0:k0 + k_chunk, :].astype(BF16)
                wu_bf[k0:k0 + k_chunk, :] = stage[1, k0:k0 + k_chunk, :].astype(BF16)

        @pl.when(_new_expert(te_ref, t))
        def _():
            _run_weights(te_ref, nx_ref, wrap_ref, copies, nb1, cast)

        hg = jnp.zeros(a_ref.shape, F32)
        hu = jnp.zeros(a_ref.shape, F32)
        for k0 in range(0, d, k_chunk):
            xk = x_ref[:, k0:k0 + k_chunk].astype(BF16)
            hg = hg + _dot(xk, wg_bf[k0:k0 + k_chunk, :])
            hu = hu + _dot(xk, wu_bf[k0:k0 + k_chunk, :])
        gate = jnp.minimum(hg + bg_ref[0, 0], SWIGLU_LIMIT)
        up = jnp.clip(hu + bu_ref[0, 0], -SWIGLU_LIMIT, SWIGLU_LIMIT)
        a_ref[...] = ((up + 1.0) * gate * jax.nn.sigmoid(SWIGLU_ALPHA * gate)).astype(a_ref.dtype)

    @pl.when(t >= nv_ref[0])
    def _():
        a_ref[...] = jnp.zeros(a_ref.shape, a_ref.dtype)


def _moe_down_kernel(te_ref, nv_ref, nx_ref, wrap_ref, a_ref, w_hbm, b_ref, y_ref, stage, w_bf, sem,
                     *, k_chunk, layer, nb2):
    t = pl.program_id(1)
    tn2 = y_ref.shape[1]

    @pl.when(t < nv_ref[0])
    def _():
        f = a_ref.shape[1]

        def copies(e, jj):
            return [pltpu.make_async_copy(w_hbm.at[layer, e, :, pl.ds(pl.multiple_of(jj * tn2, tn2), tn2)],
                                          stage, sem)]

        def cast():
            for k0 in range(0, f, k_chunk):
                w_bf[k0:k0 + k_chunk, :] = stage[k0:k0 + k_chunk, :].astype(BF16)

        @pl.when(_new_expert(te_ref, t))
        def _():
            _run_weights(te_ref, nx_ref, wrap_ref, copies, nb2, cast)

        y_ref[...] = _dot(a_ref[...], w_bf[...]) + b_ref[0, 0]

    @pl.when(t >= nv_ref[0])
    def _():
        y_ref[...] = jnp.zeros(y_ref.shape, y_ref.dtype)


def _moe_experts(xs, runs, w_gu, b_gu, w_dn, b_dn, layer, n_tiles):
    _, n_experts, d, two_f = w_gu.shape
    f = two_f // 2
    tn1 = min(512, f)
    nb1 = f // tn1
    n_slots = n_tiles * MOE_TM
    b_gu4 = b_gu.reshape(b_gu.shape[0], n_experts, 1, two_f)
    last = lambda t, nv: jnp.minimum(t, nv[0] - 1)
    act = pl.pallas_call(
        functools.partial(_moe_gate_up_kernel, k_chunk=min(1024, d), layer=layer, nb1=nb1),
        out_shape=jax.ShapeDtypeStruct((n_slots, f), BF16),
        grid_spec=pltpu.PrefetchScalarGridSpec(
            num_scalar_prefetch=4,
            grid=(nb1, n_tiles),
            in_specs=[pl.BlockSpec((MOE_TM, d), lambda j, t, te, nv, nx, wr: (last(t, nv), 0)),
                      pl.BlockSpec(memory_space=pl.ANY),
                      pl.BlockSpec((1, 1, 1, tn1), lambda j, t, te, nv, nx, wr: (layer, te[t], 0, j)),
                      pl.BlockSpec((1, 1, 1, tn1), lambda j, t, te, nv, nx, wr: (layer, te[t], 0, nb1 + j))],
            out_specs=pl.BlockSpec((MOE_TM, tn1), lambda j, t, te, nv, nx, wr: (t, j)),
            scratch_shapes=[pltpu.VMEM((2, d, tn1), F32), pltpu.VMEM((d, tn1), BF16), pltpu.VMEM((d, tn1), BF16),
                            pltpu.SemaphoreType.DMA((2,))]),
        compiler_params=_params("arbitrary", "arbitrary"),
        name="moe_gate_up",
    )(*runs, xs, w_gu, b_gu4, b_gu4)

    tn2 = min(2048, d)
    nb2 = d // tn2
    return pl.pallas_call(
        functools.partial(_moe_down_kernel, k_chunk=min(512, f), layer=layer, nb2=nb2),
        out_shape=jax.ShapeDtypeStruct((n_slots, d), F32),
        grid_spec=pltpu.PrefetchScalarGridSpec(
            num_scalar_prefetch=4,
            grid=(nb2, n_tiles),
            in_specs=[pl.BlockSpec((MOE_TM, f), lambda j, t, te, nv, nx, wr: (last(t, nv), 0)),
                      pl.BlockSpec(memory_space=pl.ANY),
                      pl.BlockSpec((1, 1, 1, tn2), lambda j, t, te, nv, nx, wr: (layer, te[t], 0, j))],
            out_specs=pl.BlockSpec((MOE_TM, tn2), lambda j, t, te, nv, nx, wr: (t, j)),
            scratch_shapes=[pltpu.VMEM((f, tn2), F32), pltpu.VMEM((f, tn2), BF16), pltpu.SemaphoreType.DMA(())]),
        compiler_params=_params("arbitrary", "arbitrary"),
        name="moe_down",
    )(*runs, act, w_dn, b_dn.reshape(b_dn.shape[0], n_experts, 1, d))


def _combine_ln_kernel(dest_ref, ys_hbm, gate_ref, x_ref, g_ref, b_ref, o_ref, ob_ref, buf, sem, *, alpha):
    t = pl.program_id(0)
    nt = pl.num_programs(0)
    tr = o_ref.shape[0]

    def issue(tile, slot):
        def body(r, carry):
            for k in range(TOP_K):
                row = dest_ref[(tile * tr + r) * TOP_K + k]
                pltpu.make_async_copy(ys_hbm.at[pl.ds(row, 1), :], buf.at[slot, k, pl.ds(r, 1), :],
                                      sem.at[slot]).start()
            return carry
        lax.fori_loop(0, tr, body, 0)

    @pl.when(t == 0)
    def _():
        issue(0, 0)

    @pl.when(t + 1 < nt)
    def _():
        issue(t + 1, (t + 1) % 2)

    slot = t % 2
    for k in range(TOP_K):
        pltpu.make_async_copy(ys_hbm.at[pl.ds(0, tr), :], buf.at[slot, k], sem.at[slot]).wait()
    gates = gate_ref[...]
    f = gates[:, 0:1] * buf[slot, 0]
    for k in range(1, TOP_K):
        f = f + gates[:, k:k + 1] * buf[slot, k]
    y = _ln_rows(alpha * x_ref[...] + f, g_ref[...], b_ref[...])
    o_ref[...] = y
    ob_ref[...] = y.astype(ob_ref.dtype)


def _combine_ln(ys, dest, gates, x1, g, b, *, alpha):
    m, d = x1.shape
    row = lambda t, dst: (t, 0)
    fixed = lambda t, dst: (0, 0)
    return pl.pallas_call(
        functools.partial(_combine_ln_kernel, alpha=alpha),
        out_shape=(jax.ShapeDtypeStruct((m, d), F32), jax.ShapeDtypeStruct((m, d), BF16)),
        grid_spec=pltpu.PrefetchScalarGridSpec(
            num_scalar_prefetch=1,
            grid=(m // COMBINE_ROWS,),
            in_specs=[pl.BlockSpec(memory_space=pl.ANY),
                      pl.BlockSpec((COMBINE_ROWS, LANES), row),
                      pl.BlockSpec((COMBINE_ROWS, d), row),
                      pl.BlockSpec((1, d), fixed), pl.BlockSpec((1, d), fixed)],
            out_specs=(pl.BlockSpec((COMBINE_ROWS, d), row), pl.BlockSpec((COMBINE_ROWS, d), row)),
            scratch_shapes=[pltpu.VMEM((2, TOP_K, COMBINE_ROWS, d), F32), pltpu.SemaphoreType.DMA((2,))]),
        compiler_params=_params("arbitrary"),
        name="moe_combine_ln",
    )(dest, ys, gates, x1, g.reshape(1, d), b.reshape(1, d))


def _moe_block(x, mix, n_tokens, layer, ln1_g, ln1_b, ln2_g, ln2_b, rw, rb, w_gu, b_gu, w_dn, b_dn, *, alpha):
    n_experts = rw.shape[1]
    x1, ids, gates, ranks, counts = _ln_router(x, mix, ln1_g, ln1_b, rw, rb, alpha=alpha, tr=256, n_tokens=n_tokens)
    dest_d, dest_c, runs, zero_fill, n_tiles, total_tiles = _moe_plan(
        ids[:, :TOP_K], ranks[:, :TOP_K], counts[0, :n_experts].astype(I32), n_tokens, n_experts)
    xs = _dispatch(x1, dest_d, zero_fill, total_tiles)
    ys = _moe_experts(xs, runs, w_gu, b_gu, w_dn, b_dn, layer, n_tiles)
    return _combine_ln(ys, dest_c, gates, x1, ln2_g, ln2_b, alpha=alpha)


ROW_TILE = 768


def kernel(x_prompt, x_sample, cache_k, cache_v, cache_kidx, state_conv, state_lru, state_s5_re, state_s5_im,
           page_table, w_in, w_idx_qb, kidx_ln_g, kidx_ln_b, conv_w, conv_b, lru_wa, lru_ba, lru_wx, lru_bx,
           lru_lambda, w_out, s5_a_re, s5_a_im, s5_log_step, s5_b_re, s5_b_im, s5_c_re, s5_c_im, s5_d,
           glu_w1, glu_b1, glu_w2, glu_b2, ln1_g, ln1_b, ln2_g, ln2_b, router_w, router_b, moe_w_gu, moe_b_gu,
           moe_w_down, moe_b_down):
    bp, sp, dm = x_prompt.shape
    bs, ss, _ = x_sample.shape
    depth = ln1_g.shape[0]
    alpha = (2 * depth) ** 0.25
    n_p, n_s = bp * sp, bs * ss
    n_tok = n_p + n_s
    n_rows = -(-n_tok // ROW_TILE) * ROW_TILE
    lru_w = conv_w.shape[2]
    q_w = w_out.shape[1] - lru_w
    kv_w = N_KV_HEADS * HEAD_DIM
    s5_groups = dm // S5_GROUP

    def unify(a_p, a_s):
        pad = jnp.zeros((n_rows - n_tok, a_p.shape[-1]), a_p.dtype)
        return jnp.concatenate([a_p.reshape(n_p, -1), a_s.reshape(n_s, -1), pad], axis=0)

    x = unify(x_prompt, x_sample)
    xb = x.astype(BF16)
    outs = {name: [] for name in ("k_p", "k_s", "v_p", "v_s", "i_p", "i_s", "c_p", "c_s", "h_p", "h_s",
                                   "r_p", "r_s", "m_p", "m_s")}
    for layer in range(depth):
        j = layer // 2
        if layer % 2 == 0:
            sizes = [q_w, kv_w, kv_w, IDX_Q_RANK, IDX_HEADS, IDX_DIM, lru_w, lru_w]
            cuts = [0]
            for s_ in sizes:
                cuts.append(cuts[-1] + s_)
            seg = lambda i: w_in[j][:, cuts[i]:cuts[i + 1]]
            lay = {"q": 0, "q_w": q_w, "u": q_w, "g": q_w + lru_w, "k": q_w + 2 * lru_w,
                   "v": q_w + 2 * lru_w + kv_w, "ki": q_w + 2 * lru_w + 2 * kv_w}
            lay["wi"] = lay["ki"] + IDX_DIM
            n_used = lay["wi"] + IDX_HEADS
            n_proj = -(-n_used // 512) * 512
            w_main = jnp.concatenate([seg(0), seg(6), seg(7), seg(1), seg(2), seg(5), seg(4),
                                      jnp.zeros((dm, n_proj - n_used), F32)], axis=1).astype(BF16)
            proj = _matmul(xb, w_main, tm=ROW_TILE, tn=512, name="in_proj")
            cq = _matmul(xb, seg(3).astype(BF16), tm=ROW_TILE, tn=IDX_Q_RANK // 2, out_dtype=BF16, name="cq_proj")
            q_idx = _matmul(cq, w_idx_qb[j].astype(BF16), tm=ROW_TILE, tn=1024, out_dtype=BF16, name="q_idx_proj")
            k_idx = _layer_norm_cols(proj, kidx_ln_g[j], kidx_ln_b[j], tr=ROW_TILE, col_block=lay["ki"] // IDX_DIM,
                                     width=IDX_DIM)
            lru_args = (conv_w[j], conv_b[j], lru_wa[j], lru_ba[j], lru_wx[j], lru_bx[j], lru_lambda[j])

            att_p = _dsa_prompt(proj, q_idx, k_idx, lay, bsz=bp, seq=sp, tq=128)
            rec_p, h_p = _rglru(proj, proj, jnp.zeros((bp, CONV_W - 1, lru_w), F32), jnp.zeros((bp, lru_w), F32),
                                *lru_args, bsz=bp, t_len=sp, t_valid=sp, u_col=lay["u"], g_col=lay["g"],
                                width=lru_w, wb=512)
            col = lambda a, name, width: a[:, lay[name]:lay[name] + width]
            proj_p = proj[:n_p]
            outs["k_p"].append(col(proj_p, "k", kv_w).reshape(bp, sp, N_KV_HEADS, HEAD_DIM))
            outs["v_p"].append(col(proj_p, "v", kv_w).reshape(bp, sp, N_KV_HEADS, HEAD_DIM))
            outs["i_p"].append(k_idx[:n_p].reshape(bp, sp, IDX_DIM))
            tail = CONV_W - 1
            outs["c_p"].append(jnp.stack([proj[(b + 1) * sp - tail:(b + 1) * sp, lay["u"]:lay["u"] + lru_w]
                                          for b in range(bp)]))
            outs["h_p"].append(h_p)

            proj_s = proj[n_p:n_tok]
            s3 = lambda a: a.reshape(bs, ss, -1)
            q_s, k_s, v_s = s3(col(proj_s, "q", q_w)), s3(col(proj_s, "k", kv_w)), s3(col(proj_s, "v", kv_w))
            u_s, g_s = s3(col(proj_s, "u", lru_w)), s3(col(proj_s, "g", lru_w))
            wi_s = s3(col(proj_s, "wi", IDX_HEADS))
            kidx_s = s3(k_idx[n_p:n_tok])
            att_s = _dsa_sample(q_s, k_s, v_s, s3(q_idx[n_p:n_tok]), wi_s, kidx_s, cache_k, cache_v, cache_kidx,
                                page_table, j)
            t_pad = -(-ss // SUBLANES) * SUBLANES
            pad_t = lambda a: jnp.pad(a, ((0, 0), (0, t_pad - ss), (0, 0))).reshape(bs * t_pad, -1)
            rec_s, h_s = _rglru(pad_t(u_s), pad_t(g_s), state_conv[j], state_lru[j], *lru_args, bsz=bs, t_len=t_pad,
                                t_valid=ss, u_col=0, g_col=0, width=lru_w, wb=512)
            rec_s = rec_s.reshape(bs, t_pad, lru_w)[:, :ss]
            outs["k_s"].append(k_s.reshape(bs, ss, N_KV_HEADS, HEAD_DIM))
            outs["v_s"].append(v_s.reshape(bs, ss, N_KV_HEADS, HEAD_DIM))
            outs["i_s"].append(kidx_s)
            up_s = jnp.concatenate([state_conv[j], u_s], axis=1)
            outs["c_s"].append(up_s[:, ss:])
            outs["h_s"].append(h_s)

            att = unify(att_p, att_s.astype(BF16))
            rec = unify(rec_p, rec_s)
            mix = _matmul_2in(att, rec, w_out[j].astype(BF16), tm=ROW_TILE, tn=512, name="out_proj")
        else:
            ab_re, ab_im, cf_re, cf_im = _s5_discretize(s5_a_re[j], s5_a_im[j], s5_log_step[j])
            b_re, b_im = s5_b_re[j], s5_b_im[j]
            bb_re = cf_re[..., None] * b_re - cf_im[..., None] * b_im
            bb_im = cf_re[..., None] * b_im + cf_im[..., None] * b_re
            s5_args = (ab_re, ab_im, bb_re, bb_im, s5_c_re[j], s5_c_im[j], s5_d[j])
            n_state = ab_re.shape[1]
            zeros = jnp.zeros((bp, s5_groups, n_state), F32)
            z_p, hr, hi = _s5_scan(x[:n_p].reshape(bp, sp, dm), zeros, zeros, *s5_args)
            outs["r_p"].append(hr)
            outs["m_p"].append(hi)
            z_s, hr, hi = _s5_scan(x[n_p:n_tok].reshape(bs, ss, dm), state_s5_re[j], state_s5_im[j], *s5_args)
            outs["r_s"].append(hr)
            outs["m_s"].append(hi)
            mix = _glu(unify(z_p, z_s), glu_w1[j].astype(BF16), glu_b1[j], glu_w2[j].astype(BF16), glu_b2[j],
                       tm=ROW_TILE, tn=512)
        x, xb = _moe_block(x, mix, n_tok, layer, ln1_g[layer], ln1_b[layer], ln2_g[layer], ln2_b[layer],
                           router_w[layer], router_b[layer], moe_w_gu, moe_b_gu, moe_w_down, moe_b_down,
                           alpha=alpha)
    st = lambda name: jnp.stack(outs[name])
    return (x[:n_p].reshape(bp, sp, dm), x[n_p:n_tok].reshape(bs, ss, dm),
            st("k_p"), st("k_s"), st("v_p"), st("v_s"), st("i_p"), st("i_s"), st("c_p"), st("c_s"),
            st("h_p"), st("h_s"), st("r_p"), st("r_s"), st("m_p"), st("m_s"))
```

```python
import functools
import math

import jax
import jax.numpy as jnp
from jax import lax
from jax.experimental import pallas as pl
from jax.experimental.pallas import tpu as pltpu

F32 = jnp.float32
BF16 = jnp.bfloat16
I32 = jnp.int32

HEAD_DIM = 128
N_KV_HEADS = 4
IDX_HEADS = 32
IDX_DIM = 128
IDX_Q_RANK = 768
TOPK_MAX = 256
LRU_HEADS = 16
CONV_W = 4
LRU_C = 8.0
S5_GROUP = 16
TOP_K = 4
SWIGLU_LIMIT = 7.0
SWIGLU_ALPHA = 1.702
LN_EPS = 1e-5

LANES = 128
SUBLANES = 8
VMEM_LIMIT_BYTES = 56 * 1024 * 1024

NEG_INF = float("-inf")
INT_MIN = -(2 ** 31)


def _params(*sem):
    return pltpu.CompilerParams(dimension_semantics=sem, vmem_limit_bytes=VMEM_LIMIT_BYTES)


def _nt_dot(a, b):
    return lax.dot_general(a, b, (((1,), (1,)), ((), ())), preferred_element_type=F32)


def _dot(a, b):
    return jnp.dot(a, b, preferred_element_type=F32)


def _mm_kernel(x_ref, w_ref, o_ref):
    o_ref[...] = _dot(x_ref[...].astype(BF16), w_ref[...]).astype(o_ref.dtype)


def _matmul(x, w, *, tm, tn, out_dtype=F32, name="matmul"):
    m, k = x.shape
    n = w.shape[1]
    return pl.pallas_call(
        _mm_kernel,
        out_shape=jax.ShapeDtypeStruct((m, n), out_dtype),
        grid=(m // tm, n // tn),
        in_specs=[pl.BlockSpec((tm, k), lambda i, j: (i, 0)),
                  pl.BlockSpec((k, tn), lambda i, j: (0, j))],
        out_specs=pl.BlockSpec((tm, tn), lambda i, j: (i, j)),
        compiler_params=_params("parallel", "arbitrary"),
        name=name,
    )(x, w)


def _mm2_kernel(xa_ref, xb_ref, wa_ref, wb_ref, o_ref):
    o_ref[...] = _dot(xa_ref[...], wa_ref[...]) + _dot(xb_ref[...], wb_ref[...])


def _matmul_2in(xa, xb, w, *, tm, tn, name):
    m, ka = xa.shape
    n = w.shape[1]
    return pl.pallas_call(
        _mm2_kernel,
        out_shape=jax.ShapeDtypeStruct((m, n), F32),
        grid=(m // tm, n // tn),
        in_specs=[pl.BlockSpec((tm, ka), lambda i, j: (i, 0)),
                  pl.BlockSpec((tm, ka), lambda i, j: (i, 0)),
                  pl.BlockSpec((ka, tn), lambda i, j: (0, j)),
                  pl.BlockSpec((ka, tn), lambda i, j: (1, j))],
        out_specs=pl.BlockSpec((tm, tn), lambda i, j: (i, j)),
        compiler_params=_params("parallel", "arbitrary"),
        name=name,
    )(xa, xb, w, w)


def _glu_kernel(z_ref, w1_ref, b1_ref, w2_ref, b2_ref, o_ref):
    z = z_ref[...]
    a = _dot(z, w1_ref[...]) + b1_ref[...]
    b = _dot(z, w2_ref[...]) + b2_ref[...]
    o_ref[...] = a * jax.nn.sigmoid(b)


def _glu(z, w1, b1, w2, b2, *, tm, tn):
    m, k = z.shape
    n = w1.shape[1]
    x_spec = pl.BlockSpec((tm, k), lambda i, j: (i, 0))
    w_spec = pl.BlockSpec((k, tn), lambda i, j: (0, j))
    b_spec = pl.BlockSpec((1, tn), lambda i, j: (0, j))
    return pl.pallas_call(
        _glu_kernel,
        out_shape=jax.ShapeDtypeStruct((m, n), F32),
        grid=(m // tm, n // tn),
        in_specs=[x_spec, w_spec, b_spec, w_spec, b_spec],
        out_specs=pl.BlockSpec((tm, tn), lambda i, j: (i, j)),
        compiler_params=_params("parallel", "arbitrary"),
        name="glu",
    )(z, w1, b1.reshape(1, n), w2, b2.reshape(1, n))


def _ln_rows(x, g, b):
    mu = jnp.mean(x, axis=-1, keepdims=True)
    xc = x - mu
    var = jnp.mean(xc * xc, axis=-1, keepdims=True)
    return xc * lax.rsqrt(var + LN_EPS) * g + b


def _ln_kernel(x_ref, g_ref, b_ref, o_ref):
    o_ref[...] = _ln_rows(x_ref[...], g_ref[...], b_ref[...])


def _layer_norm_cols(x, g, b, *, tr, col_block, width):
    m = x.shape[0]
    return pl.pallas_call(
        _ln_kernel,
        out_shape=jax.ShapeDtypeStruct((m, width), F32),
        grid=(m // tr,),
        in_specs=[pl.BlockSpec((tr, width), lambda i: (i, col_block)),
                  pl.BlockSpec((1, width), lambda i: (0, 0)),
                  pl.BlockSpec((1, width), lambda i: (0, 0))],
        out_specs=pl.BlockSpec((tr, width), lambda i: (i, 0)),
        compiler_params=_params("parallel"),
        name="layer_norm",
    )(x, g.reshape(1, width), b.reshape(1, width))


def _top_gates(logits, n_experts):
    lane_f = lax.broadcasted_iota(I32, logits.shape, 1).astype(F32)
    l = jnp.where(lane_f < float(n_experts), logits, NEG_INF)
    vals, ids = [], []
    for _ in range(TOP_K):
        m = jnp.max(l, axis=-1, keepdims=True)
        idx = jnp.min(jnp.where(l == m, lane_f, float(LANES)), axis=-1, keepdims=True)
        vals.append(m)
        ids.append(idx)
        l = jnp.where(lane_f == idx, NEG_INF, l)
    es = [jnp.exp(v - vals[0]) for v in vals]
    den = es[0]
    for e in es[1:]:
        den = den + e
    return ids, [e / den for e in es]


def _ln_router_kernel(x_ref, mix_ref, g_ref, b_ref, rw_ref, rb_ref, tri_ref, o_ref, id_ref, gate_ref, rank_ref,
                      cnt_ref, *, alpha, n_experts, n_tokens):
    i = pl.program_id(0)
    tr = x_ref.shape[0]
    y = _ln_rows(alpha * x_ref[...] + mix_ref[...], g_ref[...], b_ref[...])
    o_ref[...] = y
    logits = _dot(y.astype(BF16), rw_ref[...]) + rb_ref[...]
    ids, gates = _top_gates(logits, n_experts)

    lane = lax.broadcasted_iota(I32, logits.shape, 1)
    lane_f = lane.astype(F32)
    valid = (i * tr + lax.broadcasted_iota(I32, logits.shape, 0)) < n_tokens
    chosen = jnp.zeros(logits.shape, F32)
    for r in range(TOP_K):
        chosen = chosen + jnp.where(lane_f == ids[r], 1.0, 0.0)
    chosen = jnp.where(valid, chosen, 0.0)

    @pl.when(i == 0)
    def _():
        cnt_ref[...] = jnp.zeros(cnt_ref.shape, F32)

    before = _dot(tri_ref[...], chosen.astype(BF16)) + cnt_ref[...]
    cnt_ref[...] = cnt_ref[...] + jnp.sum(chosen, axis=0, keepdims=True)

    idv = jnp.zeros(logits.shape, I32)
    gv = jnp.zeros(logits.shape, F32)
    rv = jnp.zeros(logits.shape, I32)
    for r in range(TOP_K):
        rank = jnp.sum(jnp.where(lane_f == ids[r], before, 0.0), axis=-1, keepdims=True)
        idv = jnp.where(lane == r, ids[r].astype(I32), idv)
        gv = jnp.where(lane == r, gates[r], gv)
        rv = jnp.where(lane == r, rank.astype(I32), rv)
    id_ref[...] = idv
    gate_ref[...] = gv
    rank_ref[...] = rv


def _ln_router(x, mix, g, b, rw, rb, *, alpha, tr, n_tokens):
    m, d = x.shape
    n_experts = rw.shape[1]
    rw_p = jnp.zeros((d, LANES), BF16).at[:, :n_experts].set(rw.astype(BF16))
    rb_p = jnp.zeros((1, LANES), F32).at[0, :n_experts].set(rb)
    tri = (jnp.arange(tr)[None, :] < jnp.arange(tr)[:, None]).astype(BF16)
    row = lambda i: (i, 0)
    fixed = lambda i: (0, 0)
    lanes_i = jax.ShapeDtypeStruct((m, LANES), I32)
    return pl.pallas_call(
        functools.partial(_ln_router_kernel, alpha=alpha, n_experts=n_experts, n_tokens=n_tokens),
        out_shape=(jax.ShapeDtypeStruct((m, d), F32), lanes_i, jax.ShapeDtypeStruct((m, LANES), F32), lanes_i,
                   jax.ShapeDtypeStruct((1, LANES), F32)),
        grid=(m // tr,),
        in_specs=[pl.BlockSpec((tr, d), row), pl.BlockSpec((tr, d), row),
                  pl.BlockSpec((1, d), fixed), pl.BlockSpec((1, d), fixed),
                  pl.BlockSpec((d, LANES), fixed), pl.BlockSpec((1, LANES), fixed),
                  pl.BlockSpec((tr, tr), fixed)],
        out_specs=(pl.BlockSpec((tr, d), row), pl.BlockSpec((tr, LANES), row), pl.BlockSpec((tr, LANES), row),
                   pl.BlockSpec((tr, LANES), row), pl.BlockSpec((1, LANES), fixed)),
        compiler_params=_params("arbitrary"),
        name="ln_router",
    )(x, mix, g.reshape(1, d), b.reshape(1, d), rw_p, rb_p, tri)


def _sortable_key(x):
    bits = pltpu.bitcast(x, I32)
    return jnp.where(bits < 0, bits ^ jnp.int32(0x7FFFFFFF), bits)


def _count(mask):
    return jnp.sum(jnp.where(mask, 1.0, 0.0), axis=-1, keepdims=True)


def _kth_largest_key(keys, k):
    def body(i, t):
        cand = t + lax.shift_left(jnp.int32(1), jnp.int32(31) - i)
        cnt = _count(keys[0] >= cand)
        for kk in keys[1:]:
            cnt = cnt + _count(kk >= cand)
        return jnp.where(cnt >= float(k), cand, t)

    t0 = jnp.full((keys[0].shape[0], 1), INT_MIN, I32)
    return lax.fori_loop(0, 32, body, t0)


def _strict_upper(n):
    return (jnp.arange(n)[:, None] < jnp.arange(n)[None, :]).astype(BF16)


KEY_CHUNK = 256
KEY_NEG_INF = -2139095041


def _dsa_prompt_kernel(q_ref, k_ref, v_ref, qi_ref, wi_ref, ki_ref, tri_ref, o_ref, bias_s, k_bf, v_bf, ki_bf,
                       *, topk, n_heads, w_scale):
    i = pl.program_id(1)
    tq = q_ref.shape[0]
    s_len = k_ref.shape[0]

    @pl.when(i == 0)
    def _():
        k_bf[...] = k_ref[...].astype(BF16)
        v_bf[...] = v_ref[...].astype(BF16)
        ki_bf[...] = ki_ref[...].astype(BF16)

    w = wi_ref[...][:, :IDX_HEADS] * w_scale
    qi_stack = jnp.concatenate([qi_ref[:, h * IDX_DIM:(h + 1) * IDX_DIM] for h in range(IDX_HEADS)], axis=0)
    pieces = []
    for c0 in range(0, s_len, KEY_CHUNK):
        d = _nt_dot(qi_stack, ki_bf[c0:c0 + KEY_CHUNK, :])
        sc_c = jnp.zeros((tq, KEY_CHUNK), F32)
        for h in range(IDX_HEADS):
            sc_c = sc_c + jnp.maximum(d[h * tq:(h + 1) * tq, :], 0.0) * w[:, h:h + 1]
        pieces.append(sc_c)
    sc = jnp.concatenate(pieces, axis=1)
    row = i * tq + lax.broadcasted_iota(I32, (tq, s_len), 0)
    col = lax.broadcasted_iota(I32, (tq, s_len), 1)
    causal = col <= row
    key = _sortable_key(jnp.where(causal, sc, NEG_INF))
    thr = _kth_largest_key([key], topk)
    gt = key > thr
    eq = key == thr
    need = float(topk) - _count(gt)
    bias_s[...] = jnp.where((key >= thr) & causal, 0.0, NEG_INF)
    tied = (thr > KEY_NEG_INF) & (_count(eq) > need)

    @pl.when(jnp.max(jnp.where(tied, 1.0, 0.0)) > 0.0)
    def _():
        rank = _dot(jnp.where(eq, 1.0, 0.0).astype(BF16), tri_ref[...])
        sel = (gt | (eq & (rank < need))) & causal
        bias_s[...] = jnp.where(sel, 0.0, NEG_INF)

    group = n_heads // N_KV_HEADS
    bias = jnp.concatenate([bias_s[...]] * group, axis=0)
    for kv in range(N_KV_HEADS):
        kh = k_bf[:, kv * HEAD_DIM:(kv + 1) * HEAD_DIM]
        vh = v_bf[:, kv * HEAD_DIM:(kv + 1) * HEAD_DIM]
        heads = range(kv * group, (kv + 1) * group)
        qs = jnp.concatenate([q_ref[:, hh * HEAD_DIM:(hh + 1) * HEAD_DIM] for hh in heads], axis=0).astype(BF16)
        s = _nt_dot(qs, kh) * (HEAD_DIM ** -0.5) + bias
        m = jnp.max(s, axis=-1, keepdims=True)
        p = jnp.exp(s - m)
        l = jnp.sum(p, axis=-1, keepdims=True)
        o = _dot(p.astype(BF16), vh) / l
        for g, hh in enumerate(heads):
            o_ref[:, hh * HEAD_DIM:(hh + 1) * HEAD_DIM] = o[g * tq:(g + 1) * tq, :].astype(o_ref.dtype)


def _dsa_prompt(proj, q_idx, k_idx, lay, *, bsz, seq, tq):
    topk = min(TOPK_MAX, seq // 4)
    q_w = lay["q_w"]
    kv_w = N_KV_HEADS * HEAD_DIM
    nq = seq // tq
    return pl.pallas_call(
        functools.partial(_dsa_prompt_kernel, topk=topk, n_heads=q_w // HEAD_DIM,
                          w_scale=(IDX_HEADS * IDX_DIM) ** -0.5),
        out_shape=jax.ShapeDtypeStruct((bsz * seq, q_w), BF16),
        grid=(bsz, nq),
        in_specs=[pl.BlockSpec((tq, q_w), lambda b, i: (b * nq + i, lay["q"] // q_w)),
                  pl.BlockSpec((seq, kv_w), lambda b, i: (b, lay["k"] // kv_w)),
                  pl.BlockSpec((seq, kv_w), lambda b, i: (b, lay["v"] // kv_w)),
                  pl.BlockSpec((tq, IDX_HEADS * IDX_DIM), lambda b, i: (b * nq + i, 0)),
                  pl.BlockSpec((tq, LANES), lambda b, i: (b * nq + i, lay["wi"] // LANES)),
                  pl.BlockSpec((seq, IDX_DIM), lambda b, i: (b, 0)),
                  pl.BlockSpec((seq, seq), lambda b, i: (0, 0))],
        out_specs=pl.BlockSpec((tq, q_w), lambda b, i: (b * nq + i, 0)),
        scratch_shapes=[pltpu.VMEM((tq, seq), F32), pltpu.VMEM((seq, kv_w), BF16), pltpu.VMEM((seq, kv_w), BF16),
                        pltpu.VMEM((seq, IDX_DIM), BF16)],
        compiler_params=_params("parallel", "arbitrary"),
        name="dsa_prompt",
    )(proj, proj, proj, q_idx, proj, k_idx, _strict_upper(seq))


PAGES_PER_STEP = 16


def _page_scores(qi_rows, w_rows, page_bf, t_new):
    d = jnp.maximum(_nt_dot(qi_rows, page_bf), 0.0) * w_rows
    return jnp.concatenate(
        [jnp.sum(d[t * IDX_HEADS:(t + 1) * IDX_HEADS], axis=0, keepdims=True) for t in range(t_new)], axis=0)


def _dsa_sample_scores_kernel(pt_ref, qi_ref, w_ref, *rest, t_new, page):
    pages = rest[:PAGES_PER_STEP]
    knew_ref, sc_ref, scn_ref = rest[PAGES_PER_STEP:]
    p = pl.program_id(1)
    qi = qi_ref[0]
    w = w_ref[0]
    for g in range(PAGES_PER_STEP):
        sc_ref[0, :, g * page:(g + 1) * page] = _page_scores(qi, w, pages[g][0, 0].astype(BF16), t_new)

    @pl.when(p == pl.num_programs(1) - 1)
    def _():
        sc = _page_scores(qi, w, knew_ref[0].astype(BF16), t_new)
        row = lax.broadcasted_iota(I32, sc.shape, 0)
        col = lax.broadcasted_iota(I32, sc.shape, 1)
        scn_ref[0] = jnp.where(col <= row, sc, NEG_INF)


def _dsa_sample_attn_kernel(pt_ref, sc_ref, scn_ref, q_ref, *rest, t_new, page, topk, n_heads):
    kp = rest[:PAGES_PER_STEP]
    vp = rest[PAGES_PER_STEP:2 * PAGES_PER_STEP]
    knew_ref, vnew_ref, tri_ref, o_ref, thr_s, need_s, cnt_s, m_s, l_s, acc_s, bias_s, flag_s = rest[2 * PAGES_PER_STEP:]
    p = pl.program_id(1)
    chunk = PAGES_PER_STEP * page
    rows = t_new * n_heads

    @pl.when(p == 0)
    def _():
        key_p = _sortable_key(sc_ref[0])
        key_n = _sortable_key(scn_ref[0])
        thr = _kth_largest_key([key_p, key_n], topk)
        n_gt = _count(key_p > thr) + _count(key_n > thr)
        n_eq = _count(key_p == thr) + _count(key_n == thr)
        need = float(topk) - n_gt
        thr_s[...] = jnp.broadcast_to(thr, thr_s.shape)
        need_s[...] = jnp.broadcast_to(need, need_s.shape)
        cnt_s[...] = jnp.zeros(cnt_s.shape, F32)
        flag_s[0] = (jnp.max(n_eq - need) > 0.0).astype(I32)
        m_s[...] = jnp.full(m_s.shape, NEG_INF, F32)
        l_s[...] = jnp.zeros(l_s.shape, F32)
        acc_s[...] = jnp.zeros(acc_s.shape, F32)

    thr = thr_s[:, 0:1]
    q = [q_ref[0, kv] for kv in range(N_KV_HEADS)]

    def select(sc, width):
        key = _sortable_key(sc)
        bias_s[:, :width] = jnp.where(key >= thr, 0.0, NEG_INF)

        @pl.when(flag_s[0] != 0)
        def _():
            eq = key == thr
            eqf = jnp.where(eq, 1.0, 0.0)
            rank = _dot(eqf.astype(BF16), tri_ref[:width, :width]) + cnt_s[:, 0:1]
            sel = (key > thr) | (eq & (rank < need_s[:, 0:1]))
            bias_s[:, :width] = jnp.where(sel, 0.0, NEG_INF)
            cnt_s[...] = cnt_s[...] + jnp.sum(eqf, axis=-1, keepdims=True)

    def attend(k_pages, v_pages, bias_t):
        bias = jnp.concatenate([jnp.broadcast_to(bias_t[t:t + 1], (n_heads, bias_t.shape[1])) for t in range(t_new)],
                               axis=0)
        s_pages = []
        for k_kv in k_pages:
            s = _nt_dot(q[0], k_kv[0])
            for kv in range(1, N_KV_HEADS):
                s = s + _nt_dot(q[kv], k_kv[kv])
            s_pages.append(s)
        s = jnp.concatenate(s_pages, axis=1) * (HEAD_DIM ** -0.5) + bias
        m_old = m_s[:, 0:1]
        m_new = jnp.maximum(m_old, jnp.max(s, axis=-1, keepdims=True))
        m_safe = jnp.where(m_new == NEG_INF, 0.0, m_new)
        alpha = jnp.exp(m_old - m_safe)
        pr = jnp.exp(s - m_safe)
        l_s[...] = alpha * l_s[...] + jnp.sum(pr, axis=-1, keepdims=True)
        pb = pr.astype(BF16)
        for kv in range(N_KV_HEADS):
            cols = slice(kv * HEAD_DIM, (kv + 1) * HEAD_DIM)
            pv = _dot(pb[:, 0:page], v_pages[0][kv])
            for g in range(1, len(v_pages)):
                pv = pv + _dot(pb[:, g * page:(g + 1) * page], v_pages[g][kv])
            acc_s[:, cols] = alpha * acc_s[:, cols] + pv
        m_s[...] = jnp.broadcast_to(m_new, m_s.shape)

    def page_heads(ref):
        return [ref[0, 0, :, kv, :].astype(BF16) for kv in range(N_KV_HEADS)]

    def row_heads(ref):
        return [ref[0, :, kv * HEAD_DIM:(kv + 1) * HEAD_DIM].astype(BF16) for kv in range(N_KV_HEADS)]

    start = pl.multiple_of(p * chunk, chunk)
    select(sc_ref[0, :, pl.ds(start, chunk)], chunk)
    attend([page_heads(r) for r in kp], [page_heads(r) for r in vp], bias_s[...])

    @pl.when(p == pl.num_programs(1) - 1)
    def _():
        select(scn_ref[0], page)
        attend([row_heads(knew_ref)], [row_heads(vnew_ref)], bias_s[:, :page])
        o_ref[0] = acc_s[...] / l_s[:, 0:1]


def _dsa_sample(q, k, v, q_idx, w_idx, k_idx, cache_k, cache_v, cache_kidx, page_table, layer):
    bsz, t_new, q_w = q.shape
    n_heads = q_w // HEAD_DIM
    kv_w = N_KV_HEADS * HEAD_DIM
    page = cache_k.shape[2]
    n_pages = page_table.shape[1]
    past = n_pages * page
    topk = min(TOPK_MAX, (past + t_new) // 4)
    steps = n_pages // PAGES_PER_STEP
    chunk = PAGES_PER_STEP * page
    rows = t_new * n_heads

    qi_rows = q_idx.reshape(bsz, t_new * IDX_HEADS, IDX_DIM).astype(BF16)
    w_rows = (w_idx * (IDX_HEADS * IDX_DIM) ** -0.5).reshape(bsz, t_new * IDX_HEADS, 1)
    pad_rows = lambda a: jnp.pad(a, ((0, 0), (0, page - t_new), (0, 0)))
    kidx_new, k_new, v_new = pad_rows(k_idx), pad_rows(k), pad_rows(v)

    def page_spec(trailing, g):
        zeros = (0,) * (1 + len(trailing))
        return pl.BlockSpec((1, 1, page) + trailing,
                            lambda b, p, pt: (layer, pt[b, p * PAGES_PER_STEP + g]) + zeros)

    per_b = lambda b, p, pt: (b, 0, 0)
    sc_past, sc_new = pl.pallas_call(
        functools.partial(_dsa_sample_scores_kernel, t_new=t_new, page=page),
        out_shape=(jax.ShapeDtypeStruct((bsz, t_new, past), F32),
                   jax.ShapeDtypeStruct((bsz, t_new, page), F32)),
        grid_spec=pltpu.PrefetchScalarGridSpec(
            num_scalar_prefetch=1,
            grid=(bsz, steps),
            in_specs=[pl.BlockSpec((1, t_new * IDX_HEADS, IDX_DIM), per_b),
                      pl.BlockSpec((1, t_new * IDX_HEADS, 1), per_b)]
                     + [page_spec((IDX_DIM,), g) for g in range(PAGES_PER_STEP)]
                     + [pl.BlockSpec((1, page, IDX_DIM), per_b)],
            out_specs=(pl.BlockSpec((1, t_new, chunk), lambda b, p, pt: (b, 0, p)),
                       pl.BlockSpec((1, t_new, page), per_b))),
        compiler_params=_params("parallel", "arbitrary"),
        name="dsa_sample_scores",
    )(page_table, qi_rows, w_rows, *([cache_kidx] * PAGES_PER_STEP), kidx_new)

    group = n_heads // N_KV_HEADS
    head_kv = jnp.arange(n_heads) // group
    blockdiag = (head_kv[:, None] == jnp.arange(N_KV_HEADS)[None, :]).astype(F32)
    q4 = q.reshape(bsz, t_new, n_heads, 1, HEAD_DIM) * blockdiag[None, None, :, :, None]
    q_bd = q4.transpose(0, 3, 1, 2, 4).reshape(bsz, N_KV_HEADS, rows, HEAD_DIM).astype(BF16)

    out = pl.pallas_call(
        functools.partial(_dsa_sample_attn_kernel, t_new=t_new, page=page, topk=topk, n_heads=n_heads),
        out_shape=jax.ShapeDtypeStruct((bsz, rows, kv_w), F32),
        grid_spec=pltpu.PrefetchScalarGridSpec(
            num_scalar_prefetch=1,
            grid=(bsz, steps),
            in_specs=[pl.BlockSpec((1, t_new, past), per_b),
                      pl.BlockSpec((1, t_new, page), per_b),
                      pl.BlockSpec((1, N_KV_HEADS, rows, HEAD_DIM), lambda b, p, pt: (b, 0, 0, 0))]
                     + [page_spec((N_KV_HEADS, HEAD_DIM), g) for g in range(PAGES_PER_STEP)]
                     + [page_spec((N_KV_HEADS, HEAD_DIM), g) for g in range(PAGES_PER_STEP)]
                     + [pl.BlockSpec((1, page, kv_w), per_b), pl.BlockSpec((1, page, kv_w), per_b),
                        pl.BlockSpec((chunk, chunk), lambda b, p, pt: (0, 0))],
            out_specs=pl.BlockSpec((1, rows, kv_w), per_b),
            scratch_shapes=[pltpu.VMEM((t_new, LANES), I32), pltpu.VMEM((t_new, LANES), F32),
                            pltpu.VMEM((t_new, LANES), F32), pltpu.VMEM((rows, LANES), F32),
                            pltpu.VMEM((rows, LANES), F32), pltpu.VMEM((rows, kv_w), F32),
                            pltpu.VMEM((t_new, chunk), F32), pltpu.SMEM((1,), I32)]),
        compiler_params=_params("parallel", "arbitrary"),
        name="dsa_sample_attn",
    )(page_table, sc_past, sc_new, q_bd, *([cache_k] * PAGES_PER_STEP), *([cache_v] * PAGES_PER_STEP),
      k_new, v_new, _strict_upper(chunk))
    o5 = out.reshape(bsz, t_new, N_KV_HEADS, group, N_KV_HEADS, HEAD_DIM)
    o = jnp.stack([o5[:, :, kv, :, kv, :] for kv in range(N_KV_HEADS)], axis=2)
    return o.reshape(bsz, t_new, q_w)


def _softplus(x):
    return jnp.maximum(x, 0.0) + jnp.log1p(jnp.exp(-jnp.abs(x)))


def _rglru_kernel(u_ref, g_ref, cp_ref, h0_ref, cw_ref, cb_ref, wa_ref, ba_ref, wx_ref, bx_ref, lam_ref,
                  y_ref, hl_ref, up_s, a_s, b_s, *, t_valid, t_chunk):
    t_len, wb = u_ref.shape
    lead = SUBLANES - (CONV_W - 1)
    up_s[0:SUBLANES, :] = jnp.zeros((SUBLANES, wb), F32)
    up_s[lead:SUBLANES, :] = cp_ref[0]
    up_s[SUBLANES:SUBLANES + t_len, :] = u_ref[...]
    neg_c_sp = -LRU_C * _softplus(-lam_ref[...])
    for c0 in range(0, t_len, t_chunk):
        xc = cw_ref[0:1, :] * up_s[lead + c0:lead + c0 + t_chunk, :]
        for j in range(1, CONV_W):
            xc = xc + cw_ref[j:j + 1, :] * up_s[lead + j + c0:lead + j + c0 + t_chunk, :]
        xc = cb_ref[...] + xc
        ra, ix = [], []
        for hd in range(wb // LANES):
            xh = xc[:, hd * LANES:(hd + 1) * LANES].astype(BF16)
            ra.append(_dot(xh, wa_ref[hd]))
            ix.append(_dot(xh, wx_ref[hd]))
        r = jax.nn.sigmoid(jnp.concatenate(ra, axis=-1) + ba_ref[...])
        gi = jax.nn.sigmoid(jnp.concatenate(ix, axis=-1) + bx_ref[...])
        log_a = r * neg_c_sp
        a = jnp.exp(log_a)
        a_s[c0:c0 + t_chunk, :] = a
        b_s[c0:c0 + t_chunk, :] = jnp.sqrt(-jnp.tanh(log_a) * (a * a + 1.0)) * (gi * xc)

    def step(t, h):
        h = a_s[pl.ds(t, 1), :] * h + b_s[pl.ds(t, 1), :]
        b_s[pl.ds(t, 1), :] = h
        return h

    lax.fori_loop(0, t_len, step, h0_ref[0], unroll=8)
    hl_ref[0] = b_s[t_valid - 1:t_valid, :]
    for c0 in range(0, t_len, t_chunk):
        rows = slice(c0, c0 + t_chunk)
        y_ref[rows, :] = (b_s[rows, :] * jax.nn.gelu(g_ref[rows, :])).astype(y_ref.dtype)


def _rglru(u_src, g_src, conv_prev, h0, conv_w, conv_b, wa, ba, wx, bx, lam, *, bsz, t_len, t_valid,
           u_col, g_col, width, wb):
    nb = width // wb
    heads_per = wb // LANES
    vec = lambda a: a.reshape(1, width)
    vspec = pl.BlockSpec((1, wb), lambda b, c: (0, c))
    t_chunk = min(t_len, 256)
    y, h_last = pl.pallas_call(
        functools.partial(_rglru_kernel, t_valid=t_valid, t_chunk=t_chunk),
        out_shape=(jax.ShapeDtypeStruct((bsz * t_len, width), BF16),
                   jax.ShapeDtypeStruct((bsz, 1, width), F32)),
        grid=(bsz, nb),
        in_specs=[pl.BlockSpec((t_len, wb), lambda b, c: (b, u_col // wb + c)),
                  pl.BlockSpec((t_len, wb), lambda b, c: (b, g_col // wb + c)),
                  pl.BlockSpec((1, CONV_W - 1, wb), lambda b, c: (b, 0, c)),
                  pl.BlockSpec((1, 1, wb), lambda b, c: (b, 0, c)),
                  pl.BlockSpec((CONV_W, wb), lambda b, c: (0, c)),
                  vspec,
                  pl.BlockSpec((heads_per, LANES, LANES), lambda b, c: (c, 0, 0)),
                  vspec,
                  pl.BlockSpec((heads_per, LANES, LANES), lambda b, c: (c, 0, 0)),
                  vspec, vspec],
        out_specs=(pl.BlockSpec((t_len, wb), lambda b, c: (b, c)),
                   pl.BlockSpec((1, 1, wb), lambda b, c: (b, 0, c))),
        scratch_shapes=[pltpu.VMEM((t_len + SUBLANES, wb), F32), pltpu.VMEM((t_len, wb), F32),
                        pltpu.VMEM((t_len, wb), F32)],
        compiler_params=_params("parallel", "arbitrary"),
        name="rglru",
    )(u_src, g_src, conv_prev, h0.reshape(bsz, 1, width), conv_w, vec(conv_b), wa.astype(BF16), vec(ba),
      wx.astype(BF16), vec(bx), vec(lam))
    return y, h_last.reshape(bsz, width)


def _s5_disc_kernel(are_ref, aim_ref, ls_ref, abr_ref, abi_ref, cfr_ref, cfi_ref):
    a_re, a_im = are_ref[...], aim_ref[...]
    dt = jnp.exp(ls_ref[...])
    mag = jnp.exp(a_re * dt)
    ab_re = mag * jnp.cos(a_im * dt)
    ab_im = mag * jnp.sin(a_im * dt)
    den = a_re * a_re + a_im * a_im
    nr, ni = ab_re - 1.0, ab_im
    abr_ref[...] = ab_re
    abi_ref[...] = ab_im
    cfr_ref[...] = (nr * a_re + ni * a_im) / den
    cfi_ref[...] = (ni * a_re - nr * a_im) / den


def _s5_discretize(a_re, a_im, log_step):
    g, n = a_re.shape
    shp = jax.ShapeDtypeStruct((g, n), F32)
    return pl.pallas_call(_s5_disc_kernel, out_shape=(shp, shp, shp, shp), name="s5_discretize")(
        a_re, a_im, jnp.broadcast_to(log_step[:, None], (g, n)))


def _s5_kernel(u_ref, wb_ref, wc_ref, are_ref, aim_ref, d_ref, h0r_ref, h0i_ref, z_ref, hr_ref, hi_ref, x_s, st_s,
               *, nj, t_chunk):
    c = pl.program_id(1)
    ns = are_ref.shape[2]

    @pl.when(c == 0)
    def _():
        st_s[0] = h0r_ref[0]
        st_s[1] = h0i_ref[0]

    u = u_ref[0]
    rowj = (lax.broadcasted_iota(I32, u.shape, 0) % SUBLANES) // (SUBLANES // nj)
    u_bf = u.astype(BF16)
    if nj > 1:
        zero = jnp.zeros(u_bf.shape, BF16)
        u_bf = jnp.concatenate([jnp.where(rowj == j, u_bf, zero) for j in range(nj)], axis=1)
    x_s[...] = _dot(u_bf, wb_ref[0])
    ar, ai = are_ref[0], aim_ref[0]

    def step(t, carry):
        xr, xi = carry
        r0 = pl.multiple_of(t * SUBLANES, SUBLANES)
        nr = ar * xr - ai * xi + x_s[pl.ds(r0, SUBLANES), 0:ns]
        ni = ar * xi + ai * xr + x_s[pl.ds(r0, SUBLANES), ns:2 * ns]
        x_s[pl.ds(r0, SUBLANES), 0:ns] = nr
        x_s[pl.ds(r0, SUBLANES), ns:2 * ns] = ni
        return nr, ni

    xr, xi = lax.fori_loop(0, t_chunk, step, (st_s[0], st_s[1]), unroll=min(t_chunk, 4))
    st_s[0] = xr
    st_s[1] = xi
    yb = _dot(x_s[...].astype(BF16), wc_ref[0])
    y = yb[:, 0:LANES]
    dvec = jnp.broadcast_to(d_ref[0, 0:1, :], y.shape)
    for j in range(1, nj):
        y = jnp.where(rowj == j, yb[:, j * LANES:(j + 1) * LANES], y)
        dvec = jnp.where(rowj == j, d_ref[0, j:j + 1, :], dvec)
    z_ref[0] = jax.nn.gelu(y + dvec * u).astype(z_ref.dtype)

    @pl.when(c == pl.num_programs(1) - 1)
    def _():
        hr_ref[0] = xr
        hi_ref[0] = xi


def _s5_scan(u, h0_re, h0_im, ab_re, ab_im, bb_re, bb_im, c_re, c_im, d_skip):
    bsz, t_len, d = u.shape
    groups, n_state = ab_re.shape
    gl = LANES // S5_GROUP
    ns = gl * n_state
    nj = SUBLANES // bsz
    nq = d // (LANES * nj)
    t_chunk = min(t_len, 256)
    eye_g = jnp.eye(gl, dtype=F32)

    u_rows = u.reshape(bsz, t_len, nq, nj, LANES).transpose(2, 1, 3, 0, 4).reshape(nq, t_len * SUBLANES, LANES)

    def b_weight(bb):
        w = bb.reshape(nq, nj, gl, n_state, S5_GROUP).transpose(0, 1, 2, 4, 3)
        w = w[:, :, :, :, None, :] * eye_g[None, None, :, None, :, None]
        return w.reshape(nq, nj * LANES, ns)

    w_b = jnp.concatenate([b_weight(bb_re), b_weight(bb_im)], axis=-1).astype(BF16)

    def c_weight(cc):
        w = cc.reshape(nq, nj, gl, S5_GROUP, n_state).transpose(0, 2, 4, 1, 3)
        w = w[:, :, :, :, None, :] * eye_g[None, :, None, None, :, None]
        return w.reshape(nq, ns, nj * LANES)

    w_c = jnp.concatenate([c_weight(c_re), -c_weight(c_im)], axis=1).astype(BF16)

    def coef(a):
        a = a.reshape(nq, nj, 1, ns)
        return jnp.broadcast_to(a, (nq, nj, bsz, ns)).reshape(nq, SUBLANES, ns)

    def state_in(h):
        return h.reshape(bsz, nq, nj, ns).transpose(1, 2, 0, 3).reshape(nq, SUBLANES, ns)

    def state_out(h):
        return h.reshape(nq, nj, bsz, ns).transpose(2, 0, 1, 3).reshape(bsz, groups, n_state)

    rows = t_chunk * SUBLANES
    per_q = lambda q, c: (q, 0, 0)
    st_spec = pl.BlockSpec((1, SUBLANES, ns), per_q)
    st_shape = jax.ShapeDtypeStruct((nq, SUBLANES, ns), F32)
    z, hr, hi = pl.pallas_call(
        functools.partial(_s5_kernel, nj=nj, t_chunk=t_chunk),
        out_shape=(jax.ShapeDtypeStruct((nq, t_len * SUBLANES, LANES), BF16), st_shape, st_shape),
        grid=(nq, t_len // t_chunk),
        in_specs=[pl.BlockSpec((1, rows, LANES), lambda q, c: (q, c, 0)),
                  pl.BlockSpec((1, nj * LANES, 2 * ns), per_q),
                  pl.BlockSpec((1, 2 * ns, nj * LANES), per_q),
                  st_spec, st_spec,
                  pl.BlockSpec((1, nj, LANES), per_q),
                  st_spec, st_spec],
        out_specs=(pl.BlockSpec((1, rows, LANES), lambda q, c: (q, c, 0)), st_spec, st_spec),
        scratch_shapes=[pltpu.VMEM((rows, 2 * ns), F32), pltpu.VMEM((2, SUBLANES, ns), F32)],
        compiler_params=_params("parallel", "arbitrary"),
        name="s5_scan",
    )(u_rows, w_b, w_c, coef(ab_re), coef(ab_im), d_skip.reshape(nq, nj, LANES), state_in(h0_re), state_in(h0_im))
    z = z.reshape(nq, t_len, nj, bsz, LANES).transpose(3, 1, 0, 2, 4).reshape(bsz, t_len, d)
    return z, state_out(hr), state_out(hi)


MOE_TM = 512
COMBINE_ROWS = 32
DISPATCH_ROWS = 128


def _moe_plan(ids, ranks, counts, n_tokens, n_experts):
    rows = ids.shape[0]
    n_tiles = (n_tokens * TOP_K) // MOE_TM + n_experts
    spare_tiles = -(-(rows - n_tokens) * TOP_K // MOE_TM)
    tiles_per = (counts + MOE_TM - 1) // MOE_TM
    tile_end = jnp.cumsum(tiles_per)
    run_start = (tile_end - tiles_per) * MOE_TM
    chose = ids[..., None] == jnp.arange(n_experts, dtype=I32)
    dest = jnp.sum(jnp.where(chose, run_start, 0), axis=-1) + ranks
    r = jnp.arange(rows, dtype=I32)[:, None]
    spare = n_tiles * MOE_TM + (r - n_tokens) * TOP_K + jnp.arange(TOP_K, dtype=I32)[None, :]
    is_token = r < n_tokens
    dest_dispatch = jnp.where(is_token, dest, spare).reshape(-1)
    dest_combine = jnp.where(is_token, dest, 0).reshape(-1)
    n_valid = tile_end[-1]
    tile_ids = jnp.arange(n_tiles + spare_tiles, dtype=I32)
    tile_expert = jnp.sum(tile_end[None, :] <= jnp.minimum(tile_ids[:n_tiles], n_valid - 1)[:, None], axis=1)
    has_tiles = tiles_per > 0
    is_last = jnp.any(has_tiles[None, :] & (tile_end[None, :] - 1 == tile_ids[:, None]), axis=1)
    zero_fill = (is_last | (tile_ids >= n_valid)).astype(I32)
    e_ids = jnp.arange(n_experts, dtype=I32)
    later = jnp.min(jnp.where(has_tiles[None, :] & (e_ids[None, :] > e_ids[:, None]), e_ids[None, :], n_experts),
                    axis=1)
    first = jnp.min(jnp.where(has_tiles, e_ids, n_experts))
    wraps = later == n_experts
    next_of = jnp.where(wraps, first, later)
    mine = tile_expert[:, None] == e_ids[None, :]
    next_expert = jnp.sum(jnp.where(mine, next_of[None, :], 0), axis=1)
    next_wraps = jnp.sum(jnp.where(mine, wraps[None, :].astype(I32), 0), axis=1)
    runs = (tile_expert.astype(I32), n_valid.astype(I32).reshape(1), next_expert.astype(I32),
            next_wraps.astype(I32))
    return dest_dispatch, dest_combine, runs, zero_fill, n_tiles, n_tiles + spare_tiles


def _dispatch_kernel(dest_ref, zf_ref, x_ref, o_hbm, zbuf, sem, zsem, *, total_tiles):
    t = pl.program_id(0)
    tr = x_ref.shape[0]

    def zero_copy(tile):
        start = pl.multiple_of(tile * MOE_TM, MOE_TM)
        return pltpu.make_async_copy(zbuf, o_hbm.at[pl.ds(start, MOE_TM), :], zsem)

    @pl.when(t == 0)
    def _():
        zbuf[...] = jnp.zeros(zbuf.shape, zbuf.dtype)

        def start(tile, c):
            @pl.when(zf_ref[tile] != 0)
            def _():
                zero_copy(tile).start()
            return c

        def wait(tile, c):
            @pl.when(zf_ref[tile] != 0)
            def _():
                zero_copy(tile).wait()
            return c

        lax.fori_loop(0, total_tiles, start, 0)
        lax.fori_loop(0, total_tiles, wait, 0)

    def body(r, c):
        for k in range(TOP_K):
            row = dest_ref[(t * tr + r) * TOP_K + k]
            pltpu.make_async_copy(x_ref.at[pl.ds(r, 1), :], o_hbm.at[pl.ds(row, 1), :], sem).start()
        return c

    lax.fori_loop(0, tr, body, 0)
    for k in range(TOP_K):
        pltpu.make_async_copy(x_ref, o_hbm.at[pl.ds(0, tr), :], sem).wait()


def _dispatch(x, dest, zero_fill, total_tiles):
    m, d = x.shape
    return pl.pallas_call(
        functools.partial(_dispatch_kernel, total_tiles=total_tiles),
        out_shape=jax.ShapeDtypeStruct((total_tiles * MOE_TM, d), x.dtype),
        grid_spec=pltpu.PrefetchScalarGridSpec(
            num_scalar_prefetch=2,
            grid=(m // DISPATCH_ROWS,),
            in_specs=[pl.BlockSpec((DISPATCH_ROWS, d), lambda t, dst, zf: (t, 0))],
            out_specs=pl.BlockSpec(memory_space=pl.ANY),
            scratch_shapes=[pltpu.VMEM((MOE_TM, d), x.dtype), pltpu.SemaphoreType.DMA(()),
                            pltpu.SemaphoreType.DMA(())]),
        compiler_params=_params("arbitrary"),
        name="moe_dispatch",
    )(dest, zero_fill, x)


def _new_expert(te_ref, t):
    return (t == 0) | (te_ref[t] != te_ref[jnp.maximum(t - 1, 0)])


def _run_weights(te_ref, nx_ref, wrap_ref, copies, n_col_blocks, cast):
    j = pl.program_id(0)
    t = pl.program_id(1)

    @pl.when((j == 0) & (t == 0))
    def _():
        for c in copies(te_ref[t], j):
            c.start()

    for c in copies(te_ref[t], j):
        c.wait()
    cast()
    next_j = j + wrap_ref[t]

    @pl.when(next_j < n_col_blocks)
    def _():
        for c in copies(nx_ref[t], next_j):
            c.start()


def _moe_gate_up_kernel(te_ref, nv_ref, nx_ref, wrap_ref, x_ref, w_hbm, bg_ref, bu_ref, a_ref, stage, wg_bf, wu_bf,
                        sem, *, k_chunk, layer, nb1):
    t = pl.program_id(1)
    tn1 = a_ref.shape[1]

    @pl.when(t < nv_ref[0])
    def _():
        d = x_ref.shape[1]

        def copies(e, jj):
            return [pltpu.make_async_copy(
                w_hbm.at[layer, e, :, pl.ds(pl.multiple_of((half * nb1 + jj) * tn1, tn1), tn1)],
                stage.at[half], sem.at[half]) for half in range(2)]

        def cast():
            for k0 in range(0, d, k_chunk):
                wg_bf[k0:k0 + k_chunk, :] = stage[0, k0:k0 + k_chunk, :].astype(BF16)
                wu_bf[k0:k0 + k_chunk, :] = stage[1, k0:k0 + k_chunk, :].astype(BF16)

        @pl.when(_new_expert(te_ref, t))
        def _():
            _run_weights(te_ref, nx_ref, wrap_ref, copies, nb1, cast)

        hg = jnp.zeros(a_ref.shape, F32)
        hu = jnp.zeros(a_ref.shape, F32)
        for k0 in range(0, d, k_chunk):
            xk = x_ref[:, k0:k0 + k_chunk].astype(BF16)
            hg = hg + _dot(xk, wg_bf[k0:k0 + k_chunk, :])
            hu = hu + _dot(xk, wu_bf[k0:k0 + k_chunk, :])
        gate = jnp.minimum(hg + bg_ref[0, 0], SWIGLU_LIMIT)
        up = jnp.clip(hu + bu_ref[0, 0], -SWIGLU_LIMIT, SWIGLU_LIMIT)
        a_ref[...] = ((up + 1.0) * gate * jax.nn.sigmoid(SWIGLU_ALPHA * gate)).astype(a_ref.dtype)

    @pl.when(t >= nv_ref[0])
    def _():
        a_ref[...] = jnp.zeros(a_ref.shape, a_ref.dtype)


def _moe_down_kernel(te_ref, nv_ref, nx_ref, wrap_ref, a_ref, w_hbm, b_ref, y_ref, stage, w_bf, sem,
                     *, k_chunk, layer, nb2):
    t = pl.program_id(1)
    tn2 = y_ref.shape[1]

    @pl.when(t < nv_ref[0])
    def _():
        f = a_ref.shape[1]

        def copies(e, jj):
            return [pltpu.make_async_copy(w_hbm.at[layer, e, :, pl.ds(pl.multiple_of(jj * tn2, tn2), tn2)],
                                          stage, sem)]

        def cast():
            for k0 in range(0, f, k_chunk):
                w_bf[k0:k0 + k_chunk, :] = stage[k0:k0 + k_chunk, :].astype(BF16)

        @pl.when(_new_expert(te_ref, t))
        def _():
            _run_weights(te_ref, nx_ref, wrap_ref, copies, nb2, cast)

        y_ref[...] = _dot(a_ref[...], w_bf[...]) + b_ref[0, 0]

    @pl.when(t >= nv_ref[0])
    def _():
        y_ref[...] = jnp.zeros(y_ref.shape, y_ref.dtype)


def _moe_experts(xs, runs, w_gu, b_gu, w_dn, b_dn, layer, n_tiles):
    _, n_experts, d, two_f = w_gu.shape
    f = two_f // 2
    tn1 = min(512, f)
    nb1 = f // tn1
    n_slots = n_tiles * MOE_TM
    b_gu4 = b_gu.reshape(b_gu.shape[0], n_experts, 1, two_f)
    last = lambda t, nv: jnp.minimum(t, nv[0] - 1)
    act = pl.pallas_call(
        functools.partial(_moe_gate_up_kernel, k_chunk=min(1024, d), layer=layer, nb1=nb1),
        out_shape=jax.ShapeDtypeStruct((n_slots, f), BF16),
        grid_spec=pltpu.PrefetchScalarGridSpec(
            num_scalar_prefetch=4,
            grid=(nb1, n_tiles),
            in_specs=[pl.BlockSpec((MOE_TM, d), lambda j, t, te, nv, nx, wr: (last(t, nv), 0)),
                      pl.BlockSpec(memory_space=pl.ANY),
                      pl.BlockSpec((1, 1, 1, tn1), lambda j, t, te, nv, nx, wr: (layer, te[t], 0, j)),
                      pl.BlockSpec((1, 1, 1, tn1), lambda j, t, te, nv, nx, wr: (layer, te[t], 0, nb1 + j))],
            out_specs=pl.BlockSpec((MOE_TM, tn1), lambda j, t, te, nv, nx, wr: (t, j)),
            scratch_shapes=[pltpu.VMEM((2, d, tn1), F32), pltpu.VMEM((d, tn1), BF16), pltpu.VMEM((d, tn1), BF16),
                            pltpu.SemaphoreType.DMA((2,))]),
        compiler_params=_params("arbitrary", "arbitrary"),
        name="moe_gate_up",
    )(*runs, xs, w_gu, b_gu4, b_gu4)

    tn2 = min(2048, d)
    nb2 = d // tn2
    return pl.pallas_call(
        functools.partial(_moe_down_kernel, k_chunk=min(512, f), layer=layer, nb2=nb2),
        out_shape=jax.ShapeDtypeStruct((n_slots, d), F32),
        grid_spec=pltpu.PrefetchScalarGridSpec(
            num_scalar_prefetch=4,
            grid=(nb2, n_tiles),
            in_specs=[pl.BlockSpec((MOE_TM, f), lambda j, t, te, nv, nx, wr: (last(t, nv), 0)),
                      pl.BlockSpec(memory_space=pl.ANY),
                      pl.BlockSpec((1, 1, 1, tn2), lambda j, t, te, nv, nx, wr: (layer, te[t], 0, j))],
            out_specs=pl.BlockSpec((MOE_TM, tn2), lambda j, t, te, nv, nx, wr: (t, j)),
            scratch_shapes=[pltpu.VMEM((f, tn2), F32), pltpu.VMEM((f, tn2), BF16), pltpu.SemaphoreType.DMA(())]),
        compiler_params=_params("arbitrary", "arbitrary"),
        name="moe_down",
    )(*runs, act, w_dn, b_dn.reshape(b_dn.shape[0], n_experts, 1, d))


def _combine_ln_kernel(dest_ref, ys_hbm, gate_ref, x_ref, g_ref, b_ref, o_ref, ob_ref, buf, sem, *, alpha):
    t = pl.program_id(0)
    nt = pl.num_programs(0)
    tr = o_ref.shape[0]

    def issue(tile, slot):
        def body(r, carry):
            for k in range(TOP_K):
                row = dest_ref[(tile * tr + r) * TOP_K + k]
                pltpu.make_async_copy(ys_hbm.at[pl.ds(row, 1), :], buf.at[slot, k, pl.ds(r, 1), :],
                                      sem.at[slot]).start()
            return carry
        lax.fori_loop(0, tr, body, 0)

    @pl.when(t == 0)
    def _():
        issue(0, 0)

    @pl.when(t + 1 < nt)
    def _():
        issue(t + 1, (t + 1) % 2)

    slot = t % 2
    for k in range(TOP_K):
        pltpu.make_async_copy(ys_hbm.at[pl.ds(0, tr), :], buf.at[slot, k], sem.at[slot]).wait()
    gates = gate_ref[...]
    f = gates[:, 0:1] * buf[slot, 0]
    for k in range(1, TOP_K):
        f = f + gates[:, k:k + 1] * buf[slot, k]
    y = _ln_rows(alpha * x_ref[...] + f, g_ref[...], b_ref[...])
    o_ref[...] = y
    ob_ref[...] = y.astype(ob_ref.dtype)


def _combine_ln(ys, dest, gates, x1, g, b, *, alpha):
    m, d = x1.shape
    row = lambda t, dst: (t, 0)
    fixed = lambda t, dst: (0, 0)
    return pl.pallas_call(
        functools.partial(_combine_ln_kernel, alpha=alpha),
        out_shape=(jax.ShapeDtypeStruct((m, d), F32), jax.ShapeDtypeStruct((m, d), BF16)),
        grid_spec=pltpu.PrefetchScalarGridSpec(
            num_scalar_prefetch=1,
            grid=(m // COMBINE_ROWS,),
            in_specs=[pl.BlockSpec(memory_space=pl.ANY),
                      pl.BlockSpec((COMBINE_ROWS, LANES), row),
                      pl.BlockSpec((COMBINE_ROWS, d), row),
                      pl.BlockSpec((1, d), fixed), pl.BlockSpec((1, d), fixed)],
            out_specs=(pl.BlockSpec((COMBINE_ROWS, d), row), pl.BlockSpec((COMBINE_ROWS, d), row)),
            scratch_shapes=[pltpu.VMEM((2, TOP_K, COMBINE_ROWS, d), F32), pltpu.SemaphoreType.DMA((2,))]),
        compiler_params=_params("arbitrary"),
        name="moe_combine_ln",
    )(dest, ys, gates, x1, g.reshape(1, d), b.reshape(1, d))


def _moe_block(x, mix, n_tokens, layer, ln1_g, ln1_b, ln2_g, ln2_b, rw, rb, w_gu, b_gu, w_dn, b_dn, *, alpha):
    n_experts = rw.shape[1]
    x1, ids, gates, ranks, counts = _ln_router(x, mix, ln1_g, ln1_b, rw, rb, alpha=alpha, tr=256, n_tokens=n_tokens)
    dest_d, dest_c, runs, zero_fill, n_tiles, total_tiles = _moe_plan(
        ids[:, :TOP_K], ranks[:, :TOP_K], counts[0, :n_experts].astype(I32), n_tokens, n_experts)
    xs = _dispatch(x1, dest_d, zero_fill, total_tiles)
    ys = _moe_experts(xs, runs, w_gu, b_gu, w_dn, b_dn, layer, n_tiles)
    return _combine_ln(ys, dest_c, gates, x1, ln2_g, ln2_b, alpha=alpha)


ROW_TILE = 768


def kernel(x_prompt, x_sample, cache_k, cache_v, cache_kidx, state_conv, state_lru, state_s5_re, state_s5_im,
           page_table, w_in, w_idx_qb, kidx_ln_g, kidx_ln_b, conv_w, conv_b, lru_wa, lru_ba, lru_wx, lru_bx,
           lru_lambda, w_out, s5_a_re, s5_a_im, s5_log_step, s5_b_re, s5_b_im, s5_c_re, s5_c_im, s5_d,
           glu_w1, glu_b1, glu_w2, glu_b2, ln1_g, ln1_b, ln2_g, ln2_b, router_w, router_b, moe_w_gu, moe_b_gu,
           moe_w_down, moe_b_down):
    bp, sp, dm = x_prompt.shape
    bs, ss, _ = x_sample.shape
    depth = ln1_g.shape[0]
    alpha = (2 * depth) ** 0.25
    n_p, n_s = bp * sp, bs * ss
    n_tok = n_p + n_s
    n_rows = -(-n_tok // ROW_TILE) * ROW_TILE
    lru_w = conv_w.shape[2]
    q_w = w_out.shape[1] - lru_w
    kv_w = N_KV_HEADS * HEAD_DIM
    s5_groups = dm // S5_GROUP

    def unify(a_p, a_s):
        pad = jnp.zeros((n_rows - n_tok, a_p.shape[-1]), a_p.dtype)
        return jnp.concatenate([a_p.reshape(n_p, -1), a_s.reshape(n_s, -1), pad], axis=0)

    x = unify(x_prompt, x_sample)
    xb = x.astype(BF16)
    outs = {name: [] for name in ("k_p", "k_s", "v_p", "v_s", "i_p", "i_s", "c_p", "c_s", "h_p", "h_s",
                                   "r_p", "r_s", "m_p", "m_s")}
    for layer in range(depth):
        j = layer // 2
        if layer % 2 == 0:
            sizes = [q_w, kv_w, kv_w, IDX_Q_RANK, IDX_HEADS, IDX_DIM, lru_w, lru_w]
            cuts = [0]
            for s_ in sizes:
                cuts.append(cuts[-1] + s_)
            seg = lambda i: w_in[j][:, cuts[i]:cuts[i + 1]]
            lay = {"q": 0, "q_w": q_w, "u": q_w, "g": q_w + lru_w, "k": q_w + 2 * lru_w,
                   "v": q_w + 2 * lru_w + kv_w, "ki": q_w + 2 * lru_w + 2 * kv_w}
            lay["wi"] = lay["ki"] + IDX_DIM
            n_used = lay["wi"] + IDX_HEADS
            n_proj = -(-n_used // 512) * 512
            w_main = jnp.concatenate([seg(0), seg(6), seg(7), seg(1), seg(2), seg(5), seg(4),
                                      jnp.zeros((dm, n_proj - n_used), F32)], axis=1).astype(BF16)
            proj = _matmul(xb, w_main, tm=ROW_TILE, tn=512, name="in_proj")
            cq = _matmul(xb, seg(3).astype(BF16), tm=ROW_TILE, tn=IDX_Q_RANK // 2, out_dtype=BF16, name="cq_proj")
            q_idx = _matmul(cq, w_idx_qb[j].astype(BF16), tm=ROW_TILE, tn=1024, out_dtype=BF16, name="q_idx_proj")
            k_idx = _layer_norm_cols(proj, kidx_ln_g[j], kidx_ln_b[j], tr=ROW_TILE, col_block=lay["ki"] // IDX_DIM,
                                     width=IDX_DIM)
            lru_args = (conv_w[j], conv_b[j], lru_wa[j], lru_ba[j], lru_wx[j], lru_bx[j], lru_lambda[j])

            att_p = _dsa_prompt(proj, q_idx, k_idx, lay, bsz=bp, seq=sp, tq=128)
            rec_p, h_p = _rglru(proj, proj, jnp.zeros((bp, CONV_W - 1, lru_w), F32), jnp.zeros((bp, lru_w), F32),
                                *lru_args, bsz=bp, t_len=sp, t_valid=sp, u_col=lay["u"], g_col=lay["g"],
                                width=lru_w, wb=512)
            col = lambda a, name, width: a[:, lay[name]:lay[name] + width]
            proj_p = proj[:n_p]
            outs["k_p"].append(col(proj_p, "k", kv_w).reshape(bp, sp, N_KV_HEADS, HEAD_DIM))
            outs["v_p"].append(col(proj_p, "v", kv_w).reshape(bp, sp, N_KV_HEADS, HEAD_DIM))
            outs["i_p"].append(k_idx[:n_p].reshape(bp, sp, IDX_DIM))
            tail = CONV_W - 1
            outs["c_p"].append(jnp.stack([proj[(b + 1) * sp - tail:(b + 1) * sp, lay["u"]:lay["u"] + lru_w]
                                          for b in range(bp)]))
            outs["h_p"].append(h_p)

            proj_s = proj[n_p:n_tok]
            s3 = lambda a: a.reshape(bs, ss, -1)
            q_s, k_s, v_s = s3(col(proj_s, "q", q_w)), s3(col(proj_s, "k", kv_w)), s3(col(proj_s, "v", kv_w))
            u_s, g_s = s3(col(proj_s, "u", lru_w)), s3(col(proj_s, "g", lru_w))
            wi_s = s3(col(proj_s, "wi", IDX_HEADS))
            kidx_s = s3(k_idx[n_p:n_tok])
            att_s = _dsa_sample(q_s, k_s, v_s, s3(q_idx[n_p:n_tok]), wi_s, kidx_s, cache_k, cache_v, cache_kidx,
                                page_table, j)
            t_pad = -(-ss // SUBLANES) * SUBLANES
            pad_t = lambda a: jnp.pad(a, ((0, 0), (0, t_pad - ss), (0, 0))).reshape(bs * t_pad, -1)
            rec_s, h_s = _rglru(pad_t(u_s), pad_t(g_s), state_conv[j], state_lru[j], *lru_args, bsz=bs, t_len=t_pad,
                                t_valid=ss, u_col=0, g_col=0, width=lru_w, wb=512)
            rec_s = rec_s.reshape(bs, t_pad, lru_w)[:, :ss]
            outs["k_s"].append(k_s.reshape(bs, ss, N_KV_HEADS, HEAD_DIM))
            outs["v_s"].append(v_s.reshape(bs, ss, N_KV_HEADS, HEAD_DIM))
            outs["i_s"].append(kidx_s)
            up_s = jnp.concatenate([state_conv[j], u_s], axis=1)
            outs["c_s"].append(up_s[:, ss:])
            outs["h_s"].append(h_s)

            att = unify(att_p, att_s.astype(BF16))
            rec = unify(rec_p, rec_s)
            mix = _matmul_2in(att, rec, w_out[j].astype(BF16), tm=ROW_TILE, tn=512, name="out_proj")
        else:
            ab_re, ab_im, cf_re, cf_im = _s5_discretize(s5_a_re[j], s5_a_im[j], s5_log_step[j])
            b_re, b_im = s5_b_re[j], s5_b_im[j]
            bb_re = cf_re[..., None] * b_re - cf_im[..., None] * b_im
            bb_im = cf_re[..., None] * b_im + cf_im[..., None] * b_re
            s5_args = (ab_re, ab_im, bb_re, bb_im, s5_c_re[j], s5_c_im[j], s5_d[j])
            n_state = ab_re.shape[1]
            zeros = jnp.zeros((bp, s5_groups, n_state), F32)
            z_p, hr, hi = _s5_scan(x[:n_p].reshape(bp, sp, dm), zeros, zeros, *s5_args)
            outs["r_p"].append(hr)
            outs["m_p"].append(hi)
            z_s, hr, hi = _s5_scan(x[n_p:n_tok].reshape(bs, ss, dm), state_s5_re[j], state_s5_im[j], *s5_args)
            outs["r_s"].append(hr)
            outs["m_s"].append(hi)
            mix = _glu(unify(z_p, z_s), glu_w1[j].astype(BF16), glu_b1[j], glu_w2[j].astype(BF16), glu_b2[j],
                       tm=ROW_TILE, tn=512)
        x, xb = _moe_block(x, mix, n_tok, layer, ln1_g[layer], ln1_b[layer], ln2_g[layer], ln2_b[layer],
                           router_w[layer], router_b[layer], moe_w_gu, moe_b_gu, moe_w_down, moe_b_down,
                           alpha=alpha)
    st = lambda name: jnp.stack(outs[name])
    return (x[:n_p].reshape(bp, sp, dm), x[n_p:n_tok].reshape(bs, ss, dm),
            st("k_p"), st("k_s"), st("v_p"), st("v_s"), st("i_p"), st("i_s"), st("c_p"), st("c_s"),
            st("h_p"), st("h_s"), st("r_p"), st("r_s"), st("m_p"), st("m_s"))
```

```python
import functools
import math

import jax
import jax.numpy as jnp
from jax import lax
from jax.experimental import pallas as pl
from jax.experimental.pallas import tpu as pltpu

F32 = jnp.float32
BF16 = jnp.bfloat16
I32 = jnp.int32

HEAD_DIM = 128
N_KV_HEADS = 4
IDX_HEADS = 32
IDX_DIM = 128
IDX_Q_RANK = 768
TOPK_MAX = 256
LRU_HEADS = 16
CONV_W = 4
LRU_C = 8.0
S5_GROUP = 16
TOP_K = 4
SWIGLU_LIMIT = 7.0
SWIGLU_ALPHA = 1.702
LN_EPS = 1e-5

LANES = 128
SUBLANES = 8
VMEM_LIMIT_BYTES = 56 * 1024 * 1024

NEG_INF = float("-inf")
INT_MIN = -(2 ** 31)


def _params(*sem):
    return pltpu.CompilerParams(dimension_semantics=sem, vmem_limit_bytes=VMEM_LIMIT_BYTES)


def _nt_dot(a, b):
    return lax.dot_general(a, b, (((1,), (1,)), ((), ())), preferred_element_type=F32)


def _dot(a, b):
    return jnp.dot(a, b, preferred_element_type=F32)


def _mm_kernel(x_ref, w_ref, o_ref):
    o_ref[...] = _dot(x_ref[...].astype(BF16), w_ref[...]).astype(o_ref.dtype)


def _matmul(x, w, *, tm, tn, out_dtype=F32, name="matmul"):
    m, k = x.shape
    n = w.shape[1]
    return pl.pallas_call(
        _mm_kernel,
        out_shape=jax.ShapeDtypeStruct((m, n), out_dtype),
        grid=(m // tm, n // tn),
        in_specs=[pl.BlockSpec((tm, k), lambda i, j: (i, 0)),
                  pl.BlockSpec((k, tn), lambda i, j: (0, j))],
        out_specs=pl.BlockSpec((tm, tn), lambda i, j: (i, j)),
        compiler_params=_params("parallel", "arbitrary"),
        name=name,
    )(x, w)


def _mm2_kernel(xa_ref, xb_ref, wa_ref, wb_ref, o_ref):
    o_ref[...] = _dot(xa_ref[...], wa_ref[...]) + _dot(xb_ref[...], wb_ref[...])


def _matmul_2in(xa, xb, w, *, tm, tn, name):
    m, ka = xa.shape
    n = w.shape[1]
    return pl.pallas_call(
        _mm2_kernel,
        out_shape=jax.ShapeDtypeStruct((m, n), F32),
        grid=(m // tm, n // tn),
        in_specs=[pl.BlockSpec((tm, ka), lambda i, j: (i, 0)),
                  pl.BlockSpec((tm, ka), lambda i, j: (i, 0)),
                  pl.BlockSpec((ka, tn), lambda i, j: (0, j)),
                  pl.BlockSpec((ka, tn), lambda i, j: (1, j))],
        out_specs=pl.BlockSpec((tm, tn), lambda i, j: (i, j)),
        compiler_params=_params("parallel", "arbitrary"),
        name=name,
    )(xa, xb, w, w)


def _glu_kernel(z_ref, w1_ref, b1_ref, w2_ref, b2_ref, o_ref):
    z = z_ref[...]
    a = _dot(z, w1_ref[...]) + b1_ref[...]
    b = _dot(z, w2_ref[...]) + b2_ref[...]
    o_ref[...] = a * jax.nn.sigmoid(b)


def _glu(z, w1, b1, w2, b2, *, tm, tn):
    m, k = z.shape
    n = w1.shape[1]
    x_spec = pl.BlockSpec((tm, k), lambda i, j: (i, 0))
    w_spec = pl.BlockSpec((k, tn), lambda i, j: (0, j))
    b_spec = pl.BlockSpec((1, tn), lambda i, j: (0, j))
    return pl.pallas_call(
        _glu_kernel,
        out_shape=jax.ShapeDtypeStruct((m, n), F32),
        grid=(m // tm, n // tn),
        in_specs=[x_spec, w_spec, b_spec, w_spec, b_spec],
        out_specs=pl.BlockSpec((tm, tn), lambda i, j: (i, j)),
        compiler_params=_params("parallel", "arbitrary"),
        name="glu",
    )(z, w1, b1.reshape(1, n), w2, b2.reshape(1, n))


def _ln_rows(x, g, b):
    mu = jnp.mean(x, axis=-1, keepdims=True)
    xc = x - mu
    var = jnp.mean(xc * xc, axis=-1, keepdims=True)
    return xc * lax.rsqrt(var + LN_EPS) * g + b


def _ln_kernel(x_ref, g_ref, b_ref, o_ref):
    o_ref[...] = _ln_rows(x_ref[...], g_ref[...], b_ref[...])


def _layer_norm_cols(x, g, b, *, tr, col_block, width):
    m = x.shape[0]
    return pl.pallas_call(
        _ln_kernel,
        out_shape=jax.ShapeDtypeStruct((m, width), F32),
        grid=(m // tr,),
        in_specs=[pl.BlockSpec((tr, width), lambda i: (i, col_block)),
                  pl.BlockSpec((1, width), lambda i: (0, 0)),
                  pl.BlockSpec((1, width), lambda i: (0, 0))],
        out_specs=pl.BlockSpec((tr, width), lambda i: (i, 0)),
        compiler_params=_params("parallel"),
        name="layer_norm",
    )(x, g.reshape(1, width), b.reshape(1, width))


def _top_gates(logits, n_experts):
    lane_f = lax.broadcasted_iota(I32, logits.shape, 1).astype(F32)
    l = jnp.where(lane_f < float(n_experts), logits, NEG_INF)
    vals, ids = [], []
    for _ in range(TOP_K):
        m = jnp.max(l, axis=-1, keepdims=True)
        idx = jnp.min(jnp.where(l == m, lane_f, float(LANES)), axis=-1, keepdims=True)
        vals.append(m)
        ids.append(idx)
        l = jnp.where(lane_f == idx, NEG_INF, l)
    es = [jnp.exp(v - vals[0]) for v in vals]
    den = es[0]
    for e in es[1:]:
        den = den + e
    return ids, [e / den for e in es]


def _ln_router_kernel(x_ref, mix_ref, g_ref, b_ref, rw_ref, rb_ref, tri_ref, o_ref, id_ref, gate_ref, rank_ref,
                      cnt_ref, *, alpha, n_experts, n_tokens):
    i = pl.program_id(0)
    tr = x_ref.shape[0]
    y = _ln_rows(alpha * x_ref[...] + mix_ref[...], g_ref[...], b_ref[...])
    o_ref[...] = y
    logits = _dot(y.astype(BF16), rw_ref[...]) + rb_ref[...]
    ids, gates = _top_gates(logits, n_experts)

    lane = lax.broadcasted_iota(I32, logits.shape, 1)
    lane_f = lane.astype(F32)
    valid = (i * tr + lax.broadcasted_iota(I32, logits.shape, 0)) < n_tokens
    chosen = jnp.zeros(logits.shape, F32)
    for r in range(TOP_K):
        chosen = chosen + jnp.where(lane_f == ids[r], 1.0, 0.0)
    chosen = jnp.where(valid, chosen, 0.0)

    @pl.when(i == 0)
    def _():
        cnt_ref[...] = jnp.zeros(cnt_ref.shape, F32)

    before = _dot(tri_ref[...], chosen.astype(BF16)) + cnt_ref[...]
    cnt_ref[...] = cnt_ref[...] + jnp.sum(chosen, axis=0, keepdims=True)

    idv = jnp.zeros(logits.shape, I32)
    gv = jnp.zeros(logits.shape, F32)
    rv = jnp.zeros(logits.shape, I32)
    for r in range(TOP_K):
        rank = jnp.sum(jnp.where(lane_f == ids[r], before, 0.0), axis=-1, keepdims=True)
        idv = jnp.where(lane == r, ids[r].astype(I32), idv)
        gv = jnp.where(lane == r, gates[r], gv)
        rv = jnp.where(lane == r, rank.astype(I32), rv)
    id_ref[...] = idv
    gate_ref[...] = gv
    rank_ref[...] = rv


def _ln_router(x, mix, g, b, rw, rb, *, alpha, tr, n_tokens):
    m, d = x.shape
    n_experts = rw.shape[1]
    rw_p = jnp.zeros((d, LANES), BF16).at[:, :n_experts].set(rw.astype(BF16))
    rb_p = jnp.zeros((1, LANES), F32).at[0, :n_experts].set(rb)
    tri = (jnp.arange(tr)[None, :] < jnp.arange(tr)[:, None]).astype(BF16)
    row = lambda i: (i, 0)
    fixed = lambda i: (0, 0)
    lanes_i = jax.ShapeDtypeStruct((m, LANES), I32)
    return pl.pallas_call(
        functools.partial(_ln_router_kernel, alpha=alpha, n_experts=n_experts, n_tokens=n_tokens),
        out_shape=(jax.ShapeDtypeStruct((m, d), F32), lanes_i, jax.ShapeDtypeStruct((m, LANES), F32), lanes_i,
                   jax.ShapeDtypeStruct((1, LANES), F32)),
        grid=(m // tr,),
        in_specs=[pl.BlockSpec((tr, d), row), pl.BlockSpec((tr, d), row),
                  pl.BlockSpec((1, d), fixed), pl.BlockSpec((1, d), fixed),
                  pl.BlockSpec((d, LANES), fixed), pl.BlockSpec((1, LANES), fixed),
                  pl.BlockSpec((tr, tr), fixed)],
        out_specs=(pl.BlockSpec((tr, d), row), pl.BlockSpec((tr, LANES), row), pl.BlockSpec((tr, LANES), row),
                   pl.BlockSpec((tr, LANES), row), pl.BlockSpec((1, LANES), fixed)),
        compiler_params=_params("arbitrary"),
        name="ln_router",
    )(x, mix, g.reshape(1, d), b.reshape(1, d), rw_p, rb_p, tri)


def _sortable_key(x):
    bits = pltpu.bitcast(x, I32)
    return jnp.where(bits < 0, bits ^ jnp.int32(0x7FFFFFFF), bits)


def _count(mask):
    return jnp.sum(jnp.where(mask, 1.0, 0.0), axis=-1, keepdims=True)


def _kth_largest_key(keys, k):
    def body(i, t):
        cand = t + lax.shift_left(jnp.int32(1), jnp.int32(31) - i)
        cnt = _count(keys[0] >= cand)
        for kk in keys[1:]:
            cnt = cnt + _count(kk >= cand)
        return jnp.where(cnt >= float(k), cand, t)

    t0 = jnp.full((keys[0].shape[0], 1), INT_MIN, I32)
    return lax.fori_loop(0, 32, body, t0)


def _strict_upper(n):
    return (jnp.arange(n)[:, None] < jnp.arange(n)[None, :]).astype(BF16)


KEY_CHUNK = 256
KEY_NEG_INF = -2139095041


def _dsa_prompt_kernel(q_ref, k_ref, v_ref, qi_ref, wi_ref, ki_ref, tri_ref, o_ref, bias_s, k_bf, v_bf, ki_bf,
                       *, topk, n_heads, w_scale):
    i = pl.program_id(1)
    tq = q_ref.shape[0]
    s_len = k_ref.shape[0]

    @pl.when(i == 0)
    def _():
        k_bf[...] = k_ref[...].astype(BF16)
        v_bf[...] = v_ref[...].astype(BF16)
        ki_bf[...] = ki_ref[...].astype(BF16)

    w = wi_ref[...][:, :IDX_HEADS] * w_scale
    qi_stack = jnp.concatenate([qi_ref[:, h * IDX_DIM:(h + 1) * IDX_DIM] for h in range(IDX_HEADS)], axis=0)
    pieces = []
    for c0 in range(0, s_len, KEY_CHUNK):
        d = _nt_dot(qi_stack, ki_bf[c0:c0 + KEY_CHUNK, :])
        sc_c = jnp.zeros((tq, KEY_CHUNK), F32)
        for h in range(IDX_HEADS):
            sc_c = sc_c + jnp.maximum(d[h * tq:(h + 1) * tq, :], 0.0) * w[:, h:h + 1]
        pieces.append(sc_c)
    sc = jnp.concatenate(pieces, axis=1)
    row = i * tq + lax.broadcasted_iota(I32, (tq, s_len), 0)
    col = lax.broadcasted_iota(I32, (tq, s_len), 1)
    causal = col <= row
    key = _sortable_key(jnp.where(causal, sc, NEG_INF))
    thr = _kth_largest_key([key], topk)
    gt = key > thr
    eq = key == thr
    need = float(topk) - _count(gt)
    bias_s[...] = jnp.where((key >= thr) & causal, 0.0, NEG_INF)
    tied = (thr > KEY_NEG_INF) & (_count(eq) > need)

    @pl.when(jnp.max(jnp.where(tied, 1.0, 0.0)) > 0.0)
    def _():
        rank = _dot(jnp.where(eq, 1.0, 0.0).astype(BF16), tri_ref[...])
        sel = (gt | (eq & (rank < need))) & causal
        bias_s[...] = jnp.where(sel, 0.0, NEG_INF)

    group = n_heads // N_KV_HEADS
    bias = jnp.concatenate([bias_s[...]] * group, axis=0)
    for kv in range(N_KV_HEADS):
        kh = k_bf[:, kv * HEAD_DIM:(kv + 1) * HEAD_DIM]
        vh = v_bf[:, kv * HEAD_DIM:(kv + 1) * HEAD_DIM]
        heads = range(kv * group, (kv + 1) * group)
        qs = jnp.concatenate([q_ref[:, hh * HEAD_DIM:(hh + 1) * HEAD_DIM] for hh in heads], axis=0).astype(BF16)
        s = _nt_dot(qs, kh) * (HEAD_DIM ** -0.5) + bias
        m = jnp.max(s, axis=-1, keepdims=True)
        p = jnp.exp(s - m)
        l = jnp.sum(p, axis=-1, keepdims=True)
        o = _dot(p.astype(BF16), vh) / l
        for g, hh in enumerate(heads):
            o_ref[:, hh * HEAD_DIM:(hh + 1) * HEAD_DIM] = o[g * tq:(g + 1) * tq, :].astype(o_ref.dtype)


def _dsa_prompt(proj, q_idx, k_idx, lay, *, bsz, seq, tq):
    topk = min(TOPK_MAX, seq // 4)
    q_w = lay["q_w"]
    kv_w = N_KV_HEADS * HEAD_DIM
    nq = seq // tq
    return pl.pallas_call(
        functools.partial(_dsa_prompt_kernel, topk=topk, n_heads=q_w // HEAD_DIM,
                          w_scale=(IDX_HEADS * IDX_DIM) ** -0.5),
        out_shape=jax.ShapeDtypeStruct((bsz * seq, q_w), BF16),
        grid=(bsz, nq),
        in_specs=[pl.BlockSpec((tq, q_w), lambda b, i: (b * nq + i, lay["q"] // q_w)),
                  pl.BlockSpec((seq, kv_w), lambda b, i: (b, lay["k"] // kv_w)),
                  pl.BlockSpec((seq, kv_w), lambda b, i: (b, lay["v"] // kv_w)),
                  pl.BlockSpec((tq, IDX_HEADS * IDX_DIM), lambda b, i: (b * nq + i, 0)),
                  pl.BlockSpec((tq, LANES), lambda b, i: (b * nq + i, lay["wi"] // LANES)),
                  pl.BlockSpec((seq, IDX_DIM), lambda b, i: (b, 0)),
                  pl.BlockSpec((seq, seq), lambda b, i: (0, 0))],
        out_specs=pl.BlockSpec((tq, q_w), lambda b, i: (b * nq + i, 0)),
        scratch_shapes=[pltpu.VMEM((tq, seq), F32), pltpu.VMEM((seq, kv_w), BF16), pltpu.VMEM((seq, kv_w), BF16),
                        pltpu.VMEM((seq, IDX_DIM), BF16)],
        compiler_params=_params("parallel", "arbitrary"),
        name="dsa_prompt",
    )(proj, proj, proj, q_idx, proj, k_idx, _strict_upper(seq))


PAGES_PER_STEP = 16


def _page_scores(qi_rows, w_rows, page_bf, t_new):
    d = jnp.maximum(_nt_dot(qi_rows, page_bf), 0.0) * w_rows
    return jnp.concatenate(
        [jnp.sum(d[t * IDX_HEADS:(t + 1) * IDX_HEADS], axis=0, keepdims=True) for t in range(t_new)], axis=0)


def _dsa_sample_scores_kernel(pt_ref, qi_ref, w_ref, *rest, t_new, page):
    pages = rest[:PAGES_PER_STEP]
    knew_ref, sc_ref, scn_ref = rest[PAGES_PER_STEP:]
    p = pl.program_id(1)
    qi = qi_ref[0]
    w = w_ref[0]
    for g in range(PAGES_PER_STEP):
        sc_ref[0, :, g * page:(g + 1) * page] = _page_scores(qi, w, pages[g][0, 0].astype(BF16), t_new)

    @pl.when(p == pl.num_programs(1) - 1)
    def _():
        sc = _page_scores(qi, w, knew_ref[0].astype(BF16), t_new)
        row = lax.broadcasted_iota(I32, sc.shape, 0)
        col = lax.broadcasted_iota(I32, sc.shape, 1)
        scn_ref[0] = jnp.where(col <= row, sc, NEG_INF)


def _dsa_sample_attn_kernel(pt_ref, sc_ref, scn_ref, q_ref, *rest, t_new, page, topk, n_heads):
    kp = rest[:PAGES_PER_STEP]
    vp = rest[PAGES_PER_STEP:2 * PAGES_PER_STEP]
    knew_ref, vnew_ref, tri_ref, o_ref, thr_s, need_s, cnt_s, m_s, l_s, acc_s, bias_s, flag_s = rest[2 * PAGES_PER_STEP:]
    p = pl.program_id(1)
    chunk = PAGES_PER_STEP * page
    rows = t_new * n_heads

    @pl.when(p == 0)
    def _():
        key_p = _sortable_key(sc_ref[0])
        key_n = _sortable_key(scn_ref[0])
        thr = _kth_largest_key([key_p, key_n], topk)
        n_gt = _count(key_p > thr) + _count(key_n > thr)
        n_eq = _count(key_p == thr) + _count(key_n == thr)
        need = float(topk) - n_gt
        thr_s[...] = jnp.broadcast_to(thr, thr_s.shape)
        need_s[...] = jnp.broadcast_to(need, need_s.shape)
        cnt_s[...] = jnp.zeros(cnt_s.shape, F32)
        flag_s[0] = (jnp.max(n_eq - need) > 0.0).astype(I32)
        m_s[...] = jnp.full(m_s.shape, NEG_INF, F32)
        l_s[...] = jnp.zeros(l_s.shape, F32)
        acc_s[...] = jnp.zeros(acc_s.shape, F32)

    thr = thr_s[:, 0:1]
    q = [q_ref[0, kv] for kv in range(N_KV_HEADS)]

    def select(sc, width):
        key = _sortable_key(sc)
        bias_s[:, :width] = jnp.where(key >= thr, 0.0, NEG_INF)

        @pl.when(flag_s[0] != 0)
        def _():
            eq = key == thr
            eqf = jnp.where(eq, 1.0, 0.0)
            rank = _dot(eqf.astype(BF16), tri_ref[:width, :width]) + cnt_s[:, 0:1]
            sel = (key > thr) | (eq & (rank < need_s[:, 0:1]))
            bias_s[:, :width] = jnp.where(sel, 0.0, NEG_INF)
            cnt_s[...] = cnt_s[...] + jnp.sum(eqf, axis=-1, keepdims=True)

    def attend(k_pages, v_pages, bias_t):
        bias = jnp.concatenate([jnp.broadcast_to(bias_t[t:t + 1], (n_heads, bias_t.shape[1])) for t in range(t_new)],
                               axis=0)
        s_pages = []
        for k_kv in k_pages:
            s = _nt_dot(q[0], k_kv[0])
            for kv in range(1, N_KV_HEADS):
                s = s + _nt_dot(q[kv], k_kv[kv])
            s_pages.append(s)
        s = jnp.concatenate(s_pages, axis=1) * (HEAD_DIM ** -0.5) + bias
        m_old = m_s[:, 0:1]
        m_new = jnp.maximum(m_old, jnp.max(s, axis=-1, keepdims=True))
        m_safe = jnp.where(m_new == NEG_INF, 0.0, m_new)
        alpha = jnp.exp(m_old - m_safe)
        pr = jnp.exp(s - m_safe)
        l_s[...] = alpha * l_s[...] + jnp.sum(pr, axis=-1, keepdims=True)
        pb = pr.astype(BF16)
        for kv in range(N_KV_HEADS):
            cols = slice(kv * HEAD_DIM, (kv + 1) * HEAD_DIM)
            pv = _dot(pb[:, 0:page], v_pages[0][kv])
            for g in range(1, len(v_pages)):
                pv = pv + _dot(pb[:, g * page:(g + 1) * page], v_pages[g][kv])
            acc_s[:, cols] = alpha * acc_s[:, cols] + pv
        m_s[...] = jnp.broadcast_to(m_new, m_s.shape)

    def page_heads(ref):
        return [ref[0, 0, :, kv, :].astype(BF16) for kv in range(N_KV_HEADS)]

    def row_heads(ref):
        return [ref[0, :, kv * HEAD_DIM:(kv + 1) * HEAD_DIM].astype(BF16) for kv in range(N_KV_HEADS)]

    start = pl.multiple_of(p * chunk, chunk)
    select(sc_ref[0, :, pl.ds(start, chunk)], chunk)
    attend([page_heads(r) for r in kp], [page_heads(r) for r in vp], bias_s[...])

    @pl.when(p == pl.num_programs(1) - 1)
    def _():
        select(scn_ref[0], page)
        attend([row_heads(knew_ref)], [row_heads(vnew_ref)], bias_s[:, :page])
        o_ref[0] = acc_s[...] / l_s[:, 0:1]


def _dsa_sample(q, k, v, q_idx, w_idx, k_idx, cache_k, cache_v, cache_kidx, page_table, layer):
    bsz, t_new, q_w = q.shape
    n_heads = q_w // HEAD_DIM
    kv_w = N_KV_HEADS * HEAD_DIM
    page = cache_k.shape[2]
    n_pages = page_table.shape[1]
    past = n_pages * page
    topk = min(TOPK_MAX, (past + t_new) // 4)
    steps = n_pages // PAGES_PER_STEP
    chunk = PAGES_PER_STEP * page
    rows = t_new * n_heads

    qi_rows = q_idx.reshape(bsz, t_new * IDX_HEADS, IDX_DIM).astype(BF16)
    w_rows = (w_idx * (IDX_HEADS * IDX_DIM) ** -0.5).reshape(bsz, t_new * IDX_HEADS, 1)
    pad_rows = lambda a: jnp.pad(a, ((0, 0), (0, page - t_new), (0, 0)))
    kidx_new, k_new, v_new = pad_rows(k_idx), pad_rows(k), pad_rows(v)

    def page_spec(trailing, g):
        zeros = (0,) * (1 + len(trailing))
        return pl.BlockSpec((1, 1, page) + trailing,
                            lambda b, p, pt: (layer, pt[b, p * PAGES_PER_STEP + g]) + zeros)

    per_b = lambda b, p, pt: (b, 0, 0)
    sc_past, sc_new = pl.pallas_call(
        functools.partial(_dsa_sample_scores_kernel, t_new=t_new, page=page),
        out_shape=(jax.ShapeDtypeStruct((bsz, t_new, past), F32),
                   jax.ShapeDtypeStruct((bsz, t_new, page), F32)),
        grid_spec=pltpu.PrefetchScalarGridSpec(
            num_scalar_prefetch=1,
            grid=(bsz, steps),
            in_specs=[pl.BlockSpec((1, t_new * IDX_HEADS, IDX_DIM), per_b),
                      pl.BlockSpec((1, t_new * IDX_HEADS, 1), per_b)]
                     + [page_spec((IDX_DIM,), g) for g in range(PAGES_PER_STEP)]
                     + [pl.BlockSpec((1, page, IDX_DIM), per_b)],
            out_specs=(pl.BlockSpec((1, t_new, chunk), lambda b, p, pt: (b, 0, p)),
                       pl.BlockSpec((1, t_new, page), per_b))),
        compiler_params=_params("parallel", "arbitrary"),
        name="dsa_sample_scores",
    )(page_table, qi_rows, w_rows, *([cache_kidx] * PAGES_PER_STEP), kidx_new)

    group = n_heads // N_KV_HEADS
    head_kv = jnp.arange(n_heads) // group
    blockdiag = (head_kv[:, None] == jnp.arange(N_KV_HEADS)[None, :]).astype(F32)
    q4 = q.reshape(bsz, t_new, n_heads, 1, HEAD_DIM) * blockdiag[None, None, :, :, None]
    q_bd = q4.transpose(0, 3, 1, 2, 4).reshape(bsz, N_KV_HEADS, rows, HEAD_DIM).astype(BF16)

    out = pl.pallas_call(
        functools.partial(_dsa_sample_attn_kernel, t_new=t_new, page=page, topk=topk, n_heads=n_heads),
        out_shape=jax.ShapeDtypeStruct((bsz, rows, kv_w), F32),
        grid_spec=pltpu.PrefetchScalarGridSpec(
            num_scalar_prefetch=1,
            grid=(bsz, steps),
            in_specs=[pl.BlockSpec((1, t_new, past), per_b),
                      pl.BlockSpec((1, t_new, page), per_b),
                      pl.BlockSpec((1, N_KV_HEADS, rows, HEAD_DIM), lambda b, p, pt: (b, 0, 0, 0))]
                     + [page_spec((N_KV_HEADS, HEAD_DIM), g) for g in range(PAGES_PER_STEP)]
                     + [page_spec((N_KV_HEADS, HEAD_DIM), g) for g in range(PAGES_PER_STEP)]
                     + [pl.BlockSpec((1, page, kv_w), per_b), pl.BlockSpec((1, page, kv_w), per_b),
                        pl.BlockSpec((chunk, chunk), lambda b, p, pt: (0, 0))],
            out_specs=pl.BlockSpec((1, rows, kv_w), per_b),
            scratch_shapes=[pltpu.VMEM((t_new, LANES), I32), pltpu.VMEM((t_new, LANES), F32),
                            pltpu.VMEM((t_new, LANES), F32), pltpu.VMEM((rows, LANES), F32),
                            pltpu.VMEM((rows, LANES), F32), pltpu.VMEM((rows, kv_w), F32),
                            pltpu.VMEM((t_new, chunk), F32), pltpu.SMEM((1,), I32)]),
        compiler_params=_params("parallel", "arbitrary"),
        name="dsa_sample_attn",
    )(page_table, sc_past, sc_new, q_bd, *([cache_k] * PAGES_PER_STEP), *([cache_v] * PAGES_PER_STEP),
      k_new, v_new, _strict_upper(chunk))
    o5 = out.reshape(bsz, t_new, N_KV_HEADS, group, N_KV_HEADS, HEAD_DIM)
    o = jnp.stack([o5[:, :, kv, :, kv, :] for kv in range(N_KV_HEADS)], axis=2)
    return o.reshape(bsz, t_new, q_w)


def _softplus(x):
    return jnp.maximum(x, 0.0) + jnp.log1p(jnp.exp(-jnp.abs(x)))


def _rglru_kernel(u_ref, g_ref, cp_ref, h0_ref, cw_ref, cb_ref, wa_ref, ba_ref, wx_ref, bx_ref, lam_ref,
                  y_ref, hl_ref, up_s, a_s, b_s, *, t_valid, t_chunk):
    t_len, wb = u_ref.shape
    lead = SUBLANES - (CONV_W - 1)
    up_s[0:SUBLANES, :] = jnp.zeros((SUBLANES, wb), F32)
    up_s[lead:SUBLANES, :] = cp_ref[0]
    up_s[SUBLANES:SUBLANES + t_len, :] = u_ref[...]
    neg_c_sp = -LRU_C * _softplus(-lam_ref[...])
    for c0 in range(0, t_len, t_chunk):
        xc = cw_ref[0:1, :] * up_s[lead + c0:lead + c0 + t_chunk, :]
        for j in range(1, CONV_W):
            xc = xc + cw_ref[j:j + 1, :] * up_s[lead + j + c0:lead + j + c0 + t_chunk, :]
        xc = cb_ref[...] + xc
        ra, ix = [], []
        for hd in range(wb // LANES):
            xh = xc[:, hd * LANES:(hd + 1) * LANES].astype(BF16)
            ra.append(_dot(xh, wa_ref[hd]))
            ix.append(_dot(xh, wx_ref[hd]))
        r = jax.nn.sigmoid(jnp.concatenate(ra, axis=-1) + ba_ref[...])
        gi = jax.nn.sigmoid(jnp.concatenate(ix, axis=-1) + bx_ref[...])
        log_a = r * neg_c_sp
        a = jnp.exp(log_a)
        a_s[c0:c0 + t_chunk, :] = a
        b_s[c0:c0 + t_chunk, :] = jnp.sqrt(-jnp.tanh(log_a) * (a * a + 1.0)) * (gi * xc)

    def step(t, h):
        h = a_s[pl.ds(t, 1), :] * h + b_s[pl.ds(t, 1), :]
        b_s[pl.ds(t, 1), :] = h
        return h

    lax.fori_loop(0, t_len, step, h0_ref[0], unroll=8)
    hl_ref[0] = b_s[t_valid - 1:t_valid, :]
    for c0 in range(0, t_len, t_chunk):
        rows = slice(c0, c0 + t_chunk)
        y_ref[rows, :] = (b_s[rows, :] * jax.nn.gelu(g_ref[rows, :])).astype(y_ref.dtype)


def _rglru(u_src, g_src, conv_prev, h0, conv_w, conv_b, wa, ba, wx, bx, lam, *, bsz, t_len, t_valid,
           u_col, g_col, width, wb):
    nb = width // wb
    heads_per = wb // LANES
    vec = lambda a: a.reshape(1, width)
    vspec = pl.BlockSpec((1, wb), lambda b, c: (0, c))
    t_chunk = min(t_len, 256)
    y, h_last = pl.pallas_call(
        functools.partial(_rglru_kernel, t_valid=t_valid, t_chunk=t_chunk),
        out_shape=(jax.ShapeDtypeStruct((bsz * t_len, width), BF16),
                   jax.ShapeDtypeStruct((bsz, 1, width), F32)),
        grid=(bsz, nb),
        in_specs=[pl.BlockSpec((t_len, wb), lambda b, c: (b, u_col // wb + c)),
                  pl.BlockSpec((t_len, wb), lambda b, c: (b, g_col // wb + c)),
                  pl.BlockSpec((1, CONV_W - 1, wb), lambda b, c: (b, 0, c)),
                  pl.BlockSpec((1, 1, wb), lambda b, c: (b, 0, c)),
                  pl.BlockSpec((CONV_W, wb), lambda b, c: (0, c)),
                  vspec,
                  pl.BlockSpec((heads_per, LANES, LANES), lambda b, c: (c, 0, 0)),
                  vspec,
                  pl.BlockSpec((heads_per, LANES, LANES), lambda b, c: (c, 0, 0)),
                  vspec, vspec],
        out_specs=(pl.BlockSpec((t_len, wb), lambda b, c: (b, c)),
                   pl.BlockSpec((1, 1, wb), lambda b, c: (b, 0, c))),
        scratch_shapes=[pltpu.VMEM((t_len + SUBLANES, wb), F32), pltpu.VMEM((t_len, wb), F32),
                        pltpu.VMEM((t_len, wb), F32)],
        compiler_params=_params("parallel", "arbitrary"),
        name="rglru",
    )(u_src, g_src, conv_prev, h0.reshape(bsz, 1, width), conv_w, vec(conv_b), wa.astype(BF16), vec(ba),
      wx.astype(BF16), vec(bx), vec(lam))
    return y, h_last.reshape(bsz, width)


def _s5_disc_kernel(are_ref, aim_ref, ls_ref, abr_ref, abi_ref, cfr_ref, cfi_ref):
    a_re, a_im = are_ref[...], aim_ref[...]
    dt = jnp.exp(ls_ref[...])
    mag = jnp.exp(a_re * dt)
    ab_re = mag * jnp.cos(a_im * dt)
    ab_im = mag * jnp.sin(a_im * dt)
    den = a_re * a_re + a_im * a_im
    nr, ni = ab_re - 1.0, ab_im
    abr_ref[...] = ab_re
    abi_ref[...] = ab_im
    cfr_ref[...] = (nr * a_re + ni * a_im) / den
    cfi_ref[...] = (ni * a_re - nr * a_im) / den


def _s5_discretize(a_re, a_im, log_step):
    g, n = a_re.shape
    shp = jax.ShapeDtypeStruct((g, n), F32)
    return pl.pallas_call(_s5_disc_kernel, out_shape=(shp, shp, shp, shp), name="s5_discretize")(
        a_re, a_im, jnp.broadcast_to(log_step[:, None], (g, n)))


def _s5_kernel(u_ref, wb_ref, wc_ref, are_ref, aim_ref, d_ref, h0r_ref, h0i_ref, z_ref, hr_ref, hi_ref, x_s, st_s,
               *, nj, t_chunk):
    c = pl.program_id(1)
    ns = are_ref.shape[2]

    @pl.when(c == 0)
    def _():
        st_s[0] = h0r_ref[0]
        st_s[1] = h0i_ref[0]

    u = u_ref[0]
    rowj = (lax.broadcasted_iota(I32, u.shape, 0) % SUBLANES) // (SUBLANES // nj)
    u_bf = u.astype(BF16)
    if nj > 1:
        zero = jnp.zeros(u_bf.shape, BF16)
        u_bf = jnp.concatenate([jnp.where(rowj == j, u_bf, zero) for j in range(nj)], axis=1)
    x_s[...] = _dot(u_bf, wb_ref[0])
    ar, ai = are_ref[0], aim_ref[0]

    def step(t, carry):
        xr, xi = carry
        r0 = pl.multiple_of(t * SUBLANES, SUBLANES)
        nr = ar * xr - ai * xi + x_s[pl.ds(r0, SUBLANES), 0:ns]
        ni = ar * xi + ai * xr + x_s[pl.ds(r0, SUBLANES), ns:2 * ns]
        x_s[pl.ds(r0, SUBLANES), 0:ns] = nr
        x_s[pl.ds(r0, SUBLANES), ns:2 * ns] = ni
        return nr, ni

    xr, xi = lax.fori_loop(0, t_chunk, step, (st_s[0], st_s[1]), unroll=min(t_chunk, 4))
    st_s[0] = xr
    st_s[1] = xi
    yb = _dot(x_s[...].astype(BF16), wc_ref[0])
    y = yb[:, 0:LANES]
    dvec = jnp.broadcast_to(d_ref[0, 0:1, :], y.shape)
    for j in range(1, nj):
        y = jnp.where(rowj == j, yb[:, j * LANES:(j + 1) * LANES], y)
        dvec = jnp.where(rowj == j, d_ref[0, j:j + 1, :], dvec)
    z_ref[0] = jax.nn.gelu(y + dvec * u).astype(z_ref.dtype)

    @pl.when(c == pl.num_programs(1) - 1)
    def _():
        hr_ref[0] = xr
        hi_ref[0] = xi


def _s5_scan(u, h0_re, h0_im, ab_re, ab_im, bb_re, bb_im, c_re, c_im, d_skip):
    bsz, t_len, d = u.shape
    groups, n_state = ab_re.shape
    gl = LANES // S5_GROUP
    ns = gl * n_state
    nj = SUBLANES // bsz
    nq = d // (LANES * nj)
    t_chunk = min(t_len, 256)
    eye_g = jnp.eye(gl, dtype=F32)

    u_rows = u.reshape(bsz, t_len, nq, nj, LANES).transpose(2, 1, 3, 0, 4).reshape(nq, t_len * SUBLANES, LANES)

    def b_weight(bb):
        w = bb.reshape(nq, nj, gl, n_state, S5_GROUP).transpose(0, 1, 2, 4, 3)
        w = w[:, :, :, :, None, :] * eye_g[None, None, :, None, :, None]
        return w.reshape(nq, nj * LANES, ns)

    w_b = jnp.concatenate([b_weight(bb_re), b_weight(bb_im)], axis=-1).astype(BF16)

    def c_weight(cc):
        w = cc.reshape(nq, nj, gl, S5_GROUP, n_state).transpose(0, 2, 4, 1, 3)
        w = w[:, :, :, :, None, :] * eye_g[None, :, None, None, :, None]
        return w.reshape(nq, ns, nj * LANES)

    w_c = jnp.concatenate([c_weight(c_re), -c_weight(c_im)], axis=1).astype(BF16)

    def coef(a):
        a = a.reshape(nq, nj, 1, ns)
        return jnp.broadcast_to(a, (nq, nj, bsz, ns)).reshape(nq, SUBLANES, ns)

    def state_in(h):
        return h.reshape(bsz, nq, nj, ns).transpose(1, 2, 0, 3).reshape(nq, SUBLANES, ns)

    def state_out(h):
        return h.reshape(nq, nj, bsz, ns).transpose(2, 0, 1, 3).reshape(bsz, groups, n_state)

    rows = t_chunk * SUBLANES
    per_q = lambda q, c: (q, 0, 0)
    st_spec = pl.BlockSpec((1, SUBLANES, ns), per_q)
    st_shape = jax.ShapeDtypeStruct((nq, SUBLANES, ns), F32)
    z, hr, hi = pl.pallas_call(
        functools.partial(_s5_kernel, nj=nj, t_chunk=t_chunk),
        out_shape=(jax.ShapeDtypeStruct((nq, t_len * SUBLANES, LANES), BF16), st_shape, st_shape),
        grid=(nq, t_len // t_chunk),
        in_specs=[pl.BlockSpec((1, rows, LANES), lambda q, c: (q, c, 0)),
                  pl.BlockSpec((1, nj * LANES, 2 * ns), per_q),
                  pl.BlockSpec((1, 2 * ns, nj * LANES), per_q),
                  st_spec, st_spec,
                  pl.BlockSpec((1, nj, LANES), per_q),
                  st_spec, st_spec],
        out_specs=(pl.BlockSpec((1, rows, LANES), lambda q, c: (q, c, 0)), st_spec, st_spec),
        scratch_shapes=[pltpu.VMEM((rows, 2 * ns), F32), pltpu.VMEM((2, SUBLANES, ns), F32)],
        compiler_params=_params("parallel", "arbitrary"),
        name="s5_scan",
    )(u_rows, w_b, w_c, coef(ab_re), coef(ab_im), d_skip.reshape(nq, nj, LANES), state_in(h0_re), state_in(h0_im))
    z = z.reshape(nq, t_len, nj, bsz, LANES).transpose(3, 1, 0, 2, 4).reshape(bsz, t_len, d)
    return z, state_out(hr), state_out(hi)


MOE_TM = 512
COMBINE_ROWS = 32
DISPATCH_ROWS = 128


def _moe_plan(ids, ranks, counts, n_tokens, n_experts):
    rows = ids.shape[0]
    n_tiles = (n_tokens * TOP_K) // MOE_TM + n_experts
    spare_tiles = -(-(rows - n_tokens) * TOP_K // MOE_TM)
    tiles_per = (counts + MOE_TM - 1) // MOE_TM
    tile_end = jnp.cumsum(tiles_per)
    run_start = (tile_end - tiles_per) * MOE_TM
    chose = ids[..., None] == jnp.arange(n_experts, dtype=I32)
    dest = jnp.sum(jnp.where(chose, run_start, 0), axis=-1) + ranks
    r = jnp.arange(rows, dtype=I32)[:, None]
    spare = n_tiles * MOE_TM + (r - n_tokens) * TOP_K + jnp.arange(TOP_K, dtype=I32)[None, :]
    is_token = r < n_tokens
    dest_dispatch = jnp.where(is_token, dest, spare).reshape(-1)
    dest_combine = jnp.where(is_token, dest, 0).reshape(-1)
    n_valid = tile_end[-1]
    tile_ids = jnp.arange(n_tiles + spare_tiles, dtype=I32)
    tile_expert = jnp.sum(tile_end[None, :] <= jnp.minimum(tile_ids[:n_tiles], n_valid - 1)[:, None], axis=1)
    has_tiles = tiles_per > 0
    is_last = jnp.any(has_tiles[None, :] & (tile_end[None, :] - 1 == tile_ids[:, None]), axis=1)
    zero_fill = (is_last | (tile_ids >= n_valid)).astype(I32)
    e_ids = jnp.arange(n_experts, dtype=I32)
    later = jnp.min(jnp.where(has_tiles[None, :] & (e_ids[None, :] > e_ids[:, None]), e_ids[None, :], n_experts),
                    axis=1)
    first = jnp.min(jnp.where(has_tiles, e_ids, n_experts))
    wraps = later == n_experts
    next_of = jnp.where(wraps, first, later)
    mine = tile_expert[:, None] == e_ids[None, :]
    next_expert = jnp.sum(jnp.where(mine, next_of[None, :], 0), axis=1)
    next_wraps = jnp.sum(jnp.where(mine, wraps[None, :].astype(I32), 0), axis=1)
    runs = (tile_expert.astype(I32), n_valid.astype(I32).reshape(1), next_expert.astype(I32),
            next_wraps.astype(I32))
    return dest_dispatch, dest_combine, runs, zero_fill, n_tiles, n_tiles + spare_tiles


def _dispatch_kernel(dest_ref, zf_ref, x_ref, o_hbm, zbuf, sem, zsem, *, total_tiles):
    t = pl.program_id(0)
    tr = x_ref.shape[0]

    def zero_copy(tile):
        start = pl.multiple_of(tile * MOE_TM, MOE_TM)
        return pltpu.make_async_copy(zbuf, o_hbm.at[pl.ds(start, MOE_TM), :], zsem)

    @pl.when(t == 0)
    def _():
        zbuf[...] = jnp.zeros(zbuf.shape, zbuf.dtype)

        def start(tile, c):
            @pl.when(zf_ref[tile] != 0)
            def _():
                zero_copy(tile).start()
            return c

        def wait(tile, c):
            @pl.when(zf_ref[tile] != 0)
            def _():
                zero_copy(tile).wait()
            return c

        lax.fori_loop(0, total_tiles, start, 0)
        lax.fori_loop(0, total_tiles, wait, 0)

    def body(r, c):
        for k in range(TOP_K):
            row = dest_ref[(t * tr + r) * TOP_K + k]
            pltpu.make_async_copy(x_ref.at[pl.ds(r, 1), :], o_hbm.at[pl.ds(row, 1), :], sem).start(priority=k % 2)
        return c

    lax.fori_loop(0, tr, body, 0)
    for k in range(TOP_K):
        pltpu.make_async_copy(x_ref, o_hbm.at[pl.ds(0, tr), :], sem).wait()


def _dispatch(x, dest, zero_fill, total_tiles):
    m, d = x.shape
    return pl.pallas_call(
        functools.partial(_dispatch_kernel, total_tiles=total_tiles),
        out_shape=jax.ShapeDtypeStruct((total_tiles * MOE_TM, d), x.dtype),
        grid_spec=pltpu.PrefetchScalarGridSpec(
            num_scalar_prefetch=2,
            grid=(m // DISPATCH_ROWS,),
            in_specs=[pl.BlockSpec((DISPATCH_ROWS, d), lambda t, dst, zf: (t, 0))],
            out_specs=pl.BlockSpec(memory_space=pl.ANY),
            scratch_shapes=[pltpu.VMEM((MOE_TM, d), x.dtype), pltpu.SemaphoreType.DMA(()),
                            pltpu.SemaphoreType.DMA(())]),
        compiler_params=_params("arbitrary"),
        name="moe_dispatch",
    )(dest, zero_fill, x)


def _new_expert(te_ref, t):
    return (t == 0) | (te_ref[t] != te_ref[jnp.maximum(t - 1, 0)])


def _run_weights(te_ref, nx_ref, wrap_ref, copies, n_col_blocks, cast):
    j = pl.program_id(0)
    t = pl.program_id(1)

    @pl.when((j == 0) & (t == 0))
    def _():
        for c in copies(te_ref[t], j):
            c.start()

    for c in copies(te_ref[t], j):
        c.wait()
    cast()
    next_j = j + wrap_ref[t]

    @pl.when(next_j < n_col_blocks)
    def _():
        for c in copies(nx_ref[t], next_j):
            c.start()


def _moe_gate_up_kernel(te_ref, nv_ref, nx_ref, wrap_ref, x_ref, w_hbm, bg_ref, bu_ref, a_ref, stage, wg_bf, wu_bf,
                        sem, *, k_chunk, layer, nb1):
    t = pl.program_id(1)
    tn1 = a_ref.shape[1]

    @pl.when(t < nv_ref[0])
    def _():
        d = x_ref.shape[1]

        def copies(e, jj):
            return [pltpu.make_async_copy(
                w_hbm.at[layer, e, :, pl.ds(pl.multiple_of((half * nb1 + jj) * tn1, tn1), tn1)],
                stage.at[half], sem.at[half]) for half in range(2)]

        def cast():
            for k0 in range(0, d, k_chunk):
                wg_bf[k0:k0 + k_chunk, :] = stage[0, k0:k0 + k_chunk, :].astype(BF16)
                wu_bf[k0:k0 + k_chunk, :] = stage[1, k0:k0 + k_chunk, :].astype(BF16)

        @pl.when(_new_expert(te_ref, t))
        def _():
            _run_weights(te_ref, nx_ref, wrap_ref, copies, nb1, cast)

        hg = jnp.zeros(a_ref.shape, F32)
        hu = jnp.zeros(a_ref.shape, F32)
        for k0 in range(0, d, k_chunk):
            xk = x_ref[:, k0:k0 + k_chunk].astype(BF16)
            hg = hg + _dot(xk, wg_bf[k0:k0 + k_chunk, :])
            hu = hu + _dot(xk, wu_bf[k0:k0 + k_chunk, :])
        gate = jnp.minimum(hg + bg_ref[0, 0], SWIGLU_LIMIT)
        up = jnp.clip(hu + bu_ref[0, 0], -SWIGLU_LIMIT, SWIGLU_LIMIT)
        a_ref[...] = ((up + 1.0) * gate * jax.nn.sigmoid(SWIGLU_ALPHA * gate)).astype(a_ref.dtype)

    @pl.when(t >= nv_ref[0])
    def _():
        a_ref[...] = jnp.zeros(a_ref.shape, a_ref.dtype)


def _moe_down_kernel(te_ref, nv_ref, nx_ref, wrap_ref, a_ref, w_hbm, b_ref, y_ref, stage, w_bf, sem,
                     *, k_chunk, layer, nb2):
    t = pl.program_id(1)
    tn2 = y_ref.shape[1]

    @pl.when(t < nv_ref[0])
    def _():
        f = a_ref.shape[1]

        def copies(e, jj):
            return [pltpu.make_async_copy(w_hbm.at[layer, e, :, pl.ds(pl.multiple_of(jj * tn2, tn2), tn2)],
                                          stage, sem)]

        def cast():
            for k0 in range(0, f, k_chunk):
                w_bf[k0:k0 + k_chunk, :] = stage[k0:k0 + k_chunk, :].astype(BF16)

        @pl.when(_new_expert(te_ref, t))
        def _():
            _run_weights(te_ref, nx_ref, wrap_ref, copies, nb2, cast)

        y_ref[...] = _dot(a_ref[...], w_bf[...]) + b_ref[0, 0]

    @pl.when(t >= nv_ref[0])
    def _():
        y_ref[...] = jnp.zeros(y_ref.shape, y_ref.dtype)


def _moe_experts(xs, runs, w_gu, b_gu, w_dn, b_dn, layer, n_tiles):
    _, n_experts, d, two_f = w_gu.shape
    f = two_f // 2
    tn1 = min(512, f)
    nb1 = f // tn1
    n_slots = n_tiles * MOE_TM
    b_gu4 = b_gu.reshape(b_gu.shape[0], n_experts, 1, two_f)
    last = lambda t, nv: jnp.minimum(t, nv[0] - 1)
    act = pl.pallas_call(
        functools.partial(_moe_gate_up_kernel, k_chunk=min(1024, d), layer=layer, nb1=nb1),
        out_shape=jax.ShapeDtypeStruct((n_slots, f), BF16),
        grid_spec=pltpu.PrefetchScalarGridSpec(
            num_scalar_prefetch=4,
            grid=(nb1, n_tiles),
            in_specs=[pl.BlockSpec((MOE_TM, d), lambda j, t, te, nv, nx, wr: (last(t, nv), 0)),
                      pl.BlockSpec(memory_space=pl.ANY),
                      pl.BlockSpec((1, 1, 1, tn1), lambda j, t, te, nv, nx, wr: (layer, te[t], 0, j)),
                      pl.BlockSpec((1, 1, 1, tn1), lambda j, t, te, nv, nx, wr: (layer, te[t], 0, nb1 + j))],
            out_specs=pl.BlockSpec((MOE_TM, tn1), lambda j, t, te, nv, nx, wr: (t, j)),
            scratch_shapes=[pltpu.VMEM((2, d, tn1), F32), pltpu.VMEM((d, tn1), BF16), pltpu.VMEM((d, tn1), BF16),
                            pltpu.SemaphoreType.DMA((2,))]),
        compiler_params=_params("arbitrary", "arbitrary"),
        name="moe_gate_up",
    )(*runs, xs, w_gu, b_gu4, b_gu4)

    tn2 = min(2048, d)
    nb2 = d // tn2
    return pl.pallas_call(
        functools.partial(_moe_down_kernel, k_chunk=min(512, f), layer=layer, nb2=nb2),
        out_shape=jax.ShapeDtypeStruct((n_slots, d), F32),
        grid_spec=pltpu.PrefetchScalarGridSpec(
            num_scalar_prefetch=4,
            grid=(nb2, n_tiles),
            in_specs=[pl.BlockSpec((MOE_TM, f), lambda j, t, te, nv, nx, wr: (last(t, nv), 0)),
                      pl.BlockSpec(memory_space=pl.ANY),
                      pl.BlockSpec((1, 1, 1, tn2), lambda j, t, te, nv, nx, wr: (layer, te[t], 0, j))],
            out_specs=pl.BlockSpec((MOE_TM, tn2), lambda j, t, te, nv, nx, wr: (t, j)),
            scratch_shapes=[pltpu.VMEM((f, tn2), F32), pltpu.VMEM((f, tn2), BF16), pltpu.SemaphoreType.DMA(())]),
        compiler_params=_params("arbitrary", "arbitrary"),
        name="moe_down",
    )(*runs, act, w_dn, b_dn.reshape(b_dn.shape[0], n_experts, 1, d))


def _combine_ln_kernel(dest_ref, ys_hbm, gate_ref, x_ref, g_ref, b_ref, o_ref, ob_ref, buf, sem, *, alpha):
    t = pl.program_id(0)
    nt = pl.num_programs(0)
    tr = o_ref.shape[0]

    def issue(tile, slot):
        def body(r, carry):
            for k in range(TOP_K):
                row = dest_ref[(tile * tr + r) * TOP_K + k]
                pltpu.make_async_copy(ys_hbm.at[pl.ds(row, 1), :], buf.at[slot, k, pl.ds(r, 1), :],
                                      sem.at[slot]).start(priority=k % 2)
            return carry
        lax.fori_loop(0, tr, body, 0)

    @pl.when(t == 0)
    def _():
        issue(0, 0)

    @pl.when(t + 1 < nt)
    def _():
        issue(t + 1, (t + 1) % 2)

    slot = t % 2
    for k in range(TOP_K):
        pltpu.make_async_copy(ys_hbm.at[pl.ds(0, tr), :], buf.at[slot, k], sem.at[slot]).wait()
    gates = gate_ref[...]
    f = gates[:, 0:1] * buf[slot, 0]
    for k in range(1, TOP_K):
        f = f + gates[:, k:k + 1] * buf[slot, k]
    y = _ln_rows(alpha * x_ref[...] + f, g_ref[...], b_ref[...])
    o_ref[...] = y
    ob_ref[...] = y.astype(ob_ref.dtype)


def _combine_ln(ys, dest, gates, x1, g, b, *, alpha):
    m, d = x1.shape
    row = lambda t, dst: (t, 0)
    fixed = lambda t, dst: (0, 0)
    return pl.pallas_call(
        functools.partial(_combine_ln_kernel, alpha=alpha),
        out_shape=(jax.ShapeDtypeStruct((m, d), F32), jax.ShapeDtypeStruct((m, d), BF16)),
        grid_spec=pltpu.PrefetchScalarGridSpec(
            num_scalar_prefetch=1,
            grid=(m // COMBINE_ROWS,),
            in_specs=[pl.BlockSpec(memory_space=pl.ANY),
                      pl.BlockSpec((COMBINE_ROWS, LANES), row),
                      pl.BlockSpec((COMBINE_ROWS, d), row),
                      pl.BlockSpec((1, d), fixed), pl.BlockSpec((1, d), fixed)],
            out_specs=(pl.BlockSpec((COMBINE_ROWS, d), row), pl.BlockSpec((COMBINE_ROWS, d), row)),
            scratch_shapes=[pltpu.VMEM((2, TOP_K, COMBINE_ROWS, d), F32), pltpu.SemaphoreType.DMA((2,))]),
        compiler_params=_params("arbitrary"),
        name="moe_combine_ln",
    )(dest, ys, gates, x1, g.reshape(1, d), b.reshape(1, d))


def _moe_block(x, mix, n_tokens, layer, ln1_g, ln1_b, ln2_g, ln2_b, rw, rb, w_gu, b_gu, w_dn, b_dn, *, alpha):
    n_experts = rw.shape[1]
    x1, ids, gates, ranks, counts = _ln_router(x, mix, ln1_g, ln1_b, rw, rb, alpha=alpha, tr=256, n_tokens=n_tokens)
    dest_d, dest_c, runs, zero_fill, n_tiles, total_tiles = _moe_plan(
        ids[:, :TOP_K], ranks[:, :TOP_K], counts[0, :n_experts].astype(I32), n_tokens, n_experts)
    xs = _dispatch(x1, dest_d, zero_fill, total_tiles)
    ys = _moe_experts(xs, runs, w_gu, b_gu, w_dn, b_dn, layer, n_tiles)
    return _combine_ln(ys, dest_c, gates, x1, ln2_g, ln2_b, alpha=alpha)


ROW_TILE = 768


def kernel(x_prompt, x_sample, cache_k, cache_v, cache_kidx, state_conv, state_lru, state_s5_re, state_s5_im,
           page_table, w_in, w_idx_qb, kidx_ln_g, kidx_ln_b, conv_w, conv_b, lru_wa, lru_ba, lru_wx, lru_bx,
           lru_lambda, w_out, s5_a_re, s5_a_im, s5_log_step, s5_b_re, s5_b_im, s5_c_re, s5_c_im, s5_d,
           glu_w1, glu_b1, glu_w2, glu_b2, ln1_g, ln1_b, ln2_g, ln2_b, router_w, router_b, moe_w_gu, moe_b_gu,
           moe_w_down, moe_b_down):
    bp, sp, dm = x_prompt.shape
    bs, ss, _ = x_sample.shape
    depth = ln1_g.shape[0]
    alpha = (2 * depth) ** 0.25
    n_p, n_s = bp * sp, bs * ss
    n_tok = n_p + n_s
    n_rows = -(-n_tok // ROW_TILE) * ROW_TILE
    lru_w = conv_w.shape[2]
    q_w = w_out.shape[1] - lru_w
    kv_w = N_KV_HEADS * HEAD_DIM
    s5_groups = dm // S5_GROUP

    def unify(a_p, a_s):
        pad = jnp.zeros((n_rows - n_tok, a_p.shape[-1]), a_p.dtype)
        return jnp.concatenate([a_p.reshape(n_p, -1), a_s.reshape(n_s, -1), pad], axis=0)

    x = unify(x_prompt, x_sample)
    xb = x.astype(BF16)
    outs = {name: [] for name in ("k_p", "k_s", "v_p", "v_s", "i_p", "i_s", "c_p", "c_s", "h_p", "h_s",
                                   "r_p", "r_s", "m_p", "m_s")}
    for layer in range(depth):
        j = layer // 2
        if layer % 2 == 0:
            sizes = [q_w, kv_w, kv_w, IDX_Q_RANK, IDX_HEADS, IDX_DIM, lru_w, lru_w]
            cuts = [0]
            for s_ in sizes:
                cuts.append(cuts[-1] + s_)
            seg = lambda i: w_in[j][:, cuts[i]:cuts[i + 1]]
            lay = {"q": 0, "q_w": q_w, "u": q_w, "g": q_w + lru_w, "k": q_w + 2 * lru_w,
                   "v": q_w + 2 * lru_w + kv_w, "ki": q_w + 2 * lru_w + 2 * kv_w}
            lay["wi"] = lay["ki"] + IDX_DIM
            n_used = lay["wi"] + IDX_HEADS
            n_proj = -(-n_used // 512) * 512
            w_main = jnp.concatenate([seg(0), seg(6), seg(7), seg(1), seg(2), seg(5), seg(4),
                                      jnp.zeros((dm, n_proj - n_used), F32)], axis=1).astype(BF16)
            proj = _matmul(xb, w_main, tm=ROW_TILE, tn=512, name="in_proj")
            cq = _matmul(xb, seg(3).astype(BF16), tm=ROW_TILE, tn=IDX_Q_RANK // 2, out_dtype=BF16, name="cq_proj")
            q_idx = _matmul(cq, w_idx_qb[j].astype(BF16), tm=ROW_TILE, tn=1024, out_dtype=BF16, name="q_idx_proj")
            k_idx = _layer_norm_cols(proj, kidx_ln_g[j], kidx_ln_b[j], tr=ROW_TILE, col_block=lay["ki"] // IDX_DIM,
                                     width=IDX_DIM)
            lru_args = (conv_w[j], conv_b[j], lru_wa[j], lru_ba[j], lru_wx[j], lru_bx[j], lru_lambda[j])

            att_p = _dsa_prompt(proj, q_idx, k_idx, lay, bsz=bp, seq=sp, tq=128)
            rec_p, h_p = _rglru(proj, proj, jnp.zeros((bp, CONV_W - 1, lru_w), F32), jnp.zeros((bp, lru_w), F32),
                                *lru_args, bsz=bp, t_len=sp, t_valid=sp, u_col=lay["u"], g_col=lay["g"],
                                width=lru_w, wb=512)
            col = lambda a, name, width: a[:, lay[name]:lay[name] + width]
            proj_p = proj[:n_p]
            outs["k_p"].append(col(proj_p, "k", kv_w).reshape(bp, sp, N_KV_HEADS, HEAD_DIM))
            outs["v_p"].append(col(proj_p, "v", kv_w).reshape(bp, sp, N_KV_HEADS, HEAD_DIM))
            outs["i_p"].append(k_idx[:n_p].reshape(bp, sp, IDX_DIM))
            tail = CONV_W - 1
            outs["c_p"].append(jnp.stack([proj[(b + 1) * sp - tail:(b + 1) * sp, lay["u"]:lay["u"] + lru_w]
                                          for b in range(bp)]))
            outs["h_p"].append(h_p)

            proj_s = proj[n_p:n_tok]
            s3 = lambda a: a.reshape(bs, ss, -1)
            q_s, k_s, v_s = s3(col(proj_s, "q", q_w)), s3(col(proj_s, "k", kv_w)), s3(col(proj_s, "v", kv_w))
            u_s, g_s = s3(col(proj_s, "u", lru_w)), s3(col(proj_s, "g", lru_w))
            wi_s = s3(col(proj_s, "wi", IDX_HEADS))
            kidx_s = s3(k_idx[n_p:n_tok])
            att_s = _dsa_sample(q_s, k_s, v_s, s3(q_idx[n_p:n_tok]), wi_s, kidx_s, cache_k, cache_v, cache_kidx,
                                page_table, j)
            t_pad = -(-ss // SUBLANES) * SUBLANES
            pad_t = lambda a: jnp.pad(a, ((0, 0), (0, t_pad - ss), (0, 0))).reshape(bs * t_pad, -1)
            rec_s, h_s = _rglru(pad_t(u_s), pad_t(g_s), state_conv[j], state_lru[j], *lru_args, bsz=bs, t_len=t_pad,
                                t_valid=ss, u_col=0, g_col=0, width=lru_w, wb=512)
            rec_s = rec_s.reshape(bs, t_pad, lru_w)[:, :ss]
            outs["k_s"].append(k_s.reshape(bs, ss, N_KV_HEADS, HEAD_DIM))
            outs["v_s"].append(v_s.reshape(bs, ss, N_KV_HEADS, HEAD_DIM))
            outs["i_s"].append(kidx_s)
            up_s = jnp.concatenate([state_conv[j], u_s], axis=1)
            outs["c_s"].append(up_s[:, ss:])
            outs["h_s"].append(h_s)

            att = unify(att_p, att_s.astype(BF16))
            rec = unify(rec_p, rec_s)
            mix = _matmul_2in(att, rec, w_out[j].astype(BF16), tm=ROW_TILE, tn=512, name="out_proj")
        else:
            ab_re, ab_im, cf_re, cf_im = _s5_discretize(s5_a_re[j], s5_a_im[j], s5_log_step[j])
            b_re, b_im = s5_b_re[j], s5_b_im[j]
            bb_re = cf_re[..., None] * b_re - cf_im[..., None] * b_im
            bb_im = cf_re[..., None] * b_im + cf_im[..., None] * b_re
            s5_args = (ab_re, ab_im, bb_re, bb_im, s5_c_re[j], s5_c_im[j], s5_d[j])
            n_state = ab_re.shape[1]
            zeros = jnp.zeros((bp, s5_groups, n_state), F32)
            z_p, hr, hi = _s5_scan(x[:n_p].reshape(bp, sp, dm), zeros, zeros, *s5_args)
            outs["r_p"].append(hr)
            outs["m_p"].append(hi)
            z_s, hr, hi = _s5_scan(x[n_p:n_tok].reshape(bs, ss, dm), state_s5_re[j], state_s5_im[j], *s5_args)
            outs["r_s"].append(hr)
            outs["m_s"].append(hi)
            mix = _glu(unify(z_p, z_s), glu_w1[j].astype(BF16), glu_b1[j], glu_w2[j].astype(BF16), glu_b2[j],
                       tm=ROW_TILE, tn=512)
        x, xb = _moe_block(x, mix, n_tok, layer, ln1_g[layer], ln1_b[layer], ln2_g[layer], ln2_b[layer],
                           router_w[layer], router_b[layer], moe_w_gu, moe_b_gu, moe_w_down, moe_b_down,
                           alpha=alpha)
    st = lambda name: jnp.stack(outs[name])
    return (x[:n_p].reshape(bp, sp, dm), x[n_p:n_tok].reshape(bs, ss, dm),
            st("k_p"), st("k_s"), st("v_p"), st("v_s"), st("i_p"), st("i_s"), st("c_p"), st("c_s"),
            st("h_p"), st("h_s"), st("r_p"), st("r_s"), st("m_p"), st("m_s"))
```
